```python
import math
import jax, jax.numpy as jnp
from jax import lax
import numpy as np

D_MODEL = 1024
BATCH = 16
SEQ = 2048
DEPTH = 1

HEAD_DIM = 64
N_HEADS_DIL = D_MODEL // (2 * HEAD_DIM)
N_HEADS_NA = D_MODEL // (2 * HEAD_DIM)
WIDTH_DIL = N_HEADS_DIL * HEAD_DIM
WIDTH_NA = N_HEADS_NA * HEAD_DIM
MIX_WIDTH = WIDTH_DIL + WIDTH_NA
DIL_PATTERNS = ((128, 1), (512, 4), (2048, 16))
DIL_BLOCK = 64
GRID_W = 64
NA_WIN_ROWS = 8
NA_WIN_COLS = 16
NA_Q_ROWS = 4
NA_Q_COLS = 16
NA_K_COLS = 32
N_GROUPS = 4
EXPERTS_PER_GROUP = 4
N_EXPERTS = N_GROUPS * EXPERTS_PER_GROUP
TOP_K_EXPERT = 2
D_EXPERT = 256
RMS_EPS = 1e-6
NEG_INF = -1e30

kernel_name = 'hybrid_dilated_neighbourhood_hmoe_encoder'


def rms_norm(x, gain):
    xf = x.astype(jnp.float32)
    xf = xf * lax.rsqrt(jnp.mean(xf * xf, axis=-1, keepdims=True) + RMS_EPS)
    return (xf * gain.astype(jnp.float32)).astype(x.dtype)


def alibi_slopes(n_heads):
    return jnp.exp2(-8.0 * jnp.arange(1, n_heads + 1, dtype=jnp.float32) / n_heads)


def dilated_window_attention(q, k, v, slopes, window, dilation):
    B, H, S, Dh = q.shape
    L = S // dilation
    radius = (window // 2) // dilation
    blk = DIL_BLOCK
    n_blk = -(-L // blk)
    Lp = n_blk * blk

    def residue_major(t, lo, hi):
        t = t.reshape(B, H, L, dilation, Dh).transpose(0, 1, 3, 2, 4)
        return jnp.pad(t, ((0, 0), (0, 0), (0, 0), (lo, hi), (0, 0)))

    def key_blocks(t):
        tp = residue_major(t, blk, Lp - L + blk).reshape(B, H, dilation, n_blk + 2, blk, Dh)
        return jnp.concatenate([tp[:, :, :, i:i + n_blk] for i in range(3)], axis=4)

    qb = residue_major(q, 0, Lp - L).reshape(B, H, dilation, n_blk, blk, Dh)
    kb = key_blocks(k)
    vb = key_blocks(v)

    q_pos = jnp.arange(Lp).reshape(n_blk, blk)
    k_pos = (jnp.arange(n_blk)[:, None] - 1) * blk + jnp.arange(3 * blk)[None, :]
    off = jnp.abs(k_pos[:, None, :] - q_pos[:, :, None])
    valid = ((off <= radius) & (k_pos[:, None, :] >= 0) & (k_pos[:, None, :] < L)
             & (q_pos[:, :, None] < L))
    dist = (off * dilation).astype(jnp.float32)

    s = jnp.einsum('bhrnqd,bhrnkd->bhrnqk', qb, kb).astype(jnp.float32) * (HEAD_DIM ** -0.5)
    s = s - slopes[None, :, None, None, None, None] * dist
    s = jnp.where(valid, s, NEG_INF)
    mx = jnp.max(s, axis=-1, keepdims=True)
    p = jnp.exp(s - mx)
    den = jnp.sum(p, axis=-1)
    o = jnp.einsum('bhrnqk,bhrnkd->bhrnqd', p, vb.astype(jnp.float32)) / den[..., None]
    lse = mx[..., 0] + jnp.log(den)

    o = o.reshape(B, H, dilation, Lp, Dh)[:, :, :, :L].transpose(0, 1, 3, 2, 4).reshape(B, H, S, Dh)
    lse = lse.reshape(B, H, dilation, Lp)[..., :L].transpose(0, 1, 3, 2).reshape(B, H, S)
    return o, lse


def dilated_mixture_attention(q, k, v, slopes):
    results = [dilated_window_attention(q, k, v, slopes, w, d) for (w, d) in DIL_PATTERNS]
    outs = jnp.stack([r[0] for r in results])
    lses = jnp.stack([r[1] for r in results])
    weights = jax.nn.softmax(lses, axis=0)
    return jnp.sum(weights[..., None] * outs, axis=0).astype(q.dtype)


def neighbourhood_attention(q, k, v, rpb):
    B, H, S, Dh = q.shape
    rows = S // GRID_W
    wh = min(NA_WIN_ROWS, rows)
    q_rows = math.gcd(rows, NA_Q_ROWS)
    k_rows = min(wh + q_rows - 1, rows)
    n_rb = rows // q_rows
    n_cb = GRID_W // NA_Q_COLS

    r = jnp.arange(rows)
    c = jnp.arange(GRID_W)
    row_start = jnp.clip(r - wh // 2, 0, rows - wh)
    col_start = jnp.clip(c - NA_WIN_COLS // 2, 0, GRID_W - NA_WIN_COLS)
    key_row = jnp.clip(row_start[::q_rows], 0, rows - k_rows)[:, None] + jnp.arange(k_rows)
    key_col = (jnp.clip(jnp.arange(n_cb) * NA_Q_COLS - NA_WIN_COLS // 2, 0, GRID_W - NA_K_COLS)[:, None]
               + jnp.arange(NA_K_COLS))

    q_row = r.reshape(n_rb, q_rows)
    q_col = c.reshape(n_cb, NA_Q_COLS)
    rs = row_start.reshape(n_rb, q_rows)[:, :, None]
    cs = col_start.reshape(n_cb, NA_Q_COLS)[:, :, None]
    row_ok = (key_row[:, None, :] >= rs) & (key_row[:, None, :] < rs + wh)
    col_ok = (key_col[:, None, :] >= cs) & (key_col[:, None, :] < cs + NA_WIN_COLS)
    mask = row_ok[:, None, :, None, :, None] & col_ok[None, :, None, :, None, :]
    dr = jnp.clip(key_row[:, None, :] - q_row[:, :, None] + NA_WIN_ROWS - 1, 0, 2 * NA_WIN_ROWS - 2)
    dc = jnp.clip(key_col[:, None, :] - q_col[:, :, None] + NA_WIN_COLS - 1, 0, 2 * NA_WIN_COLS - 2)
    bias = rpb[:, dr[:, None, :, None, :, None], dc[None, :, None, :, None, :]].astype(jnp.float32)

    def grid(t):
        return t.reshape(B, H, rows, GRID_W, Dh)

    qb = grid(q).reshape(B, H, n_rb, q_rows, n_cb, NA_Q_COLS, Dh).transpose(0, 1, 2, 4, 3, 5, 6)
    ridx = key_row[:, None, :, None]
    cidx = key_col[None, :, None, :]
    kg = grid(k)[:, :, ridx, cidx]
    vg = grid(v)[:, :, ridx, cidx]

    s = jnp.einsum('bhijqcd,bhijkld->bhijqckl', qb, kg).astype(jnp.float32) * (HEAD_DIM ** -0.5)
    s = jnp.where(mask, s + bias[None], NEG_INF)
    p = jax.nn.softmax(s, axis=(-2, -1))
    o = jnp.einsum('bhijqckl,bhijkld->bhijqcd', p, vg.astype(jnp.float32))
    o = o.transpose(0, 1, 2, 4, 3, 5, 6).reshape(B, H, S, Dh)
    return o.astype(q.dtype)


def hierarchical_moe(x, w_group, b_group, w_router, b_router, w_gate, w_up, w_down):
    B, S, D = x.shape
    n_tok = B * S
    xf = x.reshape(n_tok, D)
    g_logits = (xf @ w_group).astype(jnp.float32) + b_group.astype(jnp.float32)
    g_weight, g_idx = lax.top_k(jax.nn.softmax(g_logits, axis=-1), 1)
    e_logits = ((xf @ w_router).astype(jnp.float32) + b_router.astype(jnp.float32)
                ).reshape(n_tok, N_GROUPS, EXPERTS_PER_GROUP)
    e_in_group = jnp.take_along_axis(e_logits, g_idx[:, :, None], axis=1)[:, 0]
    e_val, e_idx = lax.top_k(e_in_group, TOP_K_EXPERT)
    e_weight = jax.nn.softmax(e_val, axis=-1) * g_weight
    expert_id = g_idx * EXPERTS_PER_GROUP + e_idx
    combine = jnp.sum(jax.nn.one_hot(expert_id, N_EXPERTS, dtype=jnp.float32) * e_weight[..., None], axis=1)
    gate = jnp.einsum('nd,edf->nef', xf, w_gate)
    up = jnp.einsum('nd,edf->nef', xf, w_up)
    act = jax.nn.silu(gate) * up * combine[:, :, None].astype(x.dtype)
    out = jnp.einsum('nef,efd->nd', act, w_down)
    return out.reshape(B, S, D).astype(x.dtype)


def setup_inputs(seed: int = 0) -> dict:
    key = jax.random.key(seed)
    ks = jax.random.split(key, 17)
    f32 = jnp.float32

    def nrm(k, shape, scale):
        return jax.random.normal(k, shape, f32) * scale

    return {
        'x': nrm(ks[0], (BATCH, SEQ, D_MODEL), 1.0),
        'norm_mix_g': 1.0 + nrm(ks[1], (DEPTH, D_MODEL), 0.02),
        'w_in': nrm(ks[2], (DEPTH, D_MODEL, 3 * MIX_WIDTH), D_MODEL ** -0.5),
        'rpb': nrm(ks[3], (DEPTH, N_HEADS_NA, 2 * NA_WIN_ROWS - 1, 2 * NA_WIN_COLS - 1), 0.1),
        'g_out_dil': 1.0 + nrm(ks[4], (DEPTH, WIDTH_DIL), 0.02),
        'g_out_na': 1.0 + nrm(ks[5], (DEPTH, WIDTH_NA), 0.02),
        'w_out': nrm(ks[6], (DEPTH, MIX_WIDTH, D_MODEL), MIX_WIDTH ** -0.5),
        'norm_ffn_g': 1.0 + nrm(ks[7], (DEPTH, D_MODEL), 0.02),
        'w_group': nrm(ks[8], (DEPTH, D_MODEL, N_GROUPS), D_MODEL ** -0.5),
        'b_group': nrm(ks[9], (DEPTH, N_GROUPS), 0.01),
        'w_router': nrm(ks[10], (DEPTH, D_MODEL, N_EXPERTS), D_MODEL ** -0.5),
        'b_router': nrm(ks[11], (DEPTH, N_EXPERTS), 0.01),
        'w_gate': nrm(ks[12], (DEPTH, N_EXPERTS, D_MODEL, D_EXPERT), D_MODEL ** -0.5),
        'w_up': nrm(ks[13], (DEPTH, N_EXPERTS, D_MODEL, D_EXPERT), D_MODEL ** -0.5),
        'w_down': nrm(ks[14], (DEPTH, N_EXPERTS, D_EXPERT, D_MODEL), D_EXPERT ** -0.5),
        'norm_final_g': 1.0 + nrm(ks[15], (D_MODEL,), 0.02),
    }


def reference(x, norm_mix_g, w_in, rpb, g_out_dil, g_out_na, w_out, norm_ffn_g,
              w_group, b_group, w_router, b_router, w_gate, w_up, w_down, norm_final_g):
    B, S, _ = x.shape
    slopes = alibi_slopes(N_HEADS_DIL)
    h = x
    for layer in range(DEPTH):
        hn = rms_norm(h, norm_mix_g[layer])
        proj = hn @ w_in[layer]
        p_dil = proj[..., :3 * WIDTH_DIL].reshape(B, S, 3, N_HEADS_DIL, HEAD_DIM).transpose(2, 0, 3, 1, 4)
        p_na = proj[..., 3 * WIDTH_DIL:].reshape(B, S, 3, N_HEADS_NA, HEAD_DIM).transpose(2, 0, 3, 1, 4)
        y_dil = dilated_mixture_attention(p_dil[0], p_dil[1], p_dil[2], slopes)
        y_na = neighbourhood_attention(p_na[0], p_na[1], p_na[2], rpb[layer])
        y_dil = rms_norm(y_dil.transpose(0, 2, 1, 3).reshape(B, S, WIDTH_DIL), g_out_dil[layer])
        y_na = rms_norm(y_na.transpose(0, 2, 1, 3).reshape(B, S, WIDTH_NA), g_out_na[layer])
        h = h + jnp.concatenate([y_dil, y_na], axis=-1) @ w_out[layer]
        h = h + hierarchical_moe(rms_norm(h, norm_ffn_g[layer]), w_group[layer], b_group[layer],
                                 w_router[layer], b_router[layer], w_gate[layer], w_up[layer], w_down[layer])
    return rms_norm(h, norm_final_g)
```

```python
import functools

import numpy as np
import jax
import jax.numpy as jnp
from jax import lax
from jax.experimental import pallas as pl
from jax.experimental.pallas import tpu as pltpu

D_MODEL = 1024
HEAD_DIM = 64
N_HEADS = 8
N_PAIRS = N_HEADS // 2
WIDTH = N_HEADS * HEAD_DIM
N_SLABS = 6 * N_PAIRS
DIL_PATTERNS = ((128, 1), (512, 4), (2048, 16))
DIL_RADIUS = 64
GRID_W = 64
NA_WIN_ROWS = 8
NA_WIN_COLS = 16
N_GROUPS = 4
EXPERTS_PER_GROUP = 4
N_EXPERTS = 16
D_EXPERT = 256
RMS_EPS = 1e-6
NEG_INF = -1e30

LANES = 128
VMEM_LIMIT = 48 * 1024 * 1024

F32 = jnp.float32
BF16 = jnp.bfloat16


def _rms(x, gain):
    return x * lax.rsqrt(jnp.mean(x * x, axis=-1, keepdims=True) + RMS_EPS) * gain


def _inproj_kernel(x_ref, g_ref, w_ref, o_ref):
    xn = _rms(x_ref[0], g_ref[...]).astype(BF16)
    chunk = 4 * LANES
    for c in range(N_SLABS * LANES // chunk):
        acc = jnp.dot(xn, w_ref[:, c * chunk:(c + 1) * chunk], preferred_element_type=F32)
        for j in range(chunk // LANES):
            o_ref[0, c * (chunk // LANES) + j] = acc[:, j * LANES:(j + 1) * LANES].astype(BF16)


def _inproj(x, gain, w_bf16, tm):
    B, S, D = x.shape
    return pl.pallas_call(
        _inproj_kernel,
        grid=(B, S // tm),
        in_specs=[
            pl.BlockSpec((1, tm, D), lambda b, i: (b, i, 0)),
            pl.BlockSpec((1, D), lambda b, i: (0, 0)),
            pl.BlockSpec((D, N_SLABS * LANES), lambda b, i: (0, 0)),
        ],
        out_specs=pl.BlockSpec((1, N_SLABS, tm, LANES), lambda b, i: (b, 0, i, 0)),
        out_shape=jax.ShapeDtypeStruct((B, N_SLABS, S, LANES), BF16),
        compiler_params=pltpu.CompilerParams(
            dimension_semantics=("arbitrary", "arbitrary"), vmem_limit_bytes=VMEM_LIMIT),
        name="inproj",
    )(x, gain, w_bf16)


def _attend(qb, kw, vw, bias):
    lane = lax.broadcasted_iota(jnp.int32, qb.shape, 1)
    zero = jnp.zeros_like(qb)
    qq = jnp.concatenate([jnp.where(lane < HEAD_DIM, qb, zero),
                          jnp.where(lane >= HEAD_DIM, qb, zero)], axis=0)
    s = lax.dot_general(qq, kw, (((1,), (1,)), ((), ())), preferred_element_type=F32) + bias
    m = jnp.max(s, axis=-1, keepdims=True)
    p = jnp.exp(s - m)
    l = jnp.sum(p, axis=-1, keepdims=True)
    pv = jnp.dot(p.astype(BF16), vw, preferred_element_type=F32)
    return m, l, pv


def _merge_heads(top, bottom, q):
    lane = lax.broadcasted_iota(jnp.int32, (q, LANES), 1)
    return jnp.where(lane < HEAD_DIM, jnp.broadcast_to(top, (q, LANES)),
                     jnp.broadcast_to(bottom, (q, LANES)))


_DQ = 128
_DW = 256


def _dilated_tables():
    slopes = 2.0 ** (-(np.arange(N_HEADS) + 1.0))
    q = np.arange(_DQ)[:, None]
    t12 = np.zeros((N_PAIRS, 2, 3, 2 * _DQ, _DW), np.float32)
    k = np.arange(_DW)[None, :]
    for pat, dil in enumerate((1, 4)):
        for case, off in enumerate((0, _DW // 4, _DW // 2)):
            delta = np.abs(k - (q + off))
            for h in range(N_HEADS):
                tab = np.where(delta <= DIL_RADIUS, -slopes[h] * dil * delta, NEG_INF)
                t12[h // 2, pat, case, (h % 2) * _DQ:(h % 2 + 1) * _DQ] = tab
    t3 = np.zeros((N_PAIRS, 2 * _DQ, _DQ), np.float32)
    delta = np.abs(np.arange(_DQ)[None, :] - q)
    for h in range(N_HEADS):
        t3[h // 2, (h % 2) * _DQ:(h % 2 + 1) * _DQ] = np.where(
            delta <= DIL_RADIUS, -slopes[h] * 16 * delta, NEG_INF)
    return t12, t3


def _dilated_kernel(q_ref, k_ref, v_ref, t12_ref, t3_ref, o_ref,
                    tmp, q4, k4, v4, q16, k16, v16,
                    m1, l1, a1, m2, l2, a2, m3, l3, a3, onat):
    S = tmp.shape[0]
    L4, L16 = S // 4, S // 16

    for src, d4, d16 in ((q_ref, q4, q16), (k_ref, k4, k16), (v_ref, v4, v16)):
        tmp[...] = src[0, 0].astype(F32)
        for r in range(4):
            d4[r * L4:(r + 1) * L4, :] = tmp[pl.ds(r, L4, stride=4), :].astype(BF16)
        for r in range(16):
            d16[r * L16:(r + 1) * L16, :] = tmp[pl.ds(r, L16, stride=16), :].astype(BF16)

    def block(qb, kw, vw, bias, m_ref, l_ref, a_ref, row):
        m, l, pv = _attend(qb, kw, vw, bias)
        m_ref[pl.ds(row, _DQ), :] = _merge_heads(m[:_DQ], m[_DQ:], _DQ)
        l_ref[pl.ds(row, _DQ), :] = _merge_heads(l[:_DQ], l[_DQ:], _DQ)
        a_ref[pl.ds(row, _DQ), :] = _merge_heads(pv[:_DQ], pv[_DQ:], _DQ)

    def case_of(blk, n_blk):
        return jnp.where(blk == 0, 0, jnp.where(blk == n_blk - 1, 2, 1))

    n1 = S // _DQ

    def p1_body(blk, carry):
        t0 = pl.multiple_of(blk * _DQ, _DQ)
        ws = pl.multiple_of(jnp.clip(t0 - DIL_RADIUS, 0, S - _DW), DIL_RADIUS)
        block(q_ref[0, 0, pl.ds(t0, _DQ), :], k_ref[0, 0, pl.ds(ws, _DW), :],
              v_ref[0, 0, pl.ds(ws, _DW), :], t12_ref[0, 0, case_of(blk, n1)], m1, l1, a1, t0)
        return carry

    lax.fori_loop(0, n1, p1_body, 0)

    n2 = L4 // _DQ

    def p2_body(j, carry):
        r = j // n2
        blk = j % n2
        l0 = blk * _DQ
        ws = jnp.clip(l0 - DIL_RADIUS, 0, L4 - _DW)
        row = pl.multiple_of(r * L4 + l0, _DQ)
        krow = pl.multiple_of(r * L4 + ws, DIL_RADIUS)
        block(q4[pl.ds(row, _DQ), :], k4[pl.ds(krow, _DW), :], v4[pl.ds(krow, _DW), :],
              t12_ref[0, 1, case_of(blk, n2)], m2, l2, a2, row)
        return carry

    lax.fori_loop(0, 4 * n2, p2_body, 0)

    def p3_body(r, carry):
        row = pl.multiple_of(r * L16, L16)
        block(q16[pl.ds(row, L16), :], k16[pl.ds(row, L16), :], v16[pl.ds(row, L16), :],
              t3_ref[0], m3, l3, a3, row)
        return carry

    lax.fori_loop(0, 16, p3_body, 0)

    for r16 in range(16):
        r4, c4 = r16 % 4, r16 // 4
        nat = pl.ds(r16, L16, stride=16)
        via4 = pl.ds(r4 * L4 + c4, L16, stride=4)
        via16 = pl.ds(r16 * L16, L16)
        ma, mb, mc = m1[nat, :], m2[via4, :], m3[via16, :]
        mx = jnp.maximum(jnp.maximum(ma, mb), mc)
        wa, wb, wc = jnp.exp(ma - mx), jnp.exp(mb - mx), jnp.exp(mc - mx)
        den = wa * l1[nat, :] + wb * l2[via4, :] + wc * l3[via16, :]
        num = wa * a1[nat, :] + wb * a2[via4, :] + wc * a3[via16, :]
        onat[nat, :] = num / den
    o_ref[0] = onat[...].astype(BF16)


def _dilated_attention(qkv, t12, t3):
    B, _, S, _ = qkv.shape
    f32_buf = pltpu.VMEM((S, LANES), F32)
    bf16_buf = pltpu.VMEM((S, LANES), BF16)
    slab = lambda off: pl.BlockSpec((1, 1, S, LANES), lambda p, b: (b, off + p, 0, 0))
    return pl.pallas_call(
        _dilated_kernel,
        grid=(N_PAIRS, B),
        in_specs=[
            slab(0), slab(N_PAIRS), slab(2 * N_PAIRS),
            pl.BlockSpec((1, 2, 3, 2 * _DQ, _DW), lambda p, b: (p, 0, 0, 0, 0)),
            pl.BlockSpec((1, 2 * _DQ, _DQ), lambda p, b: (p, 0, 0)),
        ],
        out_specs=pl.BlockSpec((1, S, LANES), lambda p, b: (b, 0, p)),
        out_shape=jax.ShapeDtypeStruct((B, S, WIDTH), BF16),
        scratch_shapes=[f32_buf] + [bf16_buf] * 6 + [f32_buf] * 10,
        compiler_params=pltpu.CompilerParams(
            dimension_semantics=("arbitrary", "arbitrary"), vmem_limit_bytes=VMEM_LIMIT),
        name="dilated_attn",
    )(qkv, qkv, qkv, t12, t3)


_NQ_ROWS = 4
_NK_ROWS = 12


def _na_index_tables(rows):
    n_blk = rows // _NQ_ROWS
    tabs = []
    for i in range(n_blk):
        kr0 = min(max(_NQ_ROWS * i - NA_WIN_ROWS // 2, 0), rows - _NK_ROWS)
        qr = (_NQ_ROWS * i + np.arange(_NQ_ROWS))[:, None, None, None]
        qc = np.arange(GRID_W)[None, :, None, None]
        kr = (kr0 + np.arange(_NK_ROWS))[None, None, :, None]
        kc = np.arange(GRID_W)[None, None, None, :]
        rs = np.clip(qr - NA_WIN_ROWS // 2, 0, rows - NA_WIN_ROWS)
        cs = np.clip(qc - NA_WIN_COLS // 2, 0, GRID_W - NA_WIN_COLS)
        mask = (kr >= rs) & (kr < rs + NA_WIN_ROWS) & (kc >= cs) & (kc < cs + NA_WIN_COLS)
        dr = np.clip(kr - qr + NA_WIN_ROWS - 1, 0, 2 * NA_WIN_ROWS - 2)
        dc = np.clip(kc - qc + NA_WIN_COLS - 1, 0, 2 * NA_WIN_COLS - 2)
        shape = (_NQ_ROWS * GRID_W, _NK_ROWS * GRID_W)
        full = (_NQ_ROWS, GRID_W, _NK_ROWS, GRID_W)
        tabs.append(tuple(np.broadcast_to(a, full).reshape(shape) for a in (mask, dr, dc)))
    for i in range(2, n_blk - 1):
        assert all(np.array_equal(a, b) for a, b in zip(tabs[1], tabs[i]))
    pick = (0, 1, n_blk - 1)
    return tuple(np.stack([tabs[i][j] for i in pick]) for j in range(3))


def _na_kernel(q_ref, k_ref, v_ref, tab_ref, o_ref):
    S = q_ref.shape[2]
    rows = S // GRID_W
    n_blk = rows // _NQ_ROWS
    nq = _NQ_ROWS * GRID_W
    nk = _NK_ROWS * GRID_W

    def body(i, carry):
        q0 = pl.multiple_of(i * nq, nq)
        kr0 = jnp.clip(_NQ_ROWS * i - NA_WIN_ROWS // 2, 0, rows - _NK_ROWS)
        k0 = pl.multiple_of(kr0 * GRID_W, GRID_W)
        case = jnp.where(i == 0, 0, jnp.where(i == n_blk - 1, 2, 1))
        m, l, pv = _attend(q_ref[0, 0, pl.ds(q0, nq), :], k_ref[0, 0, pl.ds(k0, nk), :],
                           v_ref[0, 0, pl.ds(k0, nk), :], tab_ref[0, case])
        o = pv / l
        lane = lax.broadcasted_iota(jnp.int32, (nq, LANES), 1)
        o_ref[0, pl.ds(q0, nq), :] = jnp.where(lane < HEAD_DIM, o[:nq], o[nq:]).astype(BF16)
        return carry

    lax.fori_loop(0, n_blk, body, 0)


def _na_attention(qkv, tab):
    B, _, S, _ = qkv.shape
    slab = lambda off: pl.BlockSpec((1, 1, S, LANES), lambda p, b: (b, off + p, 0, 0))
    nq, nk = _NQ_ROWS * GRID_W, _NK_ROWS * GRID_W
    return pl.pallas_call(
        _na_kernel,
        grid=(N_PAIRS, B),
        in_specs=[
            slab(3 * N_PAIRS), slab(4 * N_PAIRS), slab(5 * N_PAIRS),
            pl.BlockSpec((1, 3, 2 * nq, nk), lambda p, b: (p, 0, 0, 0)),
        ],
        out_specs=pl.BlockSpec((1, S, LANES), lambda p, b: (b, 0, p)),
        out_shape=jax.ShapeDtypeStruct((B, S, WIDTH), BF16),
        compiler_params=pltpu.CompilerParams(
            dimension_semantics=("arbitrary", "arbitrary"), vmem_limit_bytes=VMEM_LIMIT),
        name="na_attn",
    )(qkv, qkv, qkv, tab)


_GROUP_LANE0 = N_EXPERTS


def _route(logits):
    lane = lax.broadcasted_iota(jnp.int32, logits.shape, 1)
    big = jnp.int32(LANES)
    is_group = (lane >= _GROUP_LANE0) & (lane < _GROUP_LANE0 + N_GROUPS)
    gl = jnp.where(is_group, logits, NEG_INF)
    gmax = jnp.max(gl, axis=-1, keepdims=True)
    g_idx = jnp.min(jnp.where(is_group & (gl == gmax), lane, big), axis=-1, keepdims=True) - _GROUP_LANE0
    g_weight = 1.0 / jnp.sum(jnp.where(is_group, jnp.exp(gl - gmax), 0.0), axis=-1, keepdims=True)
    in_group = (lane < N_EXPERTS) & ((lane // EXPERTS_PER_GROUP) == g_idx)
    el = jnp.where(in_group, logits, NEG_INF)
    v1 = jnp.max(el, axis=-1, keepdims=True)
    i1 = jnp.min(jnp.where(in_group & (el == v1), lane, big), axis=-1, keepdims=True)
    rest = in_group & (lane != i1)
    el2 = jnp.where(rest, logits, NEG_INF)
    v2 = jnp.max(el2, axis=-1, keepdims=True)
    i2 = jnp.min(jnp.where(rest & (el2 == v2), lane, big), axis=-1, keepdims=True)
    e2 = jnp.exp(v2 - v1)
    w1 = g_weight / (1.0 + e2)
    w2 = g_weight * e2 / (1.0 + e2)
    return jnp.where(lane == i1, w1, jnp.where(lane == i2, w2, 0.0))


def _outproj_kernel(x_ref, yd_ref, yn_ref, gd_ref, gn_ref, wo_ref, gf_ref, wr_hi_ref, wr_lo_ref,
                    br_ref, h_ref, hn_ref, comb_ref):
    yd = _rms(yd_ref[...].astype(F32), gd_ref[...]).astype(BF16)
    yn = _rms(yn_ref[...].astype(F32), gn_ref[...]).astype(BF16)
    h = (x_ref[...] + jnp.dot(yd, wo_ref[:WIDTH, :], preferred_element_type=F32)
         + jnp.dot(yn, wo_ref[WIDTH:, :], preferred_element_type=F32))
    h_ref[...] = h
    hn = _rms(h, gf_ref[...])
    hi = hn.astype(BF16)
    lo = (hn - hi.astype(F32)).astype(BF16)
    hn_ref[...] = hi
    logits = (jnp.dot(hi, wr_hi_ref[...], preferred_element_type=F32)
              + (jnp.dot(lo, wr_hi_ref[...], preferred_element_type=F32)
                 + jnp.dot(hi, wr_lo_ref[...], preferred_element_type=F32))
              + br_ref[...])
    comb_ref[...] = _route(logits)


def _outproj(x2, yd2, yn2, gd, gn, wo, gf, wr_hi, wr_lo, br, tm):
    N, D = x2.shape
    row = lambda w: pl.BlockSpec((tm, w), lambda i: (i, 0))
    full = lambda a, b: pl.BlockSpec((a, b), lambda i: (0, 0))
    return pl.pallas_call(
        _outproj_kernel,
        grid=(N // tm,),
        in_specs=[row(D), row(WIDTH), row(WIDTH), full(1, WIDTH), full(1, WIDTH), full(2 * WIDTH, D),
                  full(1, D), full(D, LANES), full(D, LANES), full(1, LANES)],
        out_specs=[row(D), row(D), row(LANES)],
        out_shape=[jax.ShapeDtypeStruct((N, D), F32), jax.ShapeDtypeStruct((N, D), BF16),
                   jax.ShapeDtypeStruct((N, LANES), F32)],
        compiler_params=pltpu.CompilerParams(
            dimension_semantics=("arbitrary",), vmem_limit_bytes=VMEM_LIMIT),
        name="outproj_route",
    )(x2, yd2, yn2, gd, gn, wo, gf, wr_hi, wr_lo, br)


def _moe_kernel(hn_ref, comb_ref, h_ref, wg_ref, wu_ref, wd_ref, gfin_ref, y_ref, acc_ref):
    e = pl.program_id(1)

    @pl.when(e == 0)
    def _():
        acc_ref[...] = jnp.zeros_like(acc_ref)

    hn = hn_ref[...]
    gate = jnp.dot(hn, wg_ref[0], preferred_element_type=F32)
    up = jnp.dot(hn, wu_ref[0], preferred_element_type=F32)
    comb = comb_ref[...]
    lane = lax.broadcasted_iota(jnp.int32, comb.shape, 1)
    c = jnp.sum(jnp.where(lane == e, comb, 0.0), axis=-1, keepdims=True)
    act = (gate / (1.0 + jnp.exp(-gate))) * up * c
    acc_ref[...] += jnp.dot(act.astype(BF16), wd_ref[0], preferred_element_type=F32)

    @pl.when(e == pl.num_programs(1) - 1)
    def _():
        y_ref[...] = _rms(h_ref[...] + acc_ref[...], gfin_ref[...])


def _moe(hn, comb, h, wg, wu, wd, gfin, tm):
    N, D = h.shape
    row = lambda w: pl.BlockSpec((tm, w), lambda i, e: (i, 0))
    return pl.pallas_call(
        _moe_kernel,
        grid=(N // tm, N_EXPERTS),
        in_specs=[row(D), row(LANES), row(D),
                  pl.BlockSpec((1, D, D_EXPERT), lambda i, e: (e, 0, 0)),
                  pl.BlockSpec((1, D, D_EXPERT), lambda i, e: (e, 0, 0)),
                  pl.BlockSpec((1, D_EXPERT, D), lambda i, e: (e, 0, 0)),
                  pl.BlockSpec((1, D), lambda i, e: (0, 0))],
        out_specs=row(D),
        out_shape=jax.ShapeDtypeStruct((N, D), F32),
        scratch_shapes=[pltpu.VMEM((tm, D), F32)],
        compiler_params=pltpu.CompilerParams(
            dimension_semantics=("arbitrary", "arbitrary"), vmem_limit_bytes=VMEM_LIMIT),
        name="moe_final",
    )(hn, comb, h, wg, wu, wd, gfin)


def kernel(x, norm_mix_g, w_in, rpb, g_out_dil, g_out_na, w_out, norm_ffn_g, w_group, b_group,
           w_router, b_router, w_gate, w_up, w_down, norm_final_g):
    B, S, D = x.shape
    N = B * S
    depth = w_in.shape[0]
    assert depth == 1 and D == D_MODEL and S % (16 * _DQ) == 0

    t12, t3 = _dilated_tables()
    na_mask, na_dr, na_dc = _na_index_tables(S // GRID_W)
    nq, nk = _NQ_ROWS * GRID_W, _NK_ROWS * GRID_W

    layer = 0
    col_scale = np.ones((6, WIDTH), np.float32)
    col_scale[0] = col_scale[3] = HEAD_DIM ** -0.5
    w_in_b = (w_in[layer] * col_scale.reshape(1, -1)).astype(BF16)

    qkv = _inproj(x, norm_mix_g[layer].reshape(1, D), w_in_b, tm=512)

    y_dil = _dilated_attention(qkv, jnp.asarray(t12), jnp.asarray(t3))

    na_tab = jnp.where(na_mask[None], rpb[layer][:, na_dr, na_dc].astype(F32), NEG_INF)
    na_tab = na_tab.reshape(N_PAIRS, 2, 3, nq, nk).transpose(0, 2, 1, 3, 4).reshape(N_PAIRS, 3, 2 * nq, nk)
    y_na = _na_attention(qkv, na_tab)

    w_r = jnp.zeros((D, LANES), F32)
    w_r = w_r.at[:, :N_EXPERTS].set(w_router[layer]).at[:, _GROUP_LANE0:_GROUP_LANE0 + N_GROUPS].set(w_group[layer])
    w_r_hi = w_r.astype(BF16)
    w_r_lo = (w_r - w_r_hi.astype(F32)).astype(BF16)
    b_r = jnp.zeros((1, LANES), F32)
    b_r = b_r.at[0, :N_EXPERTS].set(b_router[layer]).at[0, _GROUP_LANE0:_GROUP_LANE0 + N_GROUPS].set(b_group[layer])

    h, hn, comb = _outproj(
        x.reshape(N, D), y_dil.reshape(N, WIDTH), y_na.reshape(N, WIDTH),
        g_out_dil[layer].reshape(1, WIDTH), g_out_na[layer].reshape(1, WIDTH),
        w_out[layer].astype(BF16), norm_ffn_g[layer].reshape(1, D), w_r_hi, w_r_lo, b_r, tm=512)

    y = _moe(hn, comb, h, w_gate[layer].astype(BF16), w_up[layer].astype(BF16),
             w_down[layer].astype(BF16), norm_final_g.reshape(1, D), tm=1024)
    return y.reshape(B, S, D)
```

```python
import functools

import numpy as np
import jax
import jax.numpy as jnp
from jax import lax
from jax.experimental import pallas as pl
from jax.experimental.pallas import tpu as pltpu

D_MODEL = 1024
HEAD_DIM = 64
N_HEADS = 8
N_PAIRS = N_HEADS // 2
WIDTH = N_HEADS * HEAD_DIM
N_SLABS = 6 * N_PAIRS
DIL_PATTERNS = ((128, 1), (512, 4), (2048, 16))
DIL_RADIUS = 64
GRID_W = 64
NA_WIN_ROWS = 8
NA_WIN_COLS = 16
N_GROUPS = 4
EXPERTS_PER_GROUP = 4
N_EXPERTS = 16
D_EXPERT = 256
RMS_EPS = 1e-6
NEG_INF = -1e30

LANES = 128
VMEM_LIMIT = 48 * 1024 * 1024

F32 = jnp.float32
BF16 = jnp.bfloat16


def _rms(x, gain):
    return x * lax.rsqrt(jnp.mean(x * x, axis=-1, keepdims=True) + RMS_EPS) * gain


def _inproj_kernel(x_ref, g_ref, w_ref, o_ref):
    xn = _rms(x_ref[0], g_ref[...]).astype(BF16)
    chunk = 4 * LANES
    for c in range(N_SLABS * LANES // chunk):
        acc = jnp.dot(xn, w_ref[:, c * chunk:(c + 1) * chunk], preferred_element_type=F32)
        for j in range(chunk // LANES):
            o_ref[0, c * (chunk // LANES) + j] = acc[:, j * LANES:(j + 1) * LANES].astype(BF16)


def _inproj(x, gain, w_bf16, tm):
    B, S, D = x.shape
    return pl.pallas_call(
        _inproj_kernel,
        grid=(B, S // tm),
        in_specs=[
            pl.BlockSpec((1, tm, D), lambda b, i: (b, i, 0)),
            pl.BlockSpec((1, D), lambda b, i: (0, 0)),
            pl.BlockSpec((D, N_SLABS * LANES), lambda b, i: (0, 0)),
        ],
        out_specs=pl.BlockSpec((1, N_SLABS, tm, LANES), lambda b, i: (b, 0, i, 0)),
        out_shape=jax.ShapeDtypeStruct((B, N_SLABS, S, LANES), BF16),
        compiler_params=pltpu.CompilerParams(
            dimension_semantics=("arbitrary", "arbitrary"), vmem_limit_bytes=VMEM_LIMIT),
        name="inproj",
    )(x, gain, w_bf16)


def _attend(qb, kw, vw, bias):
    lane = lax.broadcasted_iota(jnp.int32, qb.shape, 1)
    zero = jnp.zeros_like(qb)
    qq = jnp.concatenate([jnp.where(lane < HEAD_DIM, qb, zero),
                          jnp.where(lane >= HEAD_DIM, qb, zero)], axis=0)
    s = lax.dot_general(qq, kw, (((1,), (1,)), ((), ())), preferred_element_type=F32) + bias
    m = jnp.max(s, axis=-1, keepdims=True)
    p = jnp.exp(s - m)
    l = jnp.sum(p, axis=-1, keepdims=True)
    pv = jnp.dot(p.astype(BF16), vw, preferred_element_type=F32)
    return m, l, pv


def _merge_heads(top, bottom, q):
    lane = lax.broadcasted_iota(jnp.int32, (q, LANES), 1)
    return jnp.where(lane < HEAD_DIM, jnp.broadcast_to(top, (q, LANES)),
                     jnp.broadcast_to(bottom, (q, LANES)))


_DQ = 128
_DW = 256
_UNROLL = 4


def _dilated_tables():
    slopes = 2.0 ** (-(np.arange(N_HEADS) + 1.0))
    q = np.arange(_DQ)[:, None]
    t12 = np.zeros((N_PAIRS, 2, 3, 2 * _DQ, _DW), np.float32)
    k = np.arange(_DW)[None, :]
    for pat, dil in enumerate((1, 4)):
        for case, off in enumerate((0, _DW // 4, _DW // 2)):
            delta = np.abs(k - (q + off))
            for h in range(N_HEADS):
                tab = np.where(delta <= DIL_RADIUS, -slopes[h] * dil * delta, NEG_INF)
                t12[h // 2, pat, case, (h % 2) * _DQ:(h % 2 + 1) * _DQ] = tab
    t3 = np.zeros((N_PAIRS, 2 * _DQ, _DQ), np.float32)
    delta = np.abs(np.arange(_DQ)[None, :] - q)
    for h in range(N_HEADS):
        t3[h // 2, (h % 2) * _DQ:(h % 2 + 1) * _DQ] = np.where(
            delta <= DIL_RADIUS, -slopes[h] * 16 * delta, NEG_INF)
    return t12, t3


def _dilated_kernel(q_ref, k_ref, v_ref, t12_ref, t3_ref, o_ref,
                    tmp, q4, k4, v4, q16, k16, v16,
                    m1, l1, a1, m2, l2, a2, m3, l3, a3, onat):
    S = tmp.shape[0]
    L4, L16 = S // 4, S // 16

    for src, d4, d16 in ((q_ref, q4, q16), (k_ref, k4, k16), (v_ref, v4, v16)):
        tmp[...] = src[0, 0].astype(F32)
        for r in range(4):
            d4[r * L4:(r + 1) * L4, :] = tmp[pl.ds(r, L4, stride=4), :].astype(BF16)
        for r in range(16):
            d16[r * L16:(r + 1) * L16, :] = tmp[pl.ds(r, L16, stride=16), :].astype(BF16)

    def block(qb, kw, vw, bias, m_ref, l_ref, a_ref, row):
        m, l, pv = _attend(qb, kw, vw, bias)
        m_ref[pl.ds(row, _DQ), :] = _merge_heads(m[:_DQ], m[_DQ:], _DQ)
        l_ref[pl.ds(row, _DQ), :] = _merge_heads(l[:_DQ], l[_DQ:], _DQ)
        a_ref[pl.ds(row, _DQ), :] = _merge_heads(pv[:_DQ], pv[_DQ:], _DQ)

    def case_of(blk, n_blk):
        return jnp.where(blk == 0, 0, jnp.where(blk == n_blk - 1, 2, 1))

    n1 = S // _DQ

    def p1_body(blk, carry):
        t0 = pl.multiple_of(blk * _DQ, _DQ)
        ws = pl.multiple_of(jnp.clip(t0 - DIL_RADIUS, 0, S - _DW), DIL_RADIUS)
        block(q_ref[0, 0, pl.ds(t0, _DQ), :], k_ref[0, 0, pl.ds(ws, _DW), :],
              v_ref[0, 0, pl.ds(ws, _DW), :], t12_ref[0, 0, case_of(blk, n1)], m1, l1, a1, t0)
        return carry

    lax.fori_loop(0, n1, p1_body, 0, unroll=_UNROLL)

    n2 = L4 // _DQ

    def p2_body(j, carry):
        r = j // n2
        blk = j % n2
        l0 = blk * _DQ
        ws = jnp.clip(l0 - DIL_RADIUS, 0, L4 - _DW)
        row = pl.multiple_of(r * L4 + l0, _DQ)
        krow = pl.multiple_of(r * L4 + ws, DIL_RADIUS)
        block(q4[pl.ds(row, _DQ), :], k4[pl.ds(krow, _DW), :], v4[pl.ds(krow, _DW), :],
              t12_ref[0, 1, case_of(blk, n2)], m2, l2, a2, row)
        return carry

    lax.fori_loop(0, 4 * n2, p2_body, 0, unroll=_UNROLL)

    def p3_body(r, carry):
        row = pl.multiple_of(r * L16, L16)
        block(q16[pl.ds(row, L16), :], k16[pl.ds(row, L16), :], v16[pl.ds(row, L16), :],
              t3_ref[0], m3, l3, a3, row)
        return carry

    lax.fori_loop(0, 16, p3_body, 0, unroll=_UNROLL)

    for r16 in range(16):
        r4, c4 = r16 % 4, r16 // 4
        nat = pl.ds(r16, L16, stride=16)
        via4 = pl.ds(r4 * L4 + c4, L16, stride=4)
        via16 = pl.ds(r16 * L16, L16)
        ma, mb, mc = m1[nat, :], m2[via4, :], m3[via16, :]
        mx = jnp.maximum(jnp.maximum(ma, mb), mc)
        wa, wb, wc = jnp.exp(ma - mx), jnp.exp(mb - mx), jnp.exp(mc - mx)
        den = wa * l1[nat, :] + wb * l2[via4, :] + wc * l3[via16, :]
        num = wa * a1[nat, :] + wb * a2[via4, :] + wc * a3[via16, :]
        onat[nat, :] = num / den
    o_ref[0] = onat[...].astype(BF16)


def _dilated_attention(qkv, t12, t3):
    B, _, S, _ = qkv.shape
    f32_buf = pltpu.VMEM((S, LANES), F32)
    bf16_buf = pltpu.VMEM((S, LANES), BF16)
    slab = lambda off: pl.BlockSpec((1, 1, S, LANES), lambda p, b: (b, off + p, 0, 0))
    return pl.pallas_call(
        _dilated_kernel,
        grid=(N_PAIRS, B),
        in_specs=[
            slab(0), slab(N_PAIRS), slab(2 * N_PAIRS),
            pl.BlockSpec((1, 2, 3, 2 * _DQ, _DW), lambda p, b: (p, 0, 0, 0, 0)),
            pl.BlockSpec((1, 2 * _DQ, _DQ), lambda p, b: (p, 0, 0)),
        ],
        out_specs=pl.BlockSpec((1, S, LANES), lambda p, b: (b, 0, p)),
        out_shape=jax.ShapeDtypeStruct((B, S, WIDTH), BF16),
        scratch_shapes=[f32_buf] + [bf16_buf] * 6 + [f32_buf] * 10,
        compiler_params=pltpu.CompilerParams(
            dimension_semantics=("arbitrary", "arbitrary"), vmem_limit_bytes=VMEM_LIMIT),
        name="dilated_attn",
    )(qkv, qkv, qkv, t12, t3)


_NQ_ROWS = 4
_NK_ROWS = 12


def _na_row_select(rows):
    n_blk = rows // _NQ_ROWS
    sel = np.full((n_blk, _NQ_ROWS, _NK_ROWS), -1, np.int64)
    for i in range(n_blk):
        kr0 = min(max(_NQ_ROWS * i - NA_WIN_ROWS // 2, 0), rows - _NK_ROWS)
        for a in range(_NQ_ROWS):
            qr = _NQ_ROWS * i + a
            rs = min(max(qr - NA_WIN_ROWS // 2, 0), rows - NA_WIN_ROWS)
            for b in range(_NK_ROWS):
                kr = kr0 + b
                if rs <= kr < rs + NA_WIN_ROWS:
                    sel[i, a, b] = kr - qr + NA_WIN_ROWS - 1
    for i in range(2, n_blk - 1):
        assert np.array_equal(sel[1], sel[i])
    return sel[[0, 1, n_blk - 1]]


def _na_bias_table(rpb, rows):
    n_dr, n_dc = 2 * NA_WIN_ROWS - 1, 2 * NA_WIN_COLS - 1
    qc = np.arange(GRID_W)[:, None]
    kc = np.arange(GRID_W)[None, :]
    cs = np.clip(qc - NA_WIN_COLS // 2, 0, GRID_W - NA_WIN_COLS)
    col_ok = (kc >= cs) & (kc < cs + NA_WIN_COLS)
    dc = np.clip(kc - qc + NA_WIN_COLS - 1, 0, n_dc - 1)
    onehot = (dc.reshape(1, -1) == np.arange(n_dc)[:, None]).astype(np.float32)
    t = jnp.dot(rpb.astype(F32).reshape(N_HEADS * n_dr, n_dc), onehot, precision=lax.Precision.HIGHEST)
    t = jnp.where(col_ok[None, None], t.reshape(N_HEADS, n_dr, GRID_W, GRID_W), NEG_INF)
    neg = jnp.full((N_HEADS, GRID_W, GRID_W), NEG_INF, F32)
    sel = _na_row_select(rows)
    cases = []
    for c in range(sel.shape[0]):
        per_a = [jnp.stack([neg if s < 0 else t[:, s] for s in sel[c, a]], axis=2)
                 for a in range(_NQ_ROWS)]
        cases.append(jnp.stack(per_a, axis=1))
    return jnp.stack(cases, axis=0).reshape(sel.shape[0], N_HEADS, _NQ_ROWS * GRID_W, _NK_ROWS * GRID_W)


def _na_kernel(q_ref, k_ref, v_ref, tab_ref, o_ref):
    S = q_ref.shape[2]
    rows = S // GRID_W
    n_blk = rows // _NQ_ROWS
    nq = _NQ_ROWS * GRID_W
    nk = _NK_ROWS * GRID_W

    def body(i, carry):
        q0 = pl.multiple_of(i * nq, nq)
        kr0 = jnp.clip(_NQ_ROWS * i - NA_WIN_ROWS // 2, 0, rows - _NK_ROWS)
        k0 = pl.multiple_of(kr0 * GRID_W, GRID_W)
        case = jnp.where(i == 0, 0, jnp.where(i == n_blk - 1, 2, 1))
        m, l, pv = _attend(q_ref[0, 0, pl.ds(q0, nq), :], k_ref[0, 0, pl.ds(k0, nk), :],
                           v_ref[0, 0, pl.ds(k0, nk), :], tab_ref[case].reshape(2 * nq, nk))
        o = pv / l
        lane = lax.broadcasted_iota(jnp.int32, (nq, LANES), 1)
        o_ref[0, pl.ds(q0, nq), :] = jnp.where(lane < HEAD_DIM, o[:nq], o[nq:]).astype(BF16)
        return carry

    lax.fori_loop(0, n_blk, body, 0)


def _na_attention(qkv, tab):
    B, _, S, _ = qkv.shape
    slab = lambda off: pl.BlockSpec((1, 1, S, LANES), lambda p, b: (b, off + p, 0, 0))
    nq, nk = _NQ_ROWS * GRID_W, _NK_ROWS * GRID_W
    return pl.pallas_call(
        _na_kernel,
        grid=(N_PAIRS, B),
        in_specs=[
            slab(3 * N_PAIRS), slab(4 * N_PAIRS), slab(5 * N_PAIRS),
            pl.BlockSpec((3, 2, nq, nk), lambda p, b: (0, p, 0, 0)),
        ],
        out_specs=pl.BlockSpec((1, S, LANES), lambda p, b: (b, 0, p)),
        out_shape=jax.ShapeDtypeStruct((B, S, WIDTH), BF16),
        compiler_params=pltpu.CompilerParams(
            dimension_semantics=("arbitrary", "arbitrary"), vmem_limit_bytes=VMEM_LIMIT),
        name="na_attn",
    )(qkv, qkv, qkv, tab)


_GROUP_LANE0 = N_EXPERTS


def _route(logits):
    lane = lax.broadcasted_iota(jnp.int32, logits.shape, 1)
    big = jnp.int32(LANES)
    is_group = (lane >= _GROUP_LANE0) & (lane < _GROUP_LANE0 + N_GROUPS)
    gl = jnp.where(is_group, logits, NEG_INF)
    gmax = jnp.max(gl, axis=-1, keepdims=True)
    g_idx = jnp.min(jnp.where(is_group & (gl == gmax), lane, big), axis=-1, keepdims=True) - _GROUP_LANE0
    g_weight = 1.0 / jnp.sum(jnp.where(is_group, jnp.exp(gl - gmax), 0.0), axis=-1, keepdims=True)
    in_group = (lane < N_EXPERTS) & ((lane // EXPERTS_PER_GROUP) == g_idx)
    el = jnp.where(in_group, logits, NEG_INF)
    v1 = jnp.max(el, axis=-1, keepdims=True)
    i1 = jnp.min(jnp.where(in_group & (el == v1), lane, big), axis=-1, keepdims=True)
    rest = in_group & (lane != i1)
    el2 = jnp.where(rest, logits, NEG_INF)
    v2 = jnp.max(el2, axis=-1, keepdims=True)
    i2 = jnp.min(jnp.where(rest & (el2 == v2), lane, big), axis=-1, keepdims=True)
    e2 = jnp.exp(v2 - v1)
    w1 = g_weight / (1.0 + e2)
    w2 = g_weight * e2 / (1.0 + e2)
    return jnp.where(lane == i1, w1, jnp.where(lane == i2, w2, 0.0))


def _outproj_kernel(x_ref, yd_ref, yn_ref, gd_ref, gn_ref, wo_ref, gf_ref, wr_hi_ref, wr_lo_ref,
                    br_ref, h_ref, hn_ref, comb_ref):
    yd = _rms(yd_ref[...].astype(F32), gd_ref[...]).astype(BF16)
    yn = _rms(yn_ref[...].astype(F32), gn_ref[...]).astype(BF16)
    h = (x_ref[...] + jnp.dot(yd, wo_ref[:WIDTH, :], preferred_element_type=F32)
         + jnp.dot(yn, wo_ref[WIDTH:, :], preferred_element_type=F32))
    h_ref[...] = h
    hn = _rms(h, gf_ref[...])
    hi = hn.astype(BF16)
    lo = (hn - hi.astype(F32)).astype(BF16)
    hn_ref[...] = hi
    logits = (jnp.dot(hi, wr_hi_ref[...], preferred_element_type=F32)
              + (jnp.dot(lo, wr_hi_ref[...], preferred_element_type=F32)
                 + jnp.dot(hi, wr_lo_ref[...], preferred_element_type=F32))
              + br_ref[...])
    comb_ref[...] = _route(logits)


def _outproj(x2, yd2, yn2, gd, gn, wo, gf, wr_hi, wr_lo, br, tm):
    N, D = x2.shape
    row = lambda w: pl.BlockSpec((tm, w), lambda i: (i, 0))
    full = lambda a, b: pl.BlockSpec((a, b), lambda i: (0, 0))
    return pl.pallas_call(
        _outproj_kernel,
        grid=(N // tm,),
        in_specs=[row(D), row(WIDTH), row(WIDTH), full(1, WIDTH), full(1, WIDTH), full(2 * WIDTH, D),
                  full(1, D), full(D, LANES), full(D, LANES), full(1, LANES)],
        out_specs=[row(D), row(D), row(LANES)],
        out_shape=[jax.ShapeDtypeStruct((N, D), F32), jax.ShapeDtypeStruct((N, D), BF16),
                   jax.ShapeDtypeStruct((N, LANES), F32)],
        compiler_params=pltpu.CompilerParams(
            dimension_semantics=("arbitrary",), vmem_limit_bytes=VMEM_LIMIT),
        name="outproj_route",
    )(x2, yd2, yn2, gd, gn, wo, gf, wr_hi, wr_lo, br)


def _moe_kernel(hn_ref, comb_ref, h_ref, wg_ref, wu_ref, wd_ref, gfin_ref, y_ref, acc_ref):
    e = pl.program_id(1)

    @pl.when(e == 0)
    def _():
        acc_ref[...] = jnp.zeros_like(acc_ref)

    hn = hn_ref[...]
    gate = jnp.dot(hn, wg_ref[0], preferred_element_type=F32)
    up = jnp.dot(hn, wu_ref[0], preferred_element_type=F32)
    comb = comb_ref[...]
    lane = lax.broadcasted_iota(jnp.int32, comb.shape, 1)
    c = jnp.sum(jnp.where(lane == e, comb, 0.0), axis=-1, keepdims=True)
    act = (gate / (1.0 + jnp.exp(-gate))) * up * c
    acc_ref[...] += jnp.dot(act.astype(BF16), wd_ref[0], preferred_element_type=F32)

    @pl.when(e == pl.num_programs(1) - 1)
    def _():
        y_ref[...] = _rms(h_ref[...] + acc_ref[...], gfin_ref[...])


def _moe(hn, comb, h, wg, wu, wd, gfin, tm):
    N, D = h.shape
    row = lambda w: pl.BlockSpec((tm, w), lambda i, e: (i, 0))
    return pl.pallas_call(
        _moe_kernel,
        grid=(N // tm, N_EXPERTS),
        in_specs=[row(D), row(LANES), row(D),
                  pl.BlockSpec((1, D, D_EXPERT), lambda i, e: (e, 0, 0)),
                  pl.BlockSpec((1, D, D_EXPERT), lambda i, e: (e, 0, 0)),
                  pl.BlockSpec((1, D_EXPERT, D), lambda i, e: (e, 0, 0)),
                  pl.BlockSpec((1, D), lambda i, e: (0, 0))],
        out_specs=row(D),
        out_shape=jax.ShapeDtypeStruct((N, D), F32),
        scratch_shapes=[pltpu.VMEM((tm, D), F32)],
        compiler_params=pltpu.CompilerParams(
            dimension_semantics=("arbitrary", "arbitrary"), vmem_limit_bytes=VMEM_LIMIT),
        name="moe_final",
    )(hn, comb, h, wg, wu, wd, gfin)


def kernel(x, norm_mix_g, w_in, rpb, g_out_dil, g_out_na, w_out, norm_ffn_g, w_group, b_group,
           w_router, b_router, w_gate, w_up, w_down, norm_final_g):
    B, S, D = x.shape
    N = B * S
    depth = w_in.shape[0]
    assert depth == 1 and D == D_MODEL and S % (16 * _DQ) == 0

    t12, t3 = _dilated_tables()

    layer = 0
    col_scale = np.ones((6, WIDTH), np.float32)
    col_scale[0] = col_scale[3] = HEAD_DIM ** -0.5
    w_in_b = (w_in[layer] * col_scale.reshape(1, -1)).astype(BF16)

    qkv = _inproj(x, norm_mix_g[layer].reshape(1, D), w_in_b, tm=512)

    y_dil = _dilated_attention(qkv, jnp.asarray(t12), jnp.asarray(t3))

    y_na = _na_attention(qkv, _na_bias_table(rpb[layer], S // GRID_W))

    w_r = jnp.zeros((D, LANES), F32)
    w_r = w_r.at[:, :N_EXPERTS].set(w_router[layer]).at[:, _GROUP_LANE0:_GROUP_LANE0 + N_GROUPS].set(w_group[layer])
    w_r_hi = w_r.astype(BF16)
    w_r_lo = (w_r - w_r_hi.astype(F32)).astype(BF16)
    b_r = jnp.zeros((1, LANES), F32)
    b_r = b_r.at[0, :N_EXPERTS].set(b_router[layer]).at[0, _GROUP_LANE0:_GROUP_LANE0 + N_GROUPS].set(b_group[layer])

    h, hn, comb = _outproj(
        x.reshape(N, D), y_dil.reshape(N, WIDTH), y_na.reshape(N, WIDTH),
        g_out_dil[layer].reshape(1, WIDTH), g_out_na[layer].reshape(1, WIDTH),
        w_out[layer].astype(BF16), norm_ffn_g[layer].reshape(1, D), w_r_hi, w_r_lo, b_r, tm=512)

    y = _moe(hn, comb, h, w_gate[layer].astype(BF16), w_up[layer].astype(BF16),
             w_down[layer].astype(BF16), norm_final_g.reshape(1, D), tm=1024)
    return y.reshape(B, S, D)
```

```python
import functools

import numpy as np
import jax
import jax.numpy as jnp
from jax import lax
from jax.experimental import pallas as pl
from jax.experimental.pallas import tpu as pltpu

D_MODEL = 1024
HEAD_DIM = 64
N_HEADS = 8
N_PAIRS = N_HEADS // 2
WIDTH = N_HEADS * HEAD_DIM
N_SLABS = 6 * N_PAIRS
DIL_PATTERNS = ((128, 1), (512, 4), (2048, 16))
DIL_RADIUS = 64
GRID_W = 64
NA_WIN_ROWS = 8
NA_WIN_COLS = 16
N_GROUPS = 4
EXPERTS_PER_GROUP = 4
N_EXPERTS = 16
D_EXPERT = 256
RMS_EPS = 1e-6
NEG_INF = -1e30

LANES = 128
VMEM_LIMIT = 48 * 1024 * 1024

F32 = jnp.float32
BF16 = jnp.bfloat16


def _rms(x, gain):
    return x * lax.rsqrt(jnp.mean(x * x, axis=-1, keepdims=True) + RMS_EPS) * gain


def _inproj_kernel(x_ref, g_ref, w_ref, o_ref):
    xn = _rms(x_ref[0], g_ref[...]).astype(BF16)
    chunk = 4 * LANES
    for c in range(N_SLABS * LANES // chunk):
        acc = jnp.dot(xn, w_ref[:, c * chunk:(c + 1) * chunk], preferred_element_type=F32)
        for j in range(chunk // LANES):
            o_ref[0, c * (chunk // LANES) + j] = acc[:, j * LANES:(j + 1) * LANES].astype(BF16)


def _inproj(x, gain, w_bf16, tm):
    B, S, D = x.shape
    return pl.pallas_call(
        _inproj_kernel,
        grid=(B, S // tm),
        in_specs=[
            pl.BlockSpec((1, tm, D), lambda b, i: (b, i, 0)),
            pl.BlockSpec((1, D), lambda b, i: (0, 0)),
            pl.BlockSpec((D, N_SLABS * LANES), lambda b, i: (0, 0)),
        ],
        out_specs=pl.BlockSpec((1, N_SLABS, tm, LANES), lambda b, i: (b, 0, i, 0)),
        out_shape=jax.ShapeDtypeStruct((B, N_SLABS, S, LANES), BF16),
        compiler_params=pltpu.CompilerParams(
            dimension_semantics=("arbitrary", "arbitrary"), vmem_limit_bytes=VMEM_LIMIT),
        name="inproj",
    )(x, gain, w_bf16)


def _attend(qb, kw, vw, bias):
    lane = lax.broadcasted_iota(jnp.int32, qb.shape, 1)
    zero = jnp.zeros_like(qb)
    qq = jnp.concatenate([jnp.where(lane < HEAD_DIM, qb, zero),
                          jnp.where(lane >= HEAD_DIM, qb, zero)], axis=0)
    s = lax.dot_general(qq, kw, (((1,), (1,)), ((), ())), preferred_element_type=F32) + bias
    m = jnp.max(s, axis=-1, keepdims=True)
    p = jnp.exp(s - m)
    l = jnp.sum(p, axis=-1, keepdims=True)
    pv = jnp.dot(p.astype(BF16), vw, preferred_element_type=F32)
    return m, l, pv


def _merge_heads(top, bottom, q):
    lane = lax.broadcasted_iota(jnp.int32, (q, LANES), 1)
    return jnp.where(lane < HEAD_DIM, jnp.broadcast_to(top, (q, LANES)),
                     jnp.broadcast_to(bottom, (q, LANES)))


_DQ = 128
_DW = 256
_UNROLL = 4


def _dilated_tables():
    slopes = 2.0 ** (-(np.arange(N_HEADS) + 1.0))
    q = np.arange(_DQ)[:, None]
    t12 = np.zeros((N_PAIRS, 2, 3, 2 * _DQ, _DW), np.float32)
    k = np.arange(_DW)[None, :]
    for pat, dil in enumerate((1, 4)):
        for case, off in enumerate((0, _DW // 4, _DW // 2)):
            delta = np.abs(k - (q + off))
            for h in range(N_HEADS):
                tab = np.where(delta <= DIL_RADIUS, -slopes[h] * dil * delta, NEG_INF)
                t12[h // 2, pat, case, (h % 2) * _DQ:(h % 2 + 1) * _DQ] = tab
    t3 = np.zeros((N_PAIRS, 2 * _DQ, _DQ), np.float32)
    delta = np.abs(np.arange(_DQ)[None, :] - q)
    for h in range(N_HEADS):
        t3[h // 2, (h % 2) * _DQ:(h % 2 + 1) * _DQ] = np.where(
            delta <= DIL_RADIUS, -slopes[h] * 16 * delta, NEG_INF)
    return t12, t3


def _dilated_kernel(q_ref, k_ref, v_ref, t12_ref, t3_ref, o_ref,
                    tmp, q4, k4, v4, q16, k16, v16,
                    m1, l1, a1, m2, l2, a2, m3, l3, a3, onat):
    S = tmp.shape[0]
    L4, L16 = S // 4, S // 16

    for src, d4, d16 in ((q_ref, q4, q16), (k_ref, k4, k16), (v_ref, v4, v16)):
        tmp[...] = src[0, 0].astype(F32)
        for r in range(4):
            d4[r * L4:(r + 1) * L4, :] = tmp[pl.ds(r, L4, stride=4), :].astype(BF16)
        for r in range(16):
            d16[r * L16:(r + 1) * L16, :] = tmp[pl.ds(r, L16, stride=16), :].astype(BF16)

    def block(qb, kw, vw, bias, m_ref, l_ref, a_ref, row):
        m, l, pv = _attend(qb, kw, vw, bias)
        m_ref[pl.ds(row, _DQ), :] = _merge_heads(m[:_DQ], m[_DQ:], _DQ)
        l_ref[pl.ds(row, _DQ), :] = _merge_heads(l[:_DQ], l[_DQ:], _DQ)
        a_ref[pl.ds(row, _DQ), :] = _merge_heads(pv[:_DQ], pv[_DQ:], _DQ)

    def case_of(blk, n_blk):
        return jnp.where(blk == 0, 0, jnp.where(blk == n_blk - 1, 2, 1))

    n1 = S // _DQ

    def p1_body(blk, carry):
        t0 = pl.multiple_of(blk * _DQ, _DQ)
        ws = pl.multiple_of(jnp.clip(t0 - DIL_RADIUS, 0, S - _DW), DIL_RADIUS)
        block(q_ref[0, 0, pl.ds(t0, _DQ), :], k_ref[0, 0, pl.ds(ws, _DW), :],
              v_ref[0, 0, pl.ds(ws, _DW), :], t12_ref[0, 0, case_of(blk, n1)], m1, l1, a1, t0)
        return carry

    lax.fori_loop(0, n1, p1_body, 0, unroll=_UNROLL)

    n2 = L4 // _DQ

    def p2_body(j, carry):
        r = j // n2
        blk = j % n2
        l0 = blk * _DQ
        ws = jnp.clip(l0 - DIL_RADIUS, 0, L4 - _DW)
        row = pl.multiple_of(r * L4 + l0, _DQ)
        krow = pl.multiple_of(r * L4 + ws, DIL_RADIUS)
        block(q4[pl.ds(row, _DQ), :], k4[pl.ds(krow, _DW), :], v4[pl.ds(krow, _DW), :],
              t12_ref[0, 1, case_of(blk, n2)], m2, l2, a2, row)
        return carry

    lax.fori_loop(0, 4 * n2, p2_body, 0, unroll=_UNROLL)

    def p3_body(r, carry):
        row = pl.multiple_of(r * L16, L16)
        block(q16[pl.ds(row, L16), :], k16[pl.ds(row, L16), :], v16[pl.ds(row, L16), :],
              t3_ref[0], m3, l3, a3, row)
        return carry

    lax.fori_loop(0, 16, p3_body, 0, unroll=_UNROLL)

    for r16 in range(16):
        r4, c4 = r16 % 4, r16 // 4
        nat = pl.ds(r16, L16, stride=16)
        via4 = pl.ds(r4 * L4 + c4, L16, stride=4)
        via16 = pl.ds(r16 * L16, L16)
        ma, mb, mc = m1[nat, :], m2[via4, :], m3[via16, :]
        mx = jnp.maximum(jnp.maximum(ma, mb), mc)
        wa, wb, wc = jnp.exp(ma - mx), jnp.exp(mb - mx), jnp.exp(mc - mx)
        den = wa * l1[nat, :] + wb * l2[via4, :] + wc * l3[via16, :]
        num = wa * a1[nat, :] + wb * a2[via4, :] + wc * a3[via16, :]
        onat[nat, :] = num / den
    o_ref[0] = onat[...].astype(BF16)


def _dilated_attention(qkv, t12, t3):
    B, _, S, _ = qkv.shape
    f32_buf = pltpu.VMEM((S, LANES), F32)
    bf16_buf = pltpu.VMEM((S, LANES), BF16)
    slab = lambda off: pl.BlockSpec((1, 1, S, LANES), lambda p, b: (b, off + p, 0, 0))
    return pl.pallas_call(
        _dilated_kernel,
        grid=(N_PAIRS, B),
        in_specs=[
            slab(0), slab(N_PAIRS), slab(2 * N_PAIRS),
            pl.BlockSpec((1, 2, 3, 2 * _DQ, _DW), lambda p, b: (p, 0, 0, 0, 0)),
            pl.BlockSpec((1, 2 * _DQ, _DQ), lambda p, b: (p, 0, 0)),
        ],
        out_specs=pl.BlockSpec((1, S, LANES), lambda p, b: (b, 0, p)),
        out_shape=jax.ShapeDtypeStruct((B, S, WIDTH), BF16),
        scratch_shapes=[f32_buf] + [bf16_buf] * 6 + [f32_buf] * 10,
        compiler_params=pltpu.CompilerParams(
            dimension_semantics=("arbitrary", "arbitrary"), vmem_limit_bytes=VMEM_LIMIT),
        name="dilated_attn",
    )(qkv, qkv, qkv, t12, t3)


_NQ_ROWS = 4
_NK_ROWS = 12


def _na_row_select(rows):
    n_blk = rows // _NQ_ROWS
    sel = np.full((n_blk, _NQ_ROWS, _NK_ROWS), -1, np.int64)
    for i in range(n_blk):
        kr0 = min(max(_NQ_ROWS * i - NA_WIN_ROWS // 2, 0), rows - _NK_ROWS)
        for a in range(_NQ_ROWS):
            qr = _NQ_ROWS * i + a
            rs = min(max(qr - NA_WIN_ROWS // 2, 0), rows - NA_WIN_ROWS)
            for b in range(_NK_ROWS):
                kr = kr0 + b
                if rs <= kr < rs + NA_WIN_ROWS:
                    sel[i, a, b] = kr - qr + NA_WIN_ROWS - 1
    for i in range(2, n_blk - 1):
        assert np.array_equal(sel[1], sel[i])
    return sel[[0, 1, n_blk - 1]]


def _na_bias_table(rpb, rows):
    n_dr, n_dc = 2 * NA_WIN_ROWS - 1, 2 * NA_WIN_COLS - 1
    qc = np.arange(GRID_W)[:, None]
    kc = np.arange(GRID_W)[None, :]
    cs = np.clip(qc - NA_WIN_COLS // 2, 0, GRID_W - NA_WIN_COLS)
    col_ok = (kc >= cs) & (kc < cs + NA_WIN_COLS)
    dc = np.clip(kc - qc + NA_WIN_COLS - 1, 0, n_dc - 1)
    onehot = (dc.reshape(1, -1) == np.arange(n_dc)[:, None]).astype(np.float32)
    t = jnp.dot(rpb.astype(F32).reshape(N_HEADS * n_dr, n_dc), onehot, precision=lax.Precision.HIGHEST)
    t = jnp.where(col_ok[None, None], t.reshape(N_HEADS, n_dr, GRID_W, GRID_W), NEG_INF)
    neg = jnp.full((N_HEADS, GRID_W, GRID_W), NEG_INF, F32)
    sel = _na_row_select(rows)
    cases = []
    for c in range(sel.shape[0]):
        per_a = [jnp.stack([neg if s < 0 else t[:, s] for s in sel[c, a]], axis=2)
                 for a in range(_NQ_ROWS)]
        cases.append(jnp.stack(per_a, axis=1))
    return jnp.stack(cases, axis=0).reshape(sel.shape[0], N_HEADS, _NQ_ROWS * GRID_W, _NK_ROWS * GRID_W)


def _na_kernel(q_ref, k_ref, v_ref, tab_ref, o_ref):
    S = q_ref.shape[2]
    rows = S // GRID_W
    n_blk = rows // _NQ_ROWS
    nq = _NQ_ROWS * GRID_W
    nk = _NK_ROWS * GRID_W

    def body(i, carry):
        q0 = pl.multiple_of(i * nq, nq)
        kr0 = jnp.clip(_NQ_ROWS * i - NA_WIN_ROWS // 2, 0, rows - _NK_ROWS)
        k0 = pl.multiple_of(kr0 * GRID_W, GRID_W)
        case = jnp.where(i == 0, 0, jnp.where(i == n_blk - 1, 2, 1))
        m, l, pv = _attend(q_ref[0, 0, pl.ds(q0, nq), :], k_ref[0, 0, pl.ds(k0, nk), :],
                           v_ref[0, 0, pl.ds(k0, nk), :], tab_ref[case].reshape(2 * nq, nk))
        o = pv / l
        lane = lax.broadcasted_iota(jnp.int32, (nq, LANES), 1)
        o_ref[0, pl.ds(q0, nq), :] = jnp.where(lane < HEAD_DIM, o[:nq], o[nq:]).astype(BF16)
        return carry

    lax.fori_loop(0, n_blk, body, 0)


def _na_attention(qkv, tab):
    B, _, S, _ = qkv.shape
    slab = lambda off: pl.BlockSpec((1, 1, S, LANES), lambda p, b: (b, off + p, 0, 0))
    nq, nk = _NQ_ROWS * GRID_W, _NK_ROWS * GRID_W
    return pl.pallas_call(
        _na_kernel,
        grid=(N_PAIRS, B),
        in_specs=[
            slab(3 * N_PAIRS), slab(4 * N_PAIRS), slab(5 * N_PAIRS),
            pl.BlockSpec((3, 2, nq, nk), lambda p, b: (0, p, 0, 0)),
        ],
        out_specs=pl.BlockSpec((1, S, LANES), lambda p, b: (b, 0, p)),
        out_shape=jax.ShapeDtypeStruct((B, S, WIDTH), BF16),
        compiler_params=pltpu.CompilerParams(
            dimension_semantics=("arbitrary", "arbitrary"), vmem_limit_bytes=VMEM_LIMIT),
        name="na_attn",
    )(qkv, qkv, qkv, tab)


_GROUP_LANE0 = N_EXPERTS
_ROUTE_GROUP_LANE = EXPERTS_PER_GROUP


def _route(logits):
    lane = lax.broadcasted_iota(jnp.int32, logits.shape, 1)
    big = jnp.int32(LANES)
    is_group = (lane >= _GROUP_LANE0) & (lane < _GROUP_LANE0 + N_GROUPS)
    gl = jnp.where(is_group, logits, NEG_INF)
    gmax = jnp.max(gl, axis=-1, keepdims=True)
    g_idx = jnp.min(jnp.where(is_group & (gl == gmax), lane, big), axis=-1, keepdims=True) - _GROUP_LANE0
    g_weight = 1.0 / jnp.sum(jnp.where(is_group, jnp.exp(gl - gmax), 0.0), axis=-1, keepdims=True)
    in_group = (lane < N_EXPERTS) & ((lane // EXPERTS_PER_GROUP) == g_idx)
    el = jnp.where(in_group, logits, NEG_INF)
    v1 = jnp.max(el, axis=-1, keepdims=True)
    i1 = jnp.min(jnp.where(in_group & (el == v1), lane, big), axis=-1, keepdims=True)
    rest = in_group & (lane != i1)
    el2 = jnp.where(rest, logits, NEG_INF)
    v2 = jnp.max(el2, axis=-1, keepdims=True)
    i2 = jnp.min(jnp.where(rest & (el2 == v2), lane, big), axis=-1, keepdims=True)
    e2 = jnp.exp(v2 - v1)
    w1 = g_weight / (1.0 + e2)
    w2 = g_weight * e2 / (1.0 + e2)
    base = g_idx * EXPERTS_PER_GROUP
    return jnp.where(lane == i1 - base, w1,
                     jnp.where(lane == i2 - base, w2,
                               jnp.where(lane == _ROUTE_GROUP_LANE, g_idx.astype(F32), 0.0)))


def _outproj_kernel(x_ref, yd_ref, yn_ref, gd_ref, gn_ref, wo_ref, gf_ref, wr_hi_ref, wr_lo_ref,
                    br_ref, h_ref, hn_ref, comb_ref):
    yd = _rms(yd_ref[...].astype(F32), gd_ref[...]).astype(BF16)
    yn = _rms(yn_ref[...].astype(F32), gn_ref[...]).astype(BF16)
    h = (x_ref[...] + jnp.dot(yd, wo_ref[:WIDTH, :], preferred_element_type=F32)
         + jnp.dot(yn, wo_ref[WIDTH:, :], preferred_element_type=F32))
    h_ref[...] = h
    hn = _rms(h, gf_ref[...])
    hi = hn.astype(BF16)
    lo = (hn - hi.astype(F32)).astype(BF16)
    hn_ref[...] = hi
    logits = (jnp.dot(hi, wr_hi_ref[...], preferred_element_type=F32)
              + (jnp.dot(lo, wr_hi_ref[...], preferred_element_type=F32)
                 + jnp.dot(hi, wr_lo_ref[...], preferred_element_type=F32))
              + br_ref[...])
    comb_ref[...] = _route(logits)


def _outproj(x2, yd2, yn2, gd, gn, wo, gf, wr_hi, wr_lo, br, tm):
    N, D = x2.shape
    row = lambda w: pl.BlockSpec((tm, w), lambda i: (i, 0))
    full = lambda a, b: pl.BlockSpec((a, b), lambda i: (0, 0))
    return pl.pallas_call(
        _outproj_kernel,
        grid=(N // tm,),
        in_specs=[row(D), row(WIDTH), row(WIDTH), full(1, WIDTH), full(1, WIDTH), full(2 * WIDTH, D),
                  full(1, D), full(D, LANES), full(D, LANES), full(1, LANES)],
        out_specs=[row(D), row(D), row(LANES)],
        out_shape=[jax.ShapeDtypeStruct((N, D), F32), jax.ShapeDtypeStruct((N, D), BF16),
                   jax.ShapeDtypeStruct((N, LANES), F32)],
        compiler_params=pltpu.CompilerParams(
            dimension_semantics=("arbitrary",), vmem_limit_bytes=VMEM_LIMIT),
        name="outproj_route",
    )(x2, yd2, yn2, gd, gn, wo, gf, wr_hi, wr_lo, br)


_MOE_TILE = 1024
_MOE_CHUNK = 128
_MOE_NCHUNK = _MOE_TILE // _MOE_CHUNK + N_GROUPS
_GROUP_WIDTH = EXPERTS_PER_GROUP * D_EXPERT


def _moe_kernel(hn_ref, route_ref, wgu_ref, wd_ref, o_ref, ys_ref):
    T, C = _MOE_TILE, _MOE_CHUNK
    route = route_ref[...]
    lane = lax.broadcasted_iota(jnp.int32, (T, LANES), 1)
    gid = jnp.sum(jnp.where(lane == _ROUTE_GROUP_LANE, route, 0.0), axis=-1, keepdims=True)
    onehot = jnp.where((lane < N_GROUPS) & (lane.astype(F32) == gid), 1.0, 0.0)

    before = (lax.broadcasted_iota(jnp.int32, (T, T), 1)
              < lax.broadcasted_iota(jnp.int32, (T, T), 0)).astype(BF16)
    rank = jnp.dot(before, onehot.astype(BF16), preferred_element_type=F32)
    count = jnp.sum(onehot, axis=0, keepdims=True)
    nchunk = jnp.floor((count + (C - 1)) * (1.0 / C)).astype(jnp.int32)
    off1 = nchunk[0, 0]
    off2 = off1 + nchunk[0, 1]
    off3 = off2 + nchunk[0, 2]
    n_used = off3 + nchunk[0, 3]
    start = jnp.where(lane == 1, off1, jnp.where(lane == 2, off2, jnp.where(lane == 3, off3, 0)))
    pos = jnp.sum(onehot * (rank + (start * C).astype(F32)), axis=-1, keepdims=True)
    pos_i = pos.astype(jnp.int32)
    pos_row = jnp.transpose(jnp.broadcast_to(pos, (T, LANES)))[0:1, :].astype(jnp.int32)

    r_hi = route.astype(BF16)
    r_mid = (route - r_hi.astype(F32)).astype(BF16)
    r_lo = (route - r_hi.astype(F32) - r_mid.astype(F32)).astype(BF16)
    route3 = jnp.concatenate([r_hi, r_mid, r_lo], axis=-1)

    ys_ref[...] = jnp.zeros_like(ys_ref)
    hn = hn_ref[...]

    def chunk_body(c, carry):
        g = ((c >= off1).astype(jnp.int32) + (c >= off2).astype(jnp.int32)
             + (c >= off3).astype(jnp.int32))
        row0 = pl.multiple_of(c * C, C)
        sel = (pos_row == row0 + lax.broadcasted_iota(jnp.int32, (C, T), 0)).astype(BF16)
        xs = jnp.dot(sel, hn, preferred_element_type=F32).astype(BF16)
        r3 = jnp.dot(sel, route3, preferred_element_type=F32)
        r = r3[:, :LANES] + r3[:, LANES:2 * LANES] + r3[:, 2 * LANES:]
        gu = jnp.dot(xs, wgu_ref[g], preferred_element_type=F32)
        gate, up = gu[:, :_GROUP_WIDTH], gu[:, _GROUP_WIDTH:]
        act = (gate / (1.0 + jnp.exp(-gate))) * up
        clane = lax.broadcasted_iota(jnp.int32, (C, LANES), 1)
        parts = []
        for j in range(EXPERTS_PER_GROUP):
            wj = jnp.sum(jnp.where(clane == j, r, 0.0), axis=-1, keepdims=True)
            parts.append(act[:, j * D_EXPERT:(j + 1) * D_EXPERT] * wj)
        act = jnp.concatenate(parts, axis=-1).astype(BF16)
        ys_ref[pl.ds(row0, C), :] = jnp.dot(act, wd_ref[g], preferred_element_type=F32).astype(BF16)
        return carry

    lax.fori_loop(0, n_used, chunk_body, 0)

    back = (lax.broadcasted_iota(jnp.int32, (T, _MOE_NCHUNK * C), 1) == pos_i).astype(BF16)
    o_ref[...] = jnp.dot(back, ys_ref[...], preferred_element_type=F32).astype(BF16)


def _moe(hn, route, wgu, wd):
    N, D = hn.shape
    T = _MOE_TILE
    row = lambda w: pl.BlockSpec((T, w), lambda i: (i, 0))
    whole = lambda a: pl.BlockSpec(a.shape, lambda i: (0,) * a.ndim)
    return pl.pallas_call(
        _moe_kernel,
        grid=(N // T,),
        in_specs=[row(D), row(LANES), whole(wgu), whole(wd)],
        out_specs=row(D),
        out_shape=jax.ShapeDtypeStruct((N, D), BF16),
        scratch_shapes=[pltpu.VMEM((_MOE_NCHUNK * _MOE_CHUNK, D), BF16)],
        compiler_params=pltpu.CompilerParams(
            dimension_semantics=("arbitrary",), vmem_limit_bytes=VMEM_LIMIT),
        name="moe_grouped",
    )(hn, route, wgu, wd)


def _final_kernel(h_ref, m_ref, g_ref, y_ref):
    y_ref[...] = _rms(h_ref[...] + m_ref[...].astype(F32), g_ref[...])


def _final(h, moe_out, gfin, tm):
    N, D = h.shape
    row = pl.BlockSpec((tm, D), lambda i: (i, 0))
    return pl.pallas_call(
        _final_kernel,
        grid=(N // tm,),
        in_specs=[row, row, pl.BlockSpec((1, D), lambda i: (0, 0))],
        out_specs=row,
        out_shape=jax.ShapeDtypeStruct((N, D), F32),
        compiler_params=pltpu.CompilerParams(
            dimension_semantics=("arbitrary",), vmem_limit_bytes=VMEM_LIMIT),
        name="final_norm",
    )(h, moe_out, gfin)


def kernel(x, norm_mix_g, w_in, rpb, g_out_dil, g_out_na, w_out, norm_ffn_g, w_group, b_group,
           w_router, b_router, w_gate, w_up, w_down, norm_final_g):
    B, S, D = x.shape
    N = B * S
    depth = w_in.shape[0]
    assert depth == 1 and D == D_MODEL and S % (16 * _DQ) == 0

    t12, t3 = _dilated_tables()

    layer = 0
    col_scale = np.ones((6, WIDTH), np.float32)
    col_scale[0] = col_scale[3] = HEAD_DIM ** -0.5
    w_in_b = (w_in[layer] * col_scale.reshape(1, -1)).astype(BF16)

    qkv = _inproj(x, norm_mix_g[layer].reshape(1, D), w_in_b, tm=512)

    y_dil = _dilated_attention(qkv, jnp.asarray(t12), jnp.asarray(t3))

    y_na = _na_attention(qkv, _na_bias_table(rpb[layer], S // GRID_W))

    w_r = jnp.zeros((D, LANES), F32)
    w_r = w_r.at[:, :N_EXPERTS].set(w_router[layer]).at[:, _GROUP_LANE0:_GROUP_LANE0 + N_GROUPS].set(w_group[layer])
    w_r_hi = w_r.astype(BF16)
    w_r_lo = (w_r - w_r_hi.astype(F32)).astype(BF16)
    b_r = jnp.zeros((1, LANES), F32)
    b_r = b_r.at[0, :N_EXPERTS].set(b_router[layer]).at[0, _GROUP_LANE0:_GROUP_LANE0 + N_GROUPS].set(b_group[layer])

    h, hn, route = _outproj(
        x.reshape(N, D), y_dil.reshape(N, WIDTH), y_na.reshape(N, WIDTH),
        g_out_dil[layer].reshape(1, WIDTH), g_out_na[layer].reshape(1, WIDTH),
        w_out[layer].astype(BF16), norm_ffn_g[layer].reshape(1, D), w_r_hi, w_r_lo, b_r, tm=512)

    def group_cols(w):
        w = w.astype(BF16).reshape(N_GROUPS, EXPERTS_PER_GROUP, D, D_EXPERT)
        return w.transpose(0, 2, 1, 3).reshape(N_GROUPS, D, _GROUP_WIDTH)

    wgu = jnp.concatenate([group_cols(w_gate[layer]), group_cols(w_up[layer])], axis=-1)
    wd = w_down[layer].astype(BF16).reshape(N_GROUPS, _GROUP_WIDTH, D)

    moe_out = _moe(hn, route, wgu, wd)
    y = _final(h, moe_out, norm_final_g.reshape(1, D), tm=1024)
    return y.reshape(B, S, D)
```

```python
import functools

import numpy as np
import jax
import jax.numpy as jnp
from jax import lax
from jax.experimental import pallas as pl
from jax.experimental.pallas import tpu as pltpu

D_MODEL = 1024
HEAD_DIM = 64
N_HEADS = 8
N_PAIRS = N_HEADS // 2
WIDTH = N_HEADS * HEAD_DIM
N_SLABS = 6 * N_PAIRS
DIL_PATTERNS = ((128, 1), (512, 4), (2048, 16))
DIL_RADIUS = 64
GRID_W = 64
NA_WIN_ROWS = 8
NA_WIN_COLS = 16
N_GROUPS = 4
EXPERTS_PER_GROUP = 4
N_EXPERTS = 16
D_EXPERT = 256
RMS_EPS = 1e-6
NEG_INF = -1e30

LANES = 128
VMEM_LIMIT = 48 * 1024 * 1024

F32 = jnp.float32
BF16 = jnp.bfloat16


def _rms(x, gain):
    return x * lax.rsqrt(jnp.mean(x * x, axis=-1, keepdims=True) + RMS_EPS) * gain


def _inproj_kernel(x_ref, g_ref, w_ref, o_ref):
    xn = _rms(x_ref[0], g_ref[...]).astype(BF16)
    chunk = 4 * LANES
    for c in range(N_SLABS * LANES // chunk):
        acc = jnp.dot(xn, w_ref[:, c * chunk:(c + 1) * chunk], preferred_element_type=F32)
        for j in range(chunk // LANES):
            o_ref[0, c * (chunk // LANES) + j] = acc[:, j * LANES:(j + 1) * LANES].astype(BF16)


def _inproj(x, gain, w_bf16, tm):
    B, S, D = x.shape
    return pl.pallas_call(
        _inproj_kernel,
        grid=(B, S // tm),
        in_specs=[
            pl.BlockSpec((1, tm, D), lambda b, i: (b, i, 0)),
            pl.BlockSpec((1, D), lambda b, i: (0, 0)),
            pl.BlockSpec((D, N_SLABS * LANES), lambda b, i: (0, 0)),
        ],
        out_specs=pl.BlockSpec((1, N_SLABS, tm, LANES), lambda b, i: (b, 0, i, 0)),
        out_shape=jax.ShapeDtypeStruct((B, N_SLABS, S, LANES), BF16),
        compiler_params=pltpu.CompilerParams(
            dimension_semantics=("arbitrary", "arbitrary"), vmem_limit_bytes=VMEM_LIMIT),
        name="inproj",
    )(x, gain, w_bf16)


def _attend(qb, kw, vw, bias):
    lane = lax.broadcasted_iota(jnp.int32, qb.shape, 1)
    zero = jnp.zeros_like(qb)
    qq = jnp.concatenate([jnp.where(lane < HEAD_DIM, qb, zero),
                          jnp.where(lane >= HEAD_DIM, qb, zero)], axis=0)
    s = lax.dot_general(qq, kw, (((1,), (1,)), ((), ())), preferred_element_type=F32) + bias
    m = jnp.max(s, axis=-1, keepdims=True)
    p = jnp.exp(s - m)
    l = jnp.sum(p, axis=-1, keepdims=True)
    pv = jnp.dot(p.astype(BF16), vw, preferred_element_type=F32)
    return m, l, pv


def _merge_heads(top, bottom, q):
    lane = lax.broadcasted_iota(jnp.int32, (q, LANES), 1)
    return jnp.where(lane < HEAD_DIM, jnp.broadcast_to(top, (q, LANES)),
                     jnp.broadcast_to(bottom, (q, LANES)))


_DQ = 128
_DW = 256
_UNROLL = 16


def _dilated_tables():
    slopes = 2.0 ** (-(np.arange(N_HEADS) + 1.0))
    q = np.arange(_DQ)[:, None]
    t12 = np.zeros((N_PAIRS, 2, 3, 2 * _DQ, _DW), np.float32)
    k = np.arange(_DW)[None, :]
    for pat, dil in enumerate((1, 4)):
        for case, off in enumerate((0, _DW // 4, _DW // 2)):
            delta = np.abs(k - (q + off))
            for h in range(N_HEADS):
                tab = np.where(delta <= DIL_RADIUS, -slopes[h] * dil * delta, NEG_INF)
                t12[h // 2, pat, case, (h % 2) * _DQ:(h % 2 + 1) * _DQ] = tab
    t3 = np.zeros((N_PAIRS, 2 * _DQ, _DQ), np.float32)
    delta = np.abs(np.arange(_DQ)[None, :] - q)
    for h in range(N_HEADS):
        t3[h // 2, (h % 2) * _DQ:(h % 2 + 1) * _DQ] = np.where(
            delta <= DIL_RADIUS, -slopes[h] * 16 * delta, NEG_INF)
    return t12, t3


def _dilated_kernel(q_ref, k_ref, v_ref, t12_ref, t3_ref, o_ref,
                    tmp, tmp4, q4, k4, v4, q16, k16, v16,
                    m1, l1, a1, m2, l2, a2, m3, l3, a3, onat):
    S = tmp.shape[0]
    L4, L16 = S // 4, S // 16

    for src, d4, d16 in ((q_ref, q4, q16), (k_ref, k4, k16), (v_ref, v4, v16)):
        tmp[...] = src[0, 0].astype(F32)
        for r in range(4):
            sub = tmp[pl.ds(r, L4, stride=4), :]
            tmp4[r * L4:(r + 1) * L4, :] = sub
            d4[r * L4:(r + 1) * L4, :] = sub.astype(BF16)
        for r16 in range(16):
            r4, c4 = r16 % 4, r16 // 4
            d16[r16 * L16:(r16 + 1) * L16, :] = tmp4[pl.ds(r4 * L4 + c4, L16, stride=4), :].astype(BF16)

    def block(qb, kw, vw, bias, m_ref, l_ref, a_ref, row):
        m, l, pv = _attend(qb, kw, vw, bias)
        m_ref[pl.ds(row, _DQ), :] = _merge_heads(m[:_DQ], m[_DQ:], _DQ)
        l_ref[pl.ds(row, _DQ), :] = _merge_heads(l[:_DQ], l[_DQ:], _DQ)
        a_ref[pl.ds(row, _DQ), :] = _merge_heads(pv[:_DQ], pv[_DQ:], _DQ)

    def case_of(blk, n_blk):
        return jnp.where(blk == 0, 0, jnp.where(blk == n_blk - 1, 2, 1))

    n1 = S // _DQ

    def p1_body(blk, carry):
        t0 = pl.multiple_of(blk * _DQ, _DQ)
        ws = pl.multiple_of(jnp.clip(t0 - DIL_RADIUS, 0, S - _DW), DIL_RADIUS)
        block(q_ref[0, 0, pl.ds(t0, _DQ), :], k_ref[0, 0, pl.ds(ws, _DW), :],
              v_ref[0, 0, pl.ds(ws, _DW), :], t12_ref[0, 0, case_of(blk, n1)], m1, l1, a1, t0)
        return carry

    lax.fori_loop(0, n1, p1_body, 0, unroll=_UNROLL)

    n2 = L4 // _DQ

    def p2_body(j, carry):
        r = j // n2
        blk = j % n2
        l0 = blk * _DQ
        ws = jnp.clip(l0 - DIL_RADIUS, 0, L4 - _DW)
        row = pl.multiple_of(r * L4 + l0, _DQ)
        krow = pl.multiple_of(r * L4 + ws, DIL_RADIUS)
        block(q4[pl.ds(row, _DQ), :], k4[pl.ds(krow, _DW), :], v4[pl.ds(krow, _DW), :],
              t12_ref[0, 1, case_of(blk, n2)], m2, l2, a2, row)
        return carry

    lax.fori_loop(0, 4 * n2, p2_body, 0, unroll=_UNROLL)

    def p3_body(r, carry):
        row = pl.multiple_of(r * L16, L16)
        block(q16[pl.ds(row, L16), :], k16[pl.ds(row, L16), :], v16[pl.ds(row, L16), :],
              t3_ref[0], m3, l3, a3, row)
        return carry

    lax.fori_loop(0, 16, p3_body, 0, unroll=_UNROLL)

    for r16 in range(16):
        r4, c4 = r16 % 4, r16 // 4
        via4 = pl.ds(r4 * L4 + c4, L16, stride=4)
        via16 = pl.ds(r16 * L16, L16)
        mb, mc = m2[via4, :], m3[via16, :]
        mx = jnp.maximum(mb, mc)
        wb, wc = jnp.exp(mb - mx), jnp.exp(mc - mx)
        l2[via4, :] = wb * l2[via4, :] + wc * l3[via16, :]
        a2[via4, :] = wb * a2[via4, :] + wc * a3[via16, :]
        m2[via4, :] = mx
    for r4 in range(4):
        for part in range(L4 // _DQ):
            nat = pl.ds(r4 + 4 * _DQ * part, _DQ, stride=4)
            via4 = pl.ds(r4 * L4 + _DQ * part, _DQ)
            ma, mb = m1[nat, :], m2[via4, :]
            mx = jnp.maximum(ma, mb)
            wa, wb = jnp.exp(ma - mx), jnp.exp(mb - mx)
            den = wa * l1[nat, :] + wb * l2[via4, :]
            num = wa * a1[nat, :] + wb * a2[via4, :]
            onat[nat, :] = num / den
    o_ref[0] = onat[...].astype(BF16)


def _dilated_attention(qkv, t12, t3):
    B, _, S, _ = qkv.shape
    f32_buf = pltpu.VMEM((S, LANES), F32)
    bf16_buf = pltpu.VMEM((S, LANES), BF16)
    slab = lambda off: pl.BlockSpec((1, 1, S, LANES), lambda p, b: (b, off + p, 0, 0))
    return pl.pallas_call(
        _dilated_kernel,
        grid=(N_PAIRS, B),
        in_specs=[
            slab(0), slab(N_PAIRS), slab(2 * N_PAIRS),
            pl.BlockSpec((1, 2, 3, 2 * _DQ, _DW), lambda p, b: (p, 0, 0, 0, 0)),
            pl.BlockSpec((1, 2 * _DQ, _DQ), lambda p, b: (p, 0, 0)),
        ],
        out_specs=pl.BlockSpec((1, S, LANES), lambda p, b: (b, 0, p)),
        out_shape=jax.ShapeDtypeStruct((B, S, WIDTH), BF16),
        scratch_shapes=[f32_buf] * 2 + [bf16_buf] * 6 + [f32_buf] * 10,
        compiler_params=pltpu.CompilerParams(
            dimension_semantics=("arbitrary", "arbitrary"), vmem_limit_bytes=VMEM_LIMIT),
        name="dilated_attn",
    )(qkv, qkv, qkv, t12, t3)


_NQ_ROWS = 4
_NK_ROWS = 12


def _na_row_select(rows):
    n_blk = rows // _NQ_ROWS
    sel = np.full((n_blk, _NQ_ROWS, _NK_ROWS), -1, np.int64)
    for i in range(n_blk):
        kr0 = min(max(_NQ_ROWS * i - NA_WIN_ROWS // 2, 0), rows - _NK_ROWS)
        for a in range(_NQ_ROWS):
            qr = _NQ_ROWS * i + a
            rs = min(max(qr - NA_WIN_ROWS // 2, 0), rows - NA_WIN_ROWS)
            for b in range(_NK_ROWS):
                kr = kr0 + b
                if rs <= kr < rs + NA_WIN_ROWS:
                    sel[i, a, b] = kr - qr + NA_WIN_ROWS - 1
    for i in range(2, n_blk - 1):
        assert np.array_equal(sel[1], sel[i])
    return sel[[0, 1, n_blk - 1]]


def _na_bias_table(rpb, rows):
    n_dr, n_dc = 2 * NA_WIN_ROWS - 1, 2 * NA_WIN_COLS - 1
    qc = np.arange(GRID_W)[:, None]
    kc = np.arange(GRID_W)[None, :]
    cs = np.clip(qc - NA_WIN_COLS // 2, 0, GRID_W - NA_WIN_COLS)
    col_ok = (kc >= cs) & (kc < cs + NA_WIN_COLS)
    dc = np.clip(kc - qc + NA_WIN_COLS - 1, 0, n_dc - 1)
    onehot = (dc.reshape(1, -1) == np.arange(n_dc)[:, None]).astype(np.float32)
    t = jnp.dot(rpb.astype(F32).reshape(N_HEADS * n_dr, n_dc), onehot, precision=lax.Precision.HIGHEST)
    t = jnp.where(col_ok[None, None], t.reshape(N_HEADS, n_dr, GRID_W, GRID_W), NEG_INF)
    neg = jnp.full((N_HEADS, GRID_W, GRID_W), NEG_INF, F32)
    sel = _na_row_select(rows)
    cases = []
    for c in range(sel.shape[0]):
        per_a = [jnp.stack([neg if s < 0 else t[:, s] for s in sel[c, a]], axis=2)
                 for a in range(_NQ_ROWS)]
        cases.append(jnp.stack(per_a, axis=1))
    return jnp.stack(cases, axis=0).reshape(sel.shape[0], N_HEADS, _NQ_ROWS * GRID_W, _NK_ROWS * GRID_W)


def _na_kernel(q_ref, k_ref, v_ref, tab_ref, o_ref):
    S = q_ref.shape[2]
    rows = S // GRID_W
    n_blk = rows // _NQ_ROWS
    nq = _NQ_ROWS * GRID_W
    nk = _NK_ROWS * GRID_W

    def body(i, carry):
        q0 = pl.multiple_of(i * nq, nq)
        kr0 = jnp.clip(_NQ_ROWS * i - NA_WIN_ROWS // 2, 0, rows - _NK_ROWS)
        k0 = pl.multiple_of(kr0 * GRID_W, GRID_W)
        case = jnp.where(i == 0, 0, jnp.where(i == n_blk - 1, 2, 1))
        m, l, pv = _attend(q_ref[0, 0, pl.ds(q0, nq), :], k_ref[0, 0, pl.ds(k0, nk), :],
                           v_ref[0, 0, pl.ds(k0, nk), :], tab_ref[case].reshape(2 * nq, nk))
        o = pv / l
        lane = lax.broadcasted_iota(jnp.int32, (nq, LANES), 1)
        o_ref[0, pl.ds(q0, nq), :] = jnp.where(lane < HEAD_DIM, o[:nq], o[nq:]).astype(BF16)
        return carry

    lax.fori_loop(0, n_blk, body, 0, unroll=8)


def _na_attention(qkv, tab):
    B, _, S, _ = qkv.shape
    slab = lambda off: pl.BlockSpec((1, 1, S, LANES), lambda p, b: (b, off + p, 0, 0))
    nq, nk = _NQ_ROWS * GRID_W, _NK_ROWS * GRID_W
    return pl.pallas_call(
        _na_kernel,
        grid=(N_PAIRS, B),
        in_specs=[
            slab(3 * N_PAIRS), slab(4 * N_PAIRS), slab(5 * N_PAIRS),
            pl.BlockSpec((3, 2, nq, nk), lambda p, b: (0, p, 0, 0)),
        ],
        out_specs=pl.BlockSpec((1, S, LANES), lambda p, b: (b, 0, p)),
        out_shape=jax.ShapeDtypeStruct((B, S, WIDTH), BF16),
        compiler_params=pltpu.CompilerParams(
            dimension_semantics=("arbitrary", "arbitrary"), vmem_limit_bytes=VMEM_LIMIT),
        name="na_attn",
    )(qkv, qkv, qkv, tab)


_GROUP_LANE0 = N_EXPERTS
_ROUTE_GROUP_LANE = EXPERTS_PER_GROUP


def _route(logits):
    lane = lax.broadcasted_iota(jnp.int32, logits.shape, 1)
    big = jnp.int32(LANES)
    is_group = (lane >= _GROUP_LANE0) & (lane < _GROUP_LANE0 + N_GROUPS)
    gl = jnp.where(is_group, logits, NEG_INF)
    gmax = jnp.max(gl, axis=-1, keepdims=True)
    g_idx = jnp.min(jnp.where(is_group & (gl == gmax), lane, big), axis=-1, keepdims=True) - _GROUP_LANE0
    g_weight = 1.0 / jnp.sum(jnp.where(is_group, jnp.exp(gl - gmax), 0.0), axis=-1, keepdims=True)
    in_group = (lane < N_EXPERTS) & ((lane // EXPERTS_PER_GROUP) == g_idx)
    el = jnp.where(in_group, logits, NEG_INF)
    v1 = jnp.max(el, axis=-1, keepdims=True)
    i1 = jnp.min(jnp.where(in_group & (el == v1), lane, big), axis=-1, keepdims=True)
    rest = in_group & (lane != i1)
    el2 = jnp.where(rest, logits, NEG_INF)
    v2 = jnp.max(el2, axis=-1, keepdims=True)
    i2 = jnp.min(jnp.where(rest & (el2 == v2), lane, big), axis=-1, keepdims=True)
    e2 = jnp.exp(v2 - v1)
    w1 = g_weight / (1.0 + e2)
    w2 = g_weight * e2 / (1.0 + e2)
    base = g_idx * EXPERTS_PER_GROUP
    return jnp.where(lane == i1 - base, w1,
                     jnp.where(lane == i2 - base, w2,
                               jnp.where(lane == _ROUTE_GROUP_LANE, g_idx.astype(F32), 0.0)))


def _outproj_kernel(x_ref, yd_ref, yn_ref, gd_ref, gn_ref, wo_ref, gf_ref, wr_hi_ref, wr_lo_ref,
                    br_ref, h_ref, hn_ref, comb_ref):
    yd = _rms(yd_ref[...].astype(F32), gd_ref[...]).astype(BF16)
    yn = _rms(yn_ref[...].astype(F32), gn_ref[...]).astype(BF16)
    h = (x_ref[...] + jnp.dot(yd, wo_ref[:WIDTH, :], preferred_element_type=F32)
         + jnp.dot(yn, wo_ref[WIDTH:, :], preferred_element_type=F32))
    h_ref[...] = h
    hn = _rms(h, gf_ref[...])
    hi = hn.astype(BF16)
    lo = (hn - hi.astype(F32)).astype(BF16)
    hn_ref[...] = hi
    logits = (jnp.dot(hi, wr_hi_ref[...], preferred_element_type=F32)
              + (jnp.dot(lo, wr_hi_ref[...], preferred_element_type=F32)
                 + jnp.dot(hi, wr_lo_ref[...], preferred_element_type=F32))
              + br_ref[...])
    comb_ref[...] = _route(logits)


def _outproj(x2, yd2, yn2, gd, gn, wo, gf, wr_hi, wr_lo, br, tm):
    N, D = x2.shape
    row = lambda w: pl.BlockSpec((tm, w), lambda i: (i, 0))
    full = lambda a, b: pl.BlockSpec((a, b), lambda i: (0, 0))
    return pl.pallas_call(
        _outproj_kernel,
        grid=(N // tm,),
        in_specs=[row(D), row(WIDTH), row(WIDTH), full(1, WIDTH), full(1, WIDTH), full(2 * WIDTH, D),
                  full(1, D), full(D, LANES), full(D, LANES), full(1, LANES)],
        out_specs=[row(D), row(D), row(LANES)],
        out_shape=[jax.ShapeDtypeStruct((N, D), F32), jax.ShapeDtypeStruct((N, D), BF16),
                   jax.ShapeDtypeStruct((N, LANES), F32)],
        compiler_params=pltpu.CompilerParams(
            dimension_semantics=("arbitrary",), vmem_limit_bytes=VMEM_LIMIT),
        name="outproj_route",
    )(x2, yd2, yn2, gd, gn, wo, gf, wr_hi, wr_lo, br)


_MOE_TILE = 1024
_MOE_CHUNK = 128
_MOE_NCHUNK = _MOE_TILE // _MOE_CHUNK + N_GROUPS
_GROUP_WIDTH = EXPERTS_PER_GROUP * D_EXPERT


def _moe_kernel(hn_ref, route_ref, wgu_ref, wd_ref, o_ref, ys_ref):
    T, C = _MOE_TILE, _MOE_CHUNK
    route = route_ref[...]
    lane = lax.broadcasted_iota(jnp.int32, (T, LANES), 1)
    gid = jnp.sum(jnp.where(lane == _ROUTE_GROUP_LANE, route, 0.0), axis=-1, keepdims=True)
    onehot = jnp.where((lane < N_GROUPS) & (lane.astype(F32) == gid), 1.0, 0.0)

    before = (lax.broadcasted_iota(jnp.int32, (T, T), 1)
              < lax.broadcasted_iota(jnp.int32, (T, T), 0)).astype(BF16)
    rank = jnp.dot(before, onehot.astype(BF16), preferred_element_type=F32)
    count = jnp.sum(onehot, axis=0, keepdims=True)
    nchunk = jnp.floor((count + (C - 1)) * (1.0 / C)).astype(jnp.int32)
    off1 = nchunk[0, 0]
    off2 = off1 + nchunk[0, 1]
    off3 = off2 + nchunk[0, 2]
    n_used = off3 + nchunk[0, 3]
    start = jnp.where(lane == 1, off1, jnp.where(lane == 2, off2, jnp.where(lane == 3, off3, 0)))
    pos = jnp.sum(onehot * (rank + (start * C).astype(F32)), axis=-1, keepdims=True)
    pos_i = pos.astype(jnp.int32)
    pos_row = jnp.transpose(jnp.broadcast_to(pos, (T, LANES)))[0:1, :].astype(jnp.int32)

    r_hi = route.astype(BF16)
    r_mid = (route - r_hi.astype(F32)).astype(BF16)
    r_lo = (route - r_hi.astype(F32) - r_mid.astype(F32)).astype(BF16)
    route3 = jnp.concatenate([r_hi, r_mid, r_lo], axis=-1)

    ys_ref[...] = jnp.zeros_like(ys_ref)
    hn = hn_ref[...]

    def chunk_body(c, carry):
        g = ((c >= off1).astype(jnp.int32) + (c >= off2).astype(jnp.int32)
             + (c >= off3).astype(jnp.int32))
        row0 = pl.multiple_of(c * C, C)
        sel = (pos_row == row0 + lax.broadcasted_iota(jnp.int32, (C, T), 0)).astype(BF16)
        xs = jnp.dot(sel, hn, preferred_element_type=F32).astype(BF16)
        r3 = jnp.dot(sel, route3, preferred_element_type=F32)
        r = r3[:, :LANES] + r3[:, LANES:2 * LANES] + r3[:, 2 * LANES:]
        gu = jnp.dot(xs, wgu_ref[g], preferred_element_type=F32)
        gate, up = gu[:, :_GROUP_WIDTH], gu[:, _GROUP_WIDTH:]
        act = (gate / (1.0 + jnp.exp(-gate))) * up
        clane = lax.broadcasted_iota(jnp.int32, (C, LANES), 1)
        parts = []
        for j in range(EXPERTS_PER_GROUP):
            wj = jnp.sum(jnp.where(clane == j, r, 0.0), axis=-1, keepdims=True)
            parts.append(act[:, j * D_EXPERT:(j + 1) * D_EXPERT] * wj)
        act = jnp.concatenate(parts, axis=-1).astype(BF16)
        ys_ref[pl.ds(row0, C), :] = jnp.dot(act, wd_ref[g], preferred_element_type=F32).astype(BF16)
        return carry

    lax.fori_loop(0, n_used, chunk_body, 0)

    back = (lax.broadcasted_iota(jnp.int32, (T, _MOE_NCHUNK * C), 1) == pos_i).astype(BF16)
    o_ref[...] = jnp.dot(back, ys_ref[...], preferred_element_type=F32).astype(BF16)


def _moe(hn, route, wgu, wd):
    N, D = hn.shape
    T = _MOE_TILE
    row = lambda w: pl.BlockSpec((T, w), lambda i: (i, 0))
    whole = lambda a: pl.BlockSpec(a.shape, lambda i: (0,) * a.ndim)
    return pl.pallas_call(
        _moe_kernel,
        grid=(N // T,),
        in_specs=[row(D), row(LANES), whole(wgu), whole(wd)],
        out_specs=row(D),
        out_shape=jax.ShapeDtypeStruct((N, D), BF16),
        scratch_shapes=[pltpu.VMEM((_MOE_NCHUNK * _MOE_CHUNK, D), BF16)],
        compiler_params=pltpu.CompilerParams(
            dimension_semantics=("arbitrary",), vmem_limit_bytes=VMEM_LIMIT),
        name="moe_grouped",
    )(hn, route, wgu, wd)


def _final_kernel(h_ref, m_ref, g_ref, y_ref):
    y_ref[...] = _rms(h_ref[...] + m_ref[...].astype(F32), g_ref[...])


def _final(h, moe_out, gfin, tm):
    N, D = h.shape
    row = pl.BlockSpec((tm, D), lambda i: (i, 0))
    return pl.pallas_call(
        _final_kernel,
        grid=(N // tm,),
        in_specs=[row, row, pl.BlockSpec((1, D), lambda i: (0, 0))],
        out_specs=row,
        out_shape=jax.ShapeDtypeStruct((N, D), F32),
        compiler_params=pltpu.CompilerParams(
            dimension_semantics=("arbitrary",), vmem_limit_bytes=VMEM_LIMIT),
        name="final_norm",
    )(h, moe_out, gfin)


def kernel(x, norm_mix_g, w_in, rpb, g_out_dil, g_out_na, w_out, norm_ffn_g, w_group, b_group,
           w_router, b_router, w_gate, w_up, w_down, norm_final_g):
    B, S, D = x.shape
    N = B * S
    depth = w_in.shape[0]
    assert depth == 1 and D == D_MODEL and S % (16 * _DQ) == 0

    t12, t3 = _dilated_tables()

    layer = 0
    col_scale = np.ones((6, WIDTH), np.float32)
    col_scale[0] = col_scale[3] = HEAD_DIM ** -0.5
    w_in_b = (w_in[layer] * col_scale.reshape(1, -1)).astype(BF16)

    qkv = _inproj(x, norm_mix_g[layer].reshape(1, D), w_in_b, tm=512)

    y_dil = _dilated_attention(qkv, jnp.asarray(t12), jnp.asarray(t3))

    y_na = _na_attention(qkv, _na_bias_table(rpb[layer], S // GRID_W))

    w_r = jnp.zeros((D, LANES), F32)
    w_r = w_r.at[:, :N_EXPERTS].set(w_router[layer]).at[:, _GROUP_LANE0:_GROUP_LANE0 + N_GROUPS].set(w_group[layer])
    w_r_hi = w_r.astype(BF16)
    w_r_lo = (w_r - w_r_hi.astype(F32)).astype(BF16)
    b_r = jnp.zeros((1, LANES), F32)
    b_r = b_r.at[0, :N_EXPERTS].set(b_router[layer]).at[0, _GROUP_LANE0:_GROUP_LANE0 + N_GROUPS].set(b_group[layer])

    h, hn, route = _outproj(
        x.reshape(N, D), y_dil.reshape(N, WIDTH), y_na.reshape(N, WIDTH),
        g_out_dil[layer].reshape(1, WIDTH), g_out_na[layer].reshape(1, WIDTH),
        w_out[layer].astype(BF16), norm_ffn_g[layer].reshape(1, D), w_r_hi, w_r_lo, b_r, tm=512)

    def group_cols(w):
        w = w.astype(BF16).reshape(N_GROUPS, EXPERTS_PER_GROUP, D, D_EXPERT)
        return w.transpose(0, 2, 1, 3).reshape(N_GROUPS, D, _GROUP_WIDTH)

    wgu = jnp.concatenate([group_cols(w_gate[layer]), group_cols(w_up[layer])], axis=-1)
    wd = w_down[layer].astype(BF16).reshape(N_GROUPS, _GROUP_WIDTH, D)

    moe_out = _moe(hn, route, wgu, wd)
    y = _final(h, moe_out, norm_final_g.reshape(1, D), tm=1024)
    return y.reshape(B, S, D)
```

```python
import functools

import numpy as np
import jax
import jax.numpy as jnp
from jax import lax
from jax.experimental import pallas as pl
from jax.experimental.pallas import tpu as pltpu

D_MODEL = 1024
HEAD_DIM = 64
N_HEADS = 8
N_PAIRS = N_HEADS // 2
WIDTH = N_HEADS * HEAD_DIM
N_SLABS = 6 * N_PAIRS
DIL_PATTERNS = ((128, 1), (512, 4), (2048, 16))
DIL_RADIUS = 64
GRID_W = 64
NA_WIN_ROWS = 8
NA_WIN_COLS = 16
N_GROUPS = 4
EXPERTS_PER_GROUP = 4
N_EXPERTS = 16
D_EXPERT = 256
RMS_EPS = 1e-6
NEG_INF = -1e30

LANES = 128
VMEM_LIMIT = 48 * 1024 * 1024

F32 = jnp.float32
BF16 = jnp.bfloat16


def _rms(x, gain):
    return x * lax.rsqrt(jnp.mean(x * x, axis=-1, keepdims=True) + RMS_EPS) * gain


def _inproj_kernel(x_ref, g_ref, w_ref, o_ref):
    xn = _rms(x_ref[0], g_ref[...]).astype(BF16)
    chunk = 4 * LANES
    for c in range(N_SLABS * LANES // chunk):
        acc = jnp.dot(xn, w_ref[:, c * chunk:(c + 1) * chunk], preferred_element_type=F32)
        for j in range(chunk // LANES):
            o_ref[0, c * (chunk // LANES) + j] = acc[:, j * LANES:(j + 1) * LANES].astype(BF16)


def _inproj(x, gain, w_bf16, tm):
    B, S, D = x.shape
    return pl.pallas_call(
        _inproj_kernel,
        grid=(B, S // tm),
        in_specs=[
            pl.BlockSpec((1, tm, D), lambda b, i: (b, i, 0)),
            pl.BlockSpec((1, D), lambda b, i: (0, 0)),
            pl.BlockSpec((D, N_SLABS * LANES), lambda b, i: (0, 0)),
        ],
        out_specs=pl.BlockSpec((1, N_SLABS, tm, LANES), lambda b, i: (b, 0, i, 0)),
        out_shape=jax.ShapeDtypeStruct((B, N_SLABS, S, LANES), BF16),
        compiler_params=pltpu.CompilerParams(
            dimension_semantics=("arbitrary", "arbitrary"), vmem_limit_bytes=VMEM_LIMIT),
        name="inproj",
    )(x, gain, w_bf16)


def _attend(qb, kw, vw, bias):
    lane = lax.broadcasted_iota(jnp.int32, qb.shape, 1)
    zero = jnp.zeros_like(qb)
    qq = jnp.concatenate([jnp.where(lane < HEAD_DIM, qb, zero),
                          jnp.where(lane >= HEAD_DIM, qb, zero)], axis=0)
    s = lax.dot_general(qq, kw, (((1,), (1,)), ((), ())), preferred_element_type=F32) + bias
    m = jnp.max(s, axis=-1, keepdims=True)
    p = jnp.exp(s - m)
    l = jnp.sum(p, axis=-1, keepdims=True)
    pv = jnp.dot(p.astype(BF16), vw, preferred_element_type=F32)
    return m, l, pv


def _merge_heads(top, bottom, q):
    lane = lax.broadcasted_iota(jnp.int32, (q, LANES), 1)
    return jnp.where(lane < HEAD_DIM, jnp.broadcast_to(top, (q, LANES)),
                     jnp.broadcast_to(bottom, (q, LANES)))


_DQ = 128
_DW = 256
_UNROLL = 16


def _dilated_tables():
    slopes = 2.0 ** (-(np.arange(N_HEADS) + 1.0))
    q = np.arange(_DQ)[:, None]
    t12 = np.zeros((N_PAIRS, 2, 3, 2 * _DQ, _DW), np.float32)
    k = np.arange(_DW)[None, :]
    for pat, dil in enumerate((1, 4)):
        for case, off in enumerate((0, _DW // 4, _DW // 2)):
            delta = np.abs(k - (q + off))
            for h in range(N_HEADS):
                tab = np.where(delta <= DIL_RADIUS, -slopes[h] * dil * delta, NEG_INF)
                t12[h // 2, pat, case, (h % 2) * _DQ:(h % 2 + 1) * _DQ] = tab
    t3 = np.zeros((N_PAIRS, 2 * _DQ, _DQ), np.float32)
    delta = np.abs(np.arange(_DQ)[None, :] - q)
    for h in range(N_HEADS):
        t3[h // 2, (h % 2) * _DQ:(h % 2 + 1) * _DQ] = np.where(
            delta <= DIL_RADIUS, -slopes[h] * 16 * delta, NEG_INF)
    return t12, t3


def _dilated_kernel(q_ref, k_ref, v_ref, t12_ref, t3_ref, o_ref,
                    tmp, tmp4, q4, k4, v4, q16, k16, v16,
                    m1, l1, a1, m2, l2, a2, m3, l3, a3, onat):
    S = tmp.shape[0]
    L4, L16 = S // 4, S // 16

    for src, d4, d16 in ((q_ref, q4, q16), (k_ref, k4, k16), (v_ref, v4, v16)):
        tmp[...] = src[0, 0].astype(F32)
        for r in range(4):
            sub = tmp[pl.ds(r, L4, stride=4), :]
            tmp4[r * L4:(r + 1) * L4, :] = sub
            d4[r * L4:(r + 1) * L4, :] = sub.astype(BF16)
        for r16 in range(16):
            r4, c4 = r16 % 4, r16 // 4
            d16[r16 * L16:(r16 + 1) * L16, :] = tmp4[pl.ds(r4 * L4 + c4, L16, stride=4), :].astype(BF16)

    def block(qb, kw, vw, bias, m_ref, l_ref, a_ref, row):
        m, l, pv = _attend(qb, kw, vw, bias)
        m_ref[pl.ds(row, _DQ), :] = _merge_heads(m[:_DQ], m[_DQ:], _DQ)
        l_ref[pl.ds(row, _DQ), :] = _merge_heads(l[:_DQ], l[_DQ:], _DQ)
        a_ref[pl.ds(row, _DQ), :] = _merge_heads(pv[:_DQ], pv[_DQ:], _DQ)

    def case_of(blk, n_blk):
        return jnp.where(blk == 0, 0, jnp.where(blk == n_blk - 1, 2, 1))

    n1 = S // _DQ

    def p1_body(blk, carry):
        t0 = pl.multiple_of(blk * _DQ, _DQ)
        ws = pl.multiple_of(jnp.clip(t0 - DIL_RADIUS, 0, S - _DW), DIL_RADIUS)
        block(q_ref[0, 0, pl.ds(t0, _DQ), :], k_ref[0, 0, pl.ds(ws, _DW), :],
              v_ref[0, 0, pl.ds(ws, _DW), :], t12_ref[0, 0, case_of(blk, n1)], m1, l1, a1, t0)
        return carry

    lax.fori_loop(0, n1, p1_body, 0, unroll=_UNROLL)

    n2 = L4 // _DQ

    def p2_body(j, carry):
        r = j // n2
        blk = j % n2
        l0 = blk * _DQ
        ws = jnp.clip(l0 - DIL_RADIUS, 0, L4 - _DW)
        row = pl.multiple_of(r * L4 + l0, _DQ)
        krow = pl.multiple_of(r * L4 + ws, DIL_RADIUS)
        block(q4[pl.ds(row, _DQ), :], k4[pl.ds(krow, _DW), :], v4[pl.ds(krow, _DW), :],
              t12_ref[0, 1, case_of(blk, n2)], m2, l2, a2, row)
        return carry

    lax.fori_loop(0, 4 * n2, p2_body, 0, unroll=_UNROLL)

    def p3_body(r, carry):
        row = pl.multiple_of(r * L16, L16)
        block(q16[pl.ds(row, L16), :], k16[pl.ds(row, L16), :], v16[pl.ds(row, L16), :],
              t3_ref[0], m3, l3, a3, row)
        return carry

    lax.fori_loop(0, 16, p3_body, 0, unroll=_UNROLL)

    for r16 in range(16):
        r4, c4 = r16 % 4, r16 // 4
        via4 = pl.ds(r4 * L4 + c4, L16, stride=4)
        via16 = pl.ds(r16 * L16, L16)
        mb, mc = m2[via4, :], m3[via16, :]
        mx = jnp.maximum(mb, mc)
        wb, wc = jnp.exp(mb - mx), jnp.exp(mc - mx)
        l2[via4, :] = wb * l2[via4, :] + wc * l3[via16, :]
        a2[via4, :] = wb * a2[via4, :] + wc * a3[via16, :]
        m2[via4, :] = mx
    for r4 in range(4):
        for part in range(L4 // _DQ):
            nat = pl.ds(r4 + 4 * _DQ * part, _DQ, stride=4)
            via4 = pl.ds(r4 * L4 + _DQ * part, _DQ)
            ma, mb = m1[nat, :], m2[via4, :]
            mx = jnp.maximum(ma, mb)
            wa, wb = jnp.exp(ma - mx), jnp.exp(mb - mx)
            den = wa * l1[nat, :] + wb * l2[via4, :]
            num = wa * a1[nat, :] + wb * a2[via4, :]
            onat[nat, :] = num / den
    o_ref[0] = onat[...].astype(BF16)


def _dilated_attention(qkv, t12, t3):
    B, _, S, _ = qkv.shape
    f32_buf = pltpu.VMEM((S, LANES), F32)
    bf16_buf = pltpu.VMEM((S, LANES), BF16)
    slab = lambda off: pl.BlockSpec((1, 1, S, LANES), lambda p, b: (b, off + p, 0, 0))
    return pl.pallas_call(
        _dilated_kernel,
        grid=(N_PAIRS, B),
        in_specs=[
            slab(0), slab(N_PAIRS), slab(2 * N_PAIRS),
            pl.BlockSpec((1, 2, 3, 2 * _DQ, _DW), lambda p, b: (p, 0, 0, 0, 0)),
            pl.BlockSpec((1, 2 * _DQ, _DQ), lambda p, b: (p, 0, 0)),
        ],
        out_specs=pl.BlockSpec((1, S, LANES), lambda p, b: (b, 0, p)),
        out_shape=jax.ShapeDtypeStruct((B, S, WIDTH), BF16),
        scratch_shapes=[f32_buf] * 2 + [bf16_buf] * 6 + [f32_buf] * 10,
        compiler_params=pltpu.CompilerParams(
            dimension_semantics=("arbitrary", "arbitrary"), vmem_limit_bytes=VMEM_LIMIT),
        name="dilated_attn",
    )(qkv, qkv, qkv, t12, t3)


_NQ_ROWS = 4
_NK_ROWS = 12


def _na_row_select(rows):
    n_blk = rows // _NQ_ROWS
    sel = np.full((n_blk, _NQ_ROWS, _NK_ROWS), -1, np.int64)
    for i in range(n_blk):
        kr0 = min(max(_NQ_ROWS * i - NA_WIN_ROWS // 2, 0), rows - _NK_ROWS)
        for a in range(_NQ_ROWS):
            qr = _NQ_ROWS * i + a
            rs = min(max(qr - NA_WIN_ROWS // 2, 0), rows - NA_WIN_ROWS)
            for b in range(_NK_ROWS):
                kr = kr0 + b
                if rs <= kr < rs + NA_WIN_ROWS:
                    sel[i, a, b] = kr - qr + NA_WIN_ROWS - 1
    for i in range(2, n_blk - 1):
        assert np.array_equal(sel[1], sel[i])
    return sel[[0, 1, n_blk - 1]]


def _na_column_bias(rpb):
    n_dr, n_dc = 2 * NA_WIN_ROWS - 1, 2 * NA_WIN_COLS - 1
    qc = np.arange(GRID_W)[:, None]
    kc = np.arange(GRID_W)[None, :]
    cs = np.clip(qc - NA_WIN_COLS // 2, 0, GRID_W - NA_WIN_COLS)
    col_ok = (kc >= cs) & (kc < cs + NA_WIN_COLS)
    dc = np.clip(kc - qc + NA_WIN_COLS - 1, 0, n_dc - 1)
    onehot = (dc.reshape(1, -1) == np.arange(n_dc)[:, None]).astype(np.float32)
    t = jnp.dot(rpb.astype(F32).reshape(N_HEADS * n_dr, n_dc), onehot, precision=lax.Precision.HIGHEST)
    t = jnp.where(col_ok[None, None], t.reshape(N_HEADS, n_dr, GRID_W, GRID_W), NEG_INF)
    neg = jnp.full((N_HEADS, 1, GRID_W, GRID_W), NEG_INF, F32)
    ext = jnp.concatenate([neg, t, neg], axis=1)
    return jnp.concatenate([ext[:, :n_dr + 1], ext[:, 1:]], axis=-1)


def _na_kernel(q_ref, k_ref, v_ref, cb_ref, o_ref, tab_ref):
    S = q_ref.shape[2]
    rows = S // GRID_W
    n_blk = rows // _NQ_ROWS
    nq = _NQ_ROWS * GRID_W
    nk = _NK_ROWS * GRID_W

    @pl.when(pl.program_id(1) == 0)
    def _():
        sel = _na_row_select(rows)
        lane = lax.broadcasted_iota(jnp.int32, (GRID_W, LANES), 1)
        masked = jnp.full((GRID_W, LANES), NEG_INF, F32)
        for c in range(sel.shape[0]):
            for a in range(_NQ_ROWS):
                for j in range(_NK_ROWS // 2):
                    s0, s1 = int(sel[c, a, 2 * j]), int(sel[c, a, 2 * j + 1])
                    for h in range(2):
                        if s0 < 0 and s1 < 0:
                            tile = masked
                        else:
                            d = s0 if s0 >= 0 else s1 - 1
                            assert s1 < 0 or s1 == d + 1
                            tile = cb_ref[h, d + 1]
                            if s0 < 0:
                                tile = jnp.where(lane >= GRID_W, tile, NEG_INF)
                            elif s1 < 0:
                                tile = jnp.where(lane < GRID_W, tile, NEG_INF)
                        r0 = h * nq + a * GRID_W
                        tab_ref[c, r0:r0 + GRID_W, j * LANES:(j + 1) * LANES] = tile

    def body(i, carry):
        q0 = pl.multiple_of(i * nq, nq)
        kr0 = jnp.clip(_NQ_ROWS * i - NA_WIN_ROWS // 2, 0, rows - _NK_ROWS)
        k0 = pl.multiple_of(kr0 * GRID_W, GRID_W)
        case = jnp.where(i == 0, 0, jnp.where(i == n_blk - 1, 2, 1))
        m, l, pv = _attend(q_ref[0, 0, pl.ds(q0, nq), :], k_ref[0, 0, pl.ds(k0, nk), :],
                           v_ref[0, 0, pl.ds(k0, nk), :], tab_ref[case])
        o = pv / l
        lane = lax.broadcasted_iota(jnp.int32, (nq, LANES), 1)
        o_ref[0, pl.ds(q0, nq), :] = jnp.where(lane < HEAD_DIM, o[:nq], o[nq:]).astype(BF16)
        return carry

    lax.fori_loop(0, n_blk, body, 0, unroll=8)


def _na_attention(qkv, col_bias):
    B, _, S, _ = qkv.shape
    slab = lambda off: pl.BlockSpec((1, 1, S, LANES), lambda p, b: (b, off + p, 0, 0))
    nq, nk = _NQ_ROWS * GRID_W, _NK_ROWS * GRID_W
    return pl.pallas_call(
        _na_kernel,
        grid=(N_PAIRS, B),
        in_specs=[
            slab(3 * N_PAIRS), slab(4 * N_PAIRS), slab(5 * N_PAIRS),
            pl.BlockSpec((2,) + col_bias.shape[1:], lambda p, b: (p, 0, 0, 0)),
        ],
        out_specs=pl.BlockSpec((1, S, LANES), lambda p, b: (b, 0, p)),
        out_shape=jax.ShapeDtypeStruct((B, S, WIDTH), BF16),
        scratch_shapes=[pltpu.VMEM((3, 2 * nq, nk), F32)],
        compiler_params=pltpu.CompilerParams(
            dimension_semantics=("arbitrary", "arbitrary"), vmem_limit_bytes=VMEM_LIMIT),
        name="na_attn",
    )(qkv, qkv, qkv, col_bias)


_GROUP_LANE0 = N_EXPERTS
_ROUTE_GROUP_LANE = EXPERTS_PER_GROUP


def _route(logits):
    lane = lax.broadcasted_iota(jnp.int32, logits.shape, 1)
    big = jnp.int32(LANES)
    is_group = (lane >= _GROUP_LANE0) & (lane < _GROUP_LANE0 + N_GROUPS)
    gl = jnp.where(is_group, logits, NEG_INF)
    gmax = jnp.max(gl, axis=-1, keepdims=True)
    g_idx = jnp.min(jnp.where(is_group & (gl == gmax), lane, big), axis=-1, keepdims=True) - _GROUP_LANE0
    g_weight = 1.0 / jnp.sum(jnp.where(is_group, jnp.exp(gl - gmax), 0.0), axis=-1, keepdims=True)
    in_group = (lane < N_EXPERTS) & ((lane // EXPERTS_PER_GROUP) == g_idx)
    el = jnp.where(in_group, logits, NEG_INF)
    v1 = jnp.max(el, axis=-1, keepdims=True)
    i1 = jnp.min(jnp.where(in_group & (el == v1), lane, big), axis=-1, keepdims=True)
    rest = in_group & (lane != i1)
    el2 = jnp.where(rest, logits, NEG_INF)
    v2 = jnp.max(el2, axis=-1, keepdims=True)
    i2 = jnp.min(jnp.where(rest & (el2 == v2), lane, big), axis=-1, keepdims=True)
    e2 = jnp.exp(v2 - v1)
    w1 = g_weight / (1.0 + e2)
    w2 = g_weight * e2 / (1.0 + e2)
    base = g_idx * EXPERTS_PER_GROUP
    return jnp.where(lane == i1 - base, w1,
                     jnp.where(lane == i2 - base, w2,
                               jnp.where(lane == _ROUTE_GROUP_LANE, g_idx.astype(F32), 0.0)))


def _outproj_kernel(x_ref, yd_ref, yn_ref, gd_ref, gn_ref, wo_ref, gf_ref, wr_hi_ref, wr_lo_ref,
                    br_ref, h_ref, hn_ref, comb_ref):
    yd = _rms(yd_ref[...].astype(F32), gd_ref[...]).astype(BF16)
    yn = _rms(yn_ref[...].astype(F32), gn_ref[...]).astype(BF16)
    h = (x_ref[...] + jnp.dot(yd, wo_ref[:WIDTH, :], preferred_element_type=F32)
         + jnp.dot(yn, wo_ref[WIDTH:, :], preferred_element_type=F32))
    h_ref[...] = h
    hn = _rms(h, gf_ref[...])
    hi = hn.astype(BF16)
    lo = (hn - hi.astype(F32)).astype(BF16)
    hn_ref[...] = hi
    logits = (jnp.dot(hi, wr_hi_ref[...], preferred_element_type=F32)
              + (jnp.dot(lo, wr_hi_ref[...], preferred_element_type=F32)
                 + jnp.dot(hi, wr_lo_ref[...], preferred_element_type=F32))
              + br_ref[...])
    comb_ref[...] = _route(logits)


def _outproj(x2, yd2, yn2, gd, gn, wo, gf, wr_hi, wr_lo, br, tm):
    N, D = x2.shape
    row = lambda w: pl.BlockSpec((tm, w), lambda i: (i, 0))
    full = lambda a, b: pl.BlockSpec((a, b), lambda i: (0, 0))
    return pl.pallas_call(
        _outproj_kernel,
        grid=(N // tm,),
        in_specs=[row(D), row(WIDTH), row(WIDTH), full(1, WIDTH), full(1, WIDTH), full(2 * WIDTH, D),
                  full(1, D), full(D, LANES), full(D, LANES), full(1, LANES)],
        out_specs=[row(D), row(D), row(LANES)],
        out_shape=[jax.ShapeDtypeStruct((N, D), F32), jax.ShapeDtypeStruct((N, D), BF16),
                   jax.ShapeDtypeStruct((N, LANES), F32)],
        compiler_params=pltpu.CompilerParams(
            dimension_semantics=("arbitrary",), vmem_limit_bytes=VMEM_LIMIT),
        name="outproj_route",
    )(x2, yd2, yn2, gd, gn, wo, gf, wr_hi, wr_lo, br)


_MOE_TILE = 1024
_MOE_CHUNK = 128
_MOE_NCHUNK = _MOE_TILE // _MOE_CHUNK + N_GROUPS
_GROUP_WIDTH = EXPERTS_PER_GROUP * D_EXPERT
_ROUTE_PIECE = 8


def _moe_kernel(hn_ref, route_ref, wg_ref, wu_ref, wd_ref, o_ref, ys_ref, xcat_ref):
    T, C = _MOE_TILE, _MOE_CHUNK
    route = route_ref[...]
    lane = lax.broadcasted_iota(jnp.int32, (T, LANES), 1)
    gid = jnp.sum(jnp.where(lane == _ROUTE_GROUP_LANE, route, 0.0), axis=-1, keepdims=True)
    onehot = jnp.where((lane < N_GROUPS) & (lane.astype(F32) == gid), 1.0, 0.0)

    before = (lax.broadcasted_iota(jnp.int32, (T, T), 1)
              < lax.broadcasted_iota(jnp.int32, (T, T), 0)).astype(BF16)
    rank = jnp.dot(before, onehot.astype(BF16), preferred_element_type=F32)
    count = jnp.sum(onehot, axis=0, keepdims=True)
    nchunk = jnp.floor((count + (C - 1)) * (1.0 / C)).astype(jnp.int32)
    off1 = nchunk[0, 0]
    off2 = off1 + nchunk[0, 1]
    off3 = off2 + nchunk[0, 2]
    n_used = off3 + nchunk[0, 3]
    start = jnp.where(lane == 1, off1, jnp.where(lane == 2, off2, jnp.where(lane == 3, off3, 0)))
    pos = jnp.sum(onehot * (rank + (start * C).astype(F32)), axis=-1, keepdims=True)
    pos_i = pos.astype(jnp.int32)
    pos_row = jnp.transpose(jnp.broadcast_to(pos, (T, LANES)))[0:1, :].astype(jnp.int32)

    r_hi = route.astype(BF16).astype(F32)
    r_mid = (route - r_hi).astype(BF16).astype(F32)
    r_lo = (route - r_hi - r_mid).astype(BF16).astype(F32)
    packed = r_hi + pltpu.roll(r_mid, _ROUTE_PIECE, axis=1) + pltpu.roll(r_lo, 2 * _ROUTE_PIECE, axis=1)
    xcat_ref[:, :hn_ref.shape[1]] = hn_ref[...]
    xcat_ref[:, hn_ref.shape[1]:] = packed.astype(BF16)

    ys_ref[...] = jnp.zeros_like(ys_ref)

    def chunk_body(c, carry):
        g = ((c >= off1).astype(jnp.int32) + (c >= off2).astype(jnp.int32)
             + (c >= off3).astype(jnp.int32))
        row0 = pl.multiple_of(c * C, C)
        sel = (pos_row == row0 + lax.broadcasted_iota(jnp.int32, (C, T), 0)).astype(BF16)
        xr = jnp.dot(sel, xcat_ref[...], preferred_element_type=F32)
        xs = xr[:, :hn_ref.shape[1]].astype(BF16)
        r3 = xr[:, hn_ref.shape[1]:]
        r = (r3 + pltpu.roll(r3, LANES - _ROUTE_PIECE, axis=1)
             + pltpu.roll(r3, LANES - 2 * _ROUTE_PIECE, axis=1))
        clane = lax.broadcasted_iota(jnp.int32, (C, LANES), 1)
        parts = []
        for j in range(EXPERTS_PER_GROUP):
            e = g * EXPERTS_PER_GROUP + j
            gate = jnp.dot(xs, wg_ref[e], preferred_element_type=F32)
            up = jnp.dot(xs, wu_ref[e], preferred_element_type=F32)
            wj = jnp.sum(jnp.where(clane == j, r, 0.0), axis=-1, keepdims=True)
            parts.append((gate / (1.0 + jnp.exp(-gate))) * up * wj)
        act = jnp.concatenate(parts, axis=-1).astype(BF16)
        ys_ref[pl.ds(row0, C), :] = jnp.dot(act, wd_ref[g], preferred_element_type=F32).astype(BF16)
        return carry

    lax.fori_loop(0, n_used, chunk_body, 0)

    back = (lax.broadcasted_iota(jnp.int32, (T, _MOE_NCHUNK * C), 1) == pos_i).astype(BF16)
    o_ref[...] = jnp.dot(back, ys_ref[...], preferred_element_type=F32).astype(BF16)


def _moe(hn, route, wg, wu, wd):
    N, D = hn.shape
    T = _MOE_TILE
    row = lambda w: pl.BlockSpec((T, w), lambda i: (i, 0))
    whole = lambda a: pl.BlockSpec(a.shape, lambda i: (0,) * a.ndim)
    return pl.pallas_call(
        _moe_kernel,
        grid=(N // T,),
        in_specs=[row(D), row(LANES), whole(wg), whole(wu), whole(wd)],
        out_specs=row(D),
        out_shape=jax.ShapeDtypeStruct((N, D), BF16),
        scratch_shapes=[pltpu.VMEM((_MOE_NCHUNK * _MOE_CHUNK, D), BF16), pltpu.VMEM((T, D + LANES), BF16)],
        compiler_params=pltpu.CompilerParams(
            dimension_semantics=("arbitrary",), vmem_limit_bytes=VMEM_LIMIT),
        name="moe_grouped",
    )(hn, route, wg, wu, wd)


def _final_kernel(h_ref, m_ref, g_ref, y_ref):
    y_ref[...] = _rms(h_ref[...] + m_ref[...].astype(F32), g_ref[...])


def _final(h, moe_out, gfin, tm):
    N, D = h.shape
    row = pl.BlockSpec((tm, D), lambda i: (i, 0))
    return pl.pallas_call(
        _final_kernel,
        grid=(N // tm,),
        in_specs=[row, row, pl.BlockSpec((1, D), lambda i: (0, 0))],
        out_specs=row,
        out_shape=jax.ShapeDtypeStruct((N, D), F32),
        compiler_params=pltpu.CompilerParams(
            dimension_semantics=("arbitrary",), vmem_limit_bytes=VMEM_LIMIT),
        name="final_norm",
    )(h, moe_out, gfin)


def kernel(x, norm_mix_g, w_in, rpb, g_out_dil, g_out_na, w_out, norm_ffn_g, w_group, b_group,
           w_router, b_router, w_gate, w_up, w_down, norm_final_g):
    B, S, D = x.shape
    N = B * S
    depth = w_in.shape[0]
    assert depth == 1 and D == D_MODEL and S % (16 * _DQ) == 0

    t12, t3 = _dilated_tables()

    layer = 0
    col_scale = np.ones((6, WIDTH), np.float32)
    col_scale[0] = col_scale[3] = HEAD_DIM ** -0.5
    w_in_b = (w_in[layer] * col_scale.reshape(1, -1)).astype(BF16)

    qkv = _inproj(x, norm_mix_g[layer].reshape(1, D), w_in_b, tm=512)

    y_dil = _dilated_attention(qkv, jnp.asarray(t12), jnp.asarray(t3))

    y_na = _na_attention(qkv, _na_column_bias(rpb[layer]))

    n_route = N_EXPERTS + N_GROUPS
    w_r = jnp.concatenate([w_router[layer], w_group[layer], jnp.zeros((D, LANES - n_route), F32)], axis=1)
    w_r_hi = w_r.astype(BF16)
    w_r_lo = (w_r - w_r_hi.astype(F32)).astype(BF16)
    b_r = jnp.concatenate([b_router[layer], b_group[layer], jnp.zeros((LANES - n_route,), F32)]).reshape(1, LANES)

    h, hn, route = _outproj(
        x.reshape(N, D), y_dil.reshape(N, WIDTH), y_na.reshape(N, WIDTH),
        g_out_dil[layer].reshape(1, WIDTH), g_out_na[layer].reshape(1, WIDTH),
        w_out[layer].astype(BF16), norm_ffn_g[layer].reshape(1, D), w_r_hi, w_r_lo, b_r, tm=512)

    wd = w_down[layer].astype(BF16).reshape(N_GROUPS, _GROUP_WIDTH, D)
    moe_out = _moe(hn, route, w_gate[layer].astype(BF16), w_up[layer].astype(BF16), wd)
    y = _final(h, moe_out, norm_final_g.reshape(1, D), tm=1024)
    return y.reshape(B, S, D)
```

```python
import functools

import numpy as np
import jax
import jax.numpy as jnp
from jax import lax
from jax.experimental import pallas as pl
from jax.experimental.pallas import tpu as pltpu

D_MODEL = 1024
HEAD_DIM = 64
N_HEADS = 8
N_PAIRS = N_HEADS // 2
WIDTH = N_HEADS * HEAD_DIM
N_SLABS = 6 * N_PAIRS
DIL_PATTERNS = ((128, 1), (512, 4), (2048, 16))
DIL_RADIUS = 64
GRID_W = 64
NA_WIN_ROWS = 8
NA_WIN_COLS = 16
N_GROUPS = 4
EXPERTS_PER_GROUP = 4
N_EXPERTS = 16
D_EXPERT = 256
RMS_EPS = 1e-6
NEG_INF = -1e30

LANES = 128
VMEM_LIMIT = 48 * 1024 * 1024

F32 = jnp.float32
BF16 = jnp.bfloat16


def _rms(x, gain):
    return x * lax.rsqrt(jnp.mean(x * x, axis=-1, keepdims=True) + RMS_EPS) * gain


def _inproj_kernel(x_ref, g_ref, w_ref, o_ref):
    xn = _rms(x_ref[0], g_ref[...]).astype(BF16)
    chunk = 4 * LANES
    for c in range(N_SLABS * LANES // chunk):
        acc = jnp.dot(xn, w_ref[:, c * chunk:(c + 1) * chunk], preferred_element_type=F32)
        for j in range(chunk // LANES):
            o_ref[0, c * (chunk // LANES) + j] = acc[:, j * LANES:(j + 1) * LANES].astype(BF16)


def _inproj(x, gain, w_bf16, tm):
    B, S, D = x.shape
    return pl.pallas_call(
        _inproj_kernel,
        grid=(B, S // tm),
        in_specs=[
            pl.BlockSpec((1, tm, D), lambda b, i: (b, i, 0)),
            pl.BlockSpec((1, D), lambda b, i: (0, 0)),
            pl.BlockSpec((D, N_SLABS * LANES), lambda b, i: (0, 0)),
        ],
        out_specs=pl.BlockSpec((1, N_SLABS, tm, LANES), lambda b, i: (b, 0, i, 0)),
        out_shape=jax.ShapeDtypeStruct((B, N_SLABS, S, LANES), BF16),
        compiler_params=pltpu.CompilerParams(
            dimension_semantics=("arbitrary", "arbitrary"), vmem_limit_bytes=VMEM_LIMIT),
        name="inproj",
    )(x, gain, w_bf16)


def _attend(qb, kw, vw, bias):
    lane = lax.broadcasted_iota(jnp.int32, qb.shape, 1)
    zero = jnp.zeros_like(qb)
    qq = jnp.concatenate([jnp.where(lane < HEAD_DIM, qb, zero),
                          jnp.where(lane >= HEAD_DIM, qb, zero)], axis=0)
    s = lax.dot_general(qq, kw, (((1,), (1,)), ((), ())), preferred_element_type=F32) + bias
    m = jnp.max(s, axis=-1, keepdims=True)
    p = jnp.exp(s - m)
    l = jnp.sum(p, axis=-1, keepdims=True)
    pv = jnp.dot(p.astype(BF16), vw, preferred_element_type=F32)
    return m, l, pv


def _merge_heads(top, bottom, q):
    lane = lax.broadcasted_iota(jnp.int32, (q, LANES), 1)
    return jnp.where(lane < HEAD_DIM, jnp.broadcast_to(top, (q, LANES)),
                     jnp.broadcast_to(bottom, (q, LANES)))


_DQ = 128
_DW = 256
_UNROLL = 16


def _dilated_tables():
    slopes = 2.0 ** (-(np.arange(N_HEADS) + 1.0))
    q = np.arange(_DQ)[:, None]
    t12 = np.zeros((N_PAIRS, 2, 3, 2 * _DQ, _DW), np.float32)
    k = np.arange(_DW)[None, :]
    for pat, dil in enumerate((1, 4)):
        for case, off in enumerate((0, _DW // 4, _DW // 2)):
            delta = np.abs(k - (q + off))
            for h in range(N_HEADS):
                tab = np.where(delta <= DIL_RADIUS, -slopes[h] * dil * delta, NEG_INF)
                t12[h // 2, pat, case, (h % 2) * _DQ:(h % 2 + 1) * _DQ] = tab
    t3 = np.zeros((N_PAIRS, 2 * _DQ, _DQ), np.float32)
    delta = np.abs(np.arange(_DQ)[None, :] - q)
    for h in range(N_HEADS):
        t3[h // 2, (h % 2) * _DQ:(h % 2 + 1) * _DQ] = np.where(
            delta <= DIL_RADIUS, -slopes[h] * 16 * delta, NEG_INF)
    return t12, t3


def _dilated_kernel(q_ref, k_ref, v_ref, t12_ref, t3_ref, o_ref,
                    tmp, tmp4, q4, k4, v4, q16, k16, v16,
                    m1, l1, a1, m2, l2, a2, m3, l3, a3, onat):
    S = tmp.shape[0]
    L4, L16 = S // 4, S // 16

    for src, d4, d16 in ((q_ref, q4, q16), (k_ref, k4, k16), (v_ref, v4, v16)):
        tmp[...] = src[0, 0].astype(F32)
        for r in range(4):
            sub = tmp[pl.ds(r, L4, stride=4), :]
            tmp4[r * L4:(r + 1) * L4, :] = sub
            d4[r * L4:(r + 1) * L4, :] = sub.astype(BF16)
        for r16 in range(16):
            r4, c4 = r16 % 4, r16 // 4
            d16[r16 * L16:(r16 + 1) * L16, :] = tmp4[pl.ds(r4 * L4 + c4, L16, stride=4), :].astype(BF16)

    def block(qb, kw, vw, bias, m_ref, l_ref, a_ref, row):
        m, l, pv = _attend(qb, kw, vw, bias)
        m_ref[pl.ds(row, _DQ), :] = _merge_heads(m[:_DQ], m[_DQ:], _DQ)
        l_ref[pl.ds(row, _DQ), :] = _merge_heads(l[:_DQ], l[_DQ:], _DQ)
        a_ref[pl.ds(row, _DQ), :] = _merge_heads(pv[:_DQ], pv[_DQ:], _DQ)

    def case_of(blk, n_blk):
        return jnp.where(blk == 0, 0, jnp.where(blk == n_blk - 1, 2, 1))

    n1 = S // _DQ

    def p1_body(blk, carry):
        t0 = pl.multiple_of(blk * _DQ, _DQ)
        ws = pl.multiple_of(jnp.clip(t0 - DIL_RADIUS, 0, S - _DW), DIL_RADIUS)
        block(q_ref[0, 0, pl.ds(t0, _DQ), :], k_ref[0, 0, pl.ds(ws, _DW), :],
              v_ref[0, 0, pl.ds(ws, _DW), :], t12_ref[0, 0, case_of(blk, n1)], m1, l1, a1, t0)
        return carry

    lax.fori_loop(0, n1, p1_body, 0, unroll=_UNROLL)

    n2 = L4 // _DQ

    def p2_body(j, carry):
        r = j // n2
        blk = j % n2
        l0 = blk * _DQ
        ws = jnp.clip(l0 - DIL_RADIUS, 0, L4 - _DW)
        row = pl.multiple_of(r * L4 + l0, _DQ)
        krow = pl.multiple_of(r * L4 + ws, DIL_RADIUS)
        block(q4[pl.ds(row, _DQ), :], k4[pl.ds(krow, _DW), :], v4[pl.ds(krow, _DW), :],
              t12_ref[0, 1, case_of(blk, n2)], m2, l2, a2, row)
        return carry

    lax.fori_loop(0, 4 * n2, p2_body, 0, unroll=_UNROLL)

    def p3_body(r, carry):
        row = pl.multiple_of(r * L16, L16)
        block(q16[pl.ds(row, L16), :], k16[pl.ds(row, L16), :], v16[pl.ds(row, L16), :],
              t3_ref[0], m3, l3, a3, row)
        return carry

    lax.fori_loop(0, 16, p3_body, 0, unroll=_UNROLL)

    for r16 in range(16):
        r4, c4 = r16 % 4, r16 // 4
        via4 = pl.ds(r4 * L4 + c4, L16, stride=4)
        via16 = pl.ds(r16 * L16, L16)
        mb, mc = m2[via4, :], m3[via16, :]
        mx = jnp.maximum(mb, mc)
        wb, wc = jnp.exp(mb - mx), jnp.exp(mc - mx)
        l2[via4, :] = wb * l2[via4, :] + wc * l3[via16, :]
        a2[via4, :] = wb * a2[via4, :] + wc * a3[via16, :]
        m2[via4, :] = mx
    for r4 in range(4):
        for part in range(L4 // _DQ):
            nat = pl.ds(r4 + 4 * _DQ * part, _DQ, stride=4)
            via4 = pl.ds(r4 * L4 + _DQ * part, _DQ)
            ma, mb = m1[nat, :], m2[via4, :]
            mx = jnp.maximum(ma, mb)
            wa, wb = jnp.exp(ma - mx), jnp.exp(mb - mx)
            den = wa * l1[nat, :] + wb * l2[via4, :]
            num = wa * a1[nat, :] + wb * a2[via4, :]
            onat[nat, :] = num / den
    o_ref[0] = onat[...].astype(BF16)


def _dilated_attention(qkv, t12, t3):
    B, _, S, _ = qkv.shape
    f32_buf = pltpu.VMEM((S, LANES), F32)
    bf16_buf = pltpu.VMEM((S, LANES), BF16)
    slab = lambda off: pl.BlockSpec((1, 1, S, LANES), lambda p, b: (b, off + p, 0, 0))
    return pl.pallas_call(
        _dilated_kernel,
        grid=(N_PAIRS, B),
        in_specs=[
            slab(0), slab(N_PAIRS), slab(2 * N_PAIRS),
            pl.BlockSpec((1, 2, 3, 2 * _DQ, _DW), lambda p, b: (p, 0, 0, 0, 0)),
            pl.BlockSpec((1, 2 * _DQ, _DQ), lambda p, b: (p, 0, 0)),
        ],
        out_specs=pl.BlockSpec((1, S, LANES), lambda p, b: (b, 0, p)),
        out_shape=jax.ShapeDtypeStruct((B, S, WIDTH), BF16),
        scratch_shapes=[f32_buf] * 2 + [bf16_buf] * 6 + [f32_buf] * 10,
        compiler_params=pltpu.CompilerParams(
            dimension_semantics=("arbitrary", "arbitrary"), vmem_limit_bytes=VMEM_LIMIT),
        name="dilated_attn",
    )(qkv, qkv, qkv, t12, t3)


_NQ_ROWS = 4
_NK_ROWS = 12


def _na_row_select(rows):
    n_blk = rows // _NQ_ROWS
    sel = np.full((n_blk, _NQ_ROWS, _NK_ROWS), -1, np.int64)
    for i in range(n_blk):
        kr0 = min(max(_NQ_ROWS * i - NA_WIN_ROWS // 2, 0), rows - _NK_ROWS)
        for a in range(_NQ_ROWS):
            qr = _NQ_ROWS * i + a
            rs = min(max(qr - NA_WIN_ROWS // 2, 0), rows - NA_WIN_ROWS)
            for b in range(_NK_ROWS):
                kr = kr0 + b
                if rs <= kr < rs + NA_WIN_ROWS:
                    sel[i, a, b] = kr - qr + NA_WIN_ROWS - 1
    for i in range(2, n_blk - 1):
        assert np.array_equal(sel[1], sel[i])
    return sel[[0, 1, n_blk - 1]]


def _na_column_bias(rpb):
    n_dr, n_dc = 2 * NA_WIN_ROWS - 1, 2 * NA_WIN_COLS - 1
    qc = np.arange(GRID_W)[:, None]
    kc = np.arange(GRID_W)[None, :]
    cs = np.clip(qc - NA_WIN_COLS // 2, 0, GRID_W - NA_WIN_COLS)
    col_ok = (kc >= cs) & (kc < cs + NA_WIN_COLS)
    dc = np.clip(kc - qc + NA_WIN_COLS - 1, 0, n_dc - 1)
    onehot = (dc.reshape(1, -1) == np.arange(n_dc)[:, None]).astype(np.float32)
    t = jnp.dot(rpb.astype(F32).reshape(N_HEADS * n_dr, n_dc), onehot, precision=lax.Precision.HIGHEST)
    t = jnp.where(col_ok[None, None], t.reshape(N_HEADS, n_dr, GRID_W, GRID_W), NEG_INF)
    neg = jnp.full((N_HEADS, 1, GRID_W, GRID_W), NEG_INF, F32)
    ext = jnp.concatenate([neg, t, neg], axis=1)
    return jnp.concatenate([ext[:, :n_dr + 1], ext[:, 1:]], axis=-1)


def _na_kernel(q_ref, k_ref, v_ref, cb_ref, o_ref, tab_ref):
    S = q_ref.shape[2]
    rows = S // GRID_W
    n_blk = rows // _NQ_ROWS
    nq = _NQ_ROWS * GRID_W
    nk = _NK_ROWS * GRID_W

    @pl.when(pl.program_id(1) == 0)
    def _():
        sel = _na_row_select(rows)
        lane = lax.broadcasted_iota(jnp.int32, (GRID_W, LANES), 1)
        masked = jnp.full((GRID_W, LANES), NEG_INF, F32)
        for c in range(sel.shape[0]):
            for a in range(_NQ_ROWS):
                for j in range(_NK_ROWS // 2):
                    s0, s1 = int(sel[c, a, 2 * j]), int(sel[c, a, 2 * j + 1])
                    for h in range(2):
                        if s0 < 0 and s1 < 0:
                            tile = masked
                        else:
                            d = s0 if s0 >= 0 else s1 - 1
                            assert s1 < 0 or s1 == d + 1
                            tile = cb_ref[h, d + 1]
                            if s0 < 0:
                                tile = jnp.where(lane >= GRID_W, tile, NEG_INF)
                            elif s1 < 0:
                                tile = jnp.where(lane < GRID_W, tile, NEG_INF)
                        r0 = h * nq + a * GRID_W
                        tab_ref[c, r0:r0 + GRID_W, j * LANES:(j + 1) * LANES] = tile

    def body(i, carry):
        q0 = pl.multiple_of(i * nq, nq)
        kr0 = jnp.clip(_NQ_ROWS * i - NA_WIN_ROWS // 2, 0, rows - _NK_ROWS)
        k0 = pl.multiple_of(kr0 * GRID_W, GRID_W)
        case = jnp.where(i == 0, 0, jnp.where(i == n_blk - 1, 2, 1))
        m, l, pv = _attend(q_ref[0, 0, pl.ds(q0, nq), :], k_ref[0, 0, pl.ds(k0, nk), :],
                           v_ref[0, 0, pl.ds(k0, nk), :], tab_ref[case])
        o = pv / l
        lane = lax.broadcasted_iota(jnp.int32, (nq, LANES), 1)
        o_ref[0, pl.ds(q0, nq), :] = jnp.where(lane < HEAD_DIM, o[:nq], o[nq:]).astype(BF16)
        return carry

    lax.fori_loop(0, n_blk, body, 0, unroll=8)


def _na_attention(qkv, col_bias):
    B, _, S, _ = qkv.shape
    slab = lambda off: pl.BlockSpec((1, 1, S, LANES), lambda p, b: (b, off + p, 0, 0))
    nq, nk = _NQ_ROWS * GRID_W, _NK_ROWS * GRID_W
    return pl.pallas_call(
        _na_kernel,
        grid=(N_PAIRS, B),
        in_specs=[
            slab(3 * N_PAIRS), slab(4 * N_PAIRS), slab(5 * N_PAIRS),
            pl.BlockSpec((2,) + col_bias.shape[1:], lambda p, b: (p, 0, 0, 0)),
        ],
        out_specs=pl.BlockSpec((1, S, LANES), lambda p, b: (b, 0, p)),
        out_shape=jax.ShapeDtypeStruct((B, S, WIDTH), BF16),
        scratch_shapes=[pltpu.VMEM((3, 2 * nq, nk), F32)],
        compiler_params=pltpu.CompilerParams(
            dimension_semantics=("arbitrary", "arbitrary"), vmem_limit_bytes=VMEM_LIMIT),
        name="na_attn",
    )(qkv, qkv, qkv, col_bias)


_GROUP_LANE0 = N_EXPERTS
_ROUTE_GROUP_LANE = EXPERTS_PER_GROUP


def _route(logits):
    lane = lax.broadcasted_iota(jnp.int32, logits.shape, 1)
    big = jnp.int32(LANES)
    is_group = (lane >= _GROUP_LANE0) & (lane < _GROUP_LANE0 + N_GROUPS)
    gl = jnp.where(is_group, logits, NEG_INF)
    gmax = jnp.max(gl, axis=-1, keepdims=True)
    g_idx = jnp.min(jnp.where(is_group & (gl == gmax), lane, big), axis=-1, keepdims=True) - _GROUP_LANE0
    g_weight = 1.0 / jnp.sum(jnp.where(is_group, jnp.exp(gl - gmax), 0.0), axis=-1, keepdims=True)
    in_group = (lane < N_EXPERTS) & ((lane // EXPERTS_PER_GROUP) == g_idx)
    el = jnp.where(in_group, logits, NEG_INF)
    v1 = jnp.max(el, axis=-1, keepdims=True)
    i1 = jnp.min(jnp.where(in_group & (el == v1), lane, big), axis=-1, keepdims=True)
    rest = in_group & (lane != i1)
    el2 = jnp.where(rest, logits, NEG_INF)
    v2 = jnp.max(el2, axis=-1, keepdims=True)
    i2 = jnp.min(jnp.where(rest & (el2 == v2), lane, big), axis=-1, keepdims=True)
    e2 = jnp.exp(v2 - v1)
    w1 = g_weight / (1.0 + e2)
    w2 = g_weight * e2 / (1.0 + e2)
    base = g_idx * EXPERTS_PER_GROUP
    return jnp.where(lane == i1 - base, w1,
                     jnp.where(lane == i2 - base, w2,
                               jnp.where(lane == _ROUTE_GROUP_LANE, g_idx.astype(F32), 0.0)))


def _outproj_kernel(x_ref, yd_ref, yn_ref, gd_ref, gn_ref, wo_ref, gf_ref, wr_ref,
                    br_ref, h_ref, hn_ref, comb_ref):
    yd = _rms(yd_ref[...].astype(F32), gd_ref[...]).astype(BF16)
    yn = _rms(yn_ref[...].astype(F32), gn_ref[...]).astype(BF16)
    h = x_ref[...] + jnp.dot(jnp.concatenate([yd, yn], axis=-1), wo_ref[...], preferred_element_type=F32)
    h_ref[...] = h
    hn = _rms(h, gf_ref[...]).astype(BF16)
    hn_ref[...] = hn
    logits = jnp.dot(hn, wr_ref[...], preferred_element_type=F32) + br_ref[...]
    comb_ref[...] = _route(logits)


def _outproj(x2, yd2, yn2, gd, gn, wo, gf, wr, br, tm):
    N, D = x2.shape
    row = lambda w: pl.BlockSpec((tm, w), lambda i: (i, 0))
    full = lambda a, b: pl.BlockSpec((a, b), lambda i: (0, 0))
    return pl.pallas_call(
        _outproj_kernel,
        grid=(N // tm,),
        in_specs=[row(D), row(WIDTH), row(WIDTH), full(1, WIDTH), full(1, WIDTH), full(2 * WIDTH, D),
                  full(1, D), full(D, LANES), full(1, LANES)],
        out_specs=[row(D), row(D), row(LANES)],
        out_shape=[jax.ShapeDtypeStruct((N, D), F32), jax.ShapeDtypeStruct((N, D), BF16),
                   jax.ShapeDtypeStruct((N, LANES), F32)],
        compiler_params=pltpu.CompilerParams(
            dimension_semantics=("arbitrary",), vmem_limit_bytes=VMEM_LIMIT),
        name="outproj_route",
    )(x2, yd2, yn2, gd, gn, wo, gf, wr, br)


_MOE_TILE = 1024
_MOE_CHUNK = 128
_MOE_NCHUNK = _MOE_TILE // _MOE_CHUNK + N_GROUPS
_GROUP_WIDTH = EXPERTS_PER_GROUP * D_EXPERT
_ROUTE_PIECE = 8
_MOE_VMEM_LIMIT = 58 * 1024 * 1024


def _moe_kernel(hn_ref, route_ref, h_ref, wg_ref, wu_ref, wd_ref, gfin_ref, y_ref, ys_ref, xcat_ref):
    T, C = _MOE_TILE, _MOE_CHUNK
    route = route_ref[...]
    lane = lax.broadcasted_iota(jnp.int32, (T, LANES), 1)
    gid = jnp.sum(jnp.where(lane == _ROUTE_GROUP_LANE, route, 0.0), axis=-1, keepdims=True)
    onehot = jnp.where((lane < N_GROUPS) & (lane.astype(F32) == gid), 1.0, 0.0)

    before = (lax.broadcasted_iota(jnp.int32, (LANES, LANES), 1)
              < lax.broadcasted_iota(jnp.int32, (LANES, LANES), 0)).astype(BF16)
    count = jnp.zeros((1, LANES), F32)
    ranks = []
    for blk in range(T // LANES):
        oh = onehot[blk * LANES:(blk + 1) * LANES]
        ranks.append(jnp.dot(before, oh.astype(BF16), preferred_element_type=F32) + count)
        count = count + jnp.sum(oh, axis=0, keepdims=True)
    rank = jnp.concatenate(ranks, axis=0)
    nchunk = jnp.floor((count + (C - 1)) * (1.0 / C)).astype(jnp.int32)
    off1 = nchunk[0, 0]
    off2 = off1 + nchunk[0, 1]
    off3 = off2 + nchunk[0, 2]
    n_used = off3 + nchunk[0, 3]
    start = jnp.where(lane == 1, off1, jnp.where(lane == 2, off2, jnp.where(lane == 3, off3, 0)))
    pos = jnp.sum(onehot * (rank + (start * C).astype(F32)), axis=-1, keepdims=True)
    pos_i = pos.astype(jnp.int32)
    pos_row = jnp.transpose(jnp.broadcast_to(pos, (T, LANES)))[0:1, :].astype(jnp.int32)

    r_hi = route.astype(BF16).astype(F32)
    r_mid = (route - r_hi).astype(BF16).astype(F32)
    r_lo = (route - r_hi - r_mid).astype(BF16).astype(F32)
    packed = r_hi + pltpu.roll(r_mid, _ROUTE_PIECE, axis=1) + pltpu.roll(r_lo, 2 * _ROUTE_PIECE, axis=1)
    xcat_ref[:, :hn_ref.shape[1]] = hn_ref[...]
    xcat_ref[:, hn_ref.shape[1]:] = packed.astype(BF16)

    ys_ref[...] = jnp.zeros_like(ys_ref)

    def chunk_body(c, carry):
        g = ((c >= off1).astype(jnp.int32) + (c >= off2).astype(jnp.int32)
             + (c >= off3).astype(jnp.int32))
        row0 = pl.multiple_of(c * C, C)
        sel = (pos_row == row0 + lax.broadcasted_iota(jnp.int32, (C, T), 0)).astype(BF16)
        xr = jnp.dot(sel, xcat_ref[...], preferred_element_type=F32)
        xs = xr[:, :hn_ref.shape[1]].astype(BF16)
        r3 = xr[:, hn_ref.shape[1]:]
        r = (r3 + pltpu.roll(r3, LANES - _ROUTE_PIECE, axis=1)
             + pltpu.roll(r3, LANES - 2 * _ROUTE_PIECE, axis=1))
        clane = lax.broadcasted_iota(jnp.int32, (C, LANES), 1)
        parts = []
        for j in range(EXPERTS_PER_GROUP):
            e = g * EXPERTS_PER_GROUP + j
            gate = jnp.dot(xs, wg_ref[e], preferred_element_type=F32)
            up = jnp.dot(xs, wu_ref[e], preferred_element_type=F32)
            wj = jnp.sum(jnp.where(clane == j, r, 0.0), axis=-1, keepdims=True)
            parts.append((gate / (1.0 + jnp.exp(-gate))) * up * wj)
        act = jnp.concatenate(parts, axis=-1).astype(BF16)
        ys_ref[pl.ds(row0, C), :] = jnp.dot(act, wd_ref[g], preferred_element_type=F32).astype(BF16)
        return carry

    lax.fori_loop(0, n_used, chunk_body, 0)

    n_main = _MOE_TILE // C + N_GROUPS // 2
    n_tail = _MOE_NCHUNK - n_main
    back = (lax.broadcasted_iota(jnp.int32, (T, n_main * C), 1) == pos_i).astype(BF16)
    y_ref[...] = h_ref[...] + jnp.dot(back, ys_ref[:n_main * C, :], preferred_element_type=F32)

    @pl.when(n_used > n_main)
    def _():
        tail = (lax.broadcasted_iota(jnp.int32, (T, n_tail * C), 1) == pos_i - n_main * C).astype(BF16)
        y_ref[...] += jnp.dot(tail, ys_ref[n_main * C:, :], preferred_element_type=F32)

    y_ref[...] = _rms(y_ref[...], gfin_ref[...])


def _moe(hn, route, h, wg, wu, wd, gfin):
    N, D = hn.shape
    T = _MOE_TILE
    row = lambda w, **kw: pl.BlockSpec((T, w), lambda i: (i, 0), **kw)
    whole = lambda a: pl.BlockSpec(a.shape, lambda i: (0,) * a.ndim)
    return pl.pallas_call(
        _moe_kernel,
        grid=(N // T,),
        in_specs=[row(D), row(LANES), row(D, pipeline_mode=pl.Buffered(1)),
                  whole(wg), whole(wu), whole(wd), whole(gfin)],
        out_specs=row(D),
        out_shape=jax.ShapeDtypeStruct((N, D), F32),
        scratch_shapes=[pltpu.VMEM((_MOE_NCHUNK * _MOE_CHUNK, D), BF16), pltpu.VMEM((T, D + LANES), BF16)],
        compiler_params=pltpu.CompilerParams(
            dimension_semantics=("arbitrary",), vmem_limit_bytes=_MOE_VMEM_LIMIT),
        name="moe_grouped",
    )(hn, route, h, wg, wu, wd, gfin)


def kernel(x, norm_mix_g, w_in, rpb, g_out_dil, g_out_na, w_out, norm_ffn_g, w_group, b_group,
           w_router, b_router, w_gate, w_up, w_down, norm_final_g):
    B, S, D = x.shape
    N = B * S
    depth = w_in.shape[0]
    assert depth == 1 and D == D_MODEL and S % (16 * _DQ) == 0

    t12, t3 = _dilated_tables()

    layer = 0
    col_scale = np.ones((6, WIDTH), np.float32)
    col_scale[0] = col_scale[3] = HEAD_DIM ** -0.5
    w_in_b = (w_in[layer] * col_scale.reshape(1, -1)).astype(BF16)

    qkv = _inproj(x, norm_mix_g[layer].reshape(1, D), w_in_b, tm=512)

    y_dil = _dilated_attention(qkv, jnp.asarray(t12), jnp.asarray(t3))

    y_na = _na_attention(qkv, _na_column_bias(rpb[layer]))

    n_route = N_EXPERTS + N_GROUPS
    w_r = jnp.concatenate([w_router[layer], w_group[layer], jnp.zeros((D, LANES - n_route), F32)], axis=1)
    b_r = jnp.concatenate([b_router[layer], b_group[layer], jnp.zeros((LANES - n_route,), F32)]).reshape(1, LANES)

    h, hn, route = _outproj(
        x.reshape(N, D), y_dil.reshape(N, WIDTH), y_na.reshape(N, WIDTH),
        g_out_dil[layer].reshape(1, WIDTH), g_out_na[layer].reshape(1, WIDTH),
        w_out[layer].astype(BF16), norm_ffn_g[layer].reshape(1, D), w_r.astype(BF16), b_r, tm=512)

    wd = w_down[layer].astype(BF16).reshape(N_GROUPS, _GROUP_WIDTH, D)
    y = _moe(hn, route, h, w_gate[layer].astype(BF16), w_up[layer].astype(BF16), wd,
             norm_final_g.reshape(1, D))
    return y.reshape(B, S, D)
```

```python
import functools

import numpy as np
import jax
import jax.numpy as jnp
from jax import lax
from jax.experimental import pallas as pl
from jax.experimental.pallas import tpu as pltpu

D_MODEL = 1024
HEAD_DIM = 64
N_HEADS = 8
N_PAIRS = N_HEADS // 2
WIDTH = N_HEADS * HEAD_DIM
N_SLABS = 6 * N_PAIRS
DIL_PATTERNS = ((128, 1), (512, 4), (2048, 16))
DIL_RADIUS = 64
GRID_W = 64
NA_WIN_ROWS = 8
NA_WIN_COLS = 16
N_GROUPS = 4
EXPERTS_PER_GROUP = 4
N_EXPERTS = 16
D_EXPERT = 256
RMS_EPS = 1e-6
NEG_INF = -1e30
LOG2E = 1.4426950408889634

LANES = 128
VMEM_LIMIT = 48 * 1024 * 1024

F32 = jnp.float32
BF16 = jnp.bfloat16


def _rms(x, gain):
    return x * lax.rsqrt(jnp.mean(x * x, axis=-1, keepdims=True) + RMS_EPS) * gain


def _inproj_kernel(x_ref, g_ref, w_ref, o_ref):
    xn = _rms(x_ref[0], g_ref[...]).astype(BF16)
    chunk = 4 * LANES
    for c in range(N_SLABS * LANES // chunk):
        acc = jnp.dot(xn, w_ref[:, c * chunk:(c + 1) * chunk], preferred_element_type=F32)
        for j in range(chunk // LANES):
            o_ref[0, c * (chunk // LANES) + j] = acc[:, j * LANES:(j + 1) * LANES].astype(BF16)


def _inproj(x, gain, w_bf16, tm):
    B, S, D = x.shape
    return pl.pallas_call(
        _inproj_kernel,
        grid=(B, S // tm),
        in_specs=[
            pl.BlockSpec((1, tm, D), lambda b, i: (b, i, 0)),
            pl.BlockSpec((1, D), lambda b, i: (0, 0)),
            pl.BlockSpec((D, N_SLABS * LANES), lambda b, i: (0, 0)),
        ],
        out_specs=pl.BlockSpec((1, N_SLABS, tm, LANES), lambda b, i: (b, 0, i, 0)),
        out_shape=jax.ShapeDtypeStruct((B, N_SLABS, S, LANES), BF16),
        compiler_params=pltpu.CompilerParams(
            dimension_semantics=("arbitrary", "arbitrary"), vmem_limit_bytes=VMEM_LIMIT),
        name="inproj",
    )(x, gain, w_bf16)


def _attend(qb, kw, vw, bias):
    lane = lax.broadcasted_iota(jnp.int32, qb.shape, 1)
    zero = jnp.zeros_like(qb)
    qq = jnp.concatenate([jnp.where(lane < HEAD_DIM, qb, zero),
                          jnp.where(lane >= HEAD_DIM, qb, zero)], axis=0)
    s = lax.dot_general(qq, kw, (((1,), (1,)), ((), ())), preferred_element_type=F32) + bias
    m = jnp.max(s, axis=-1, keepdims=True)
    p = jnp.exp2(s - m)
    l = jnp.sum(p, axis=-1, keepdims=True)
    pv = jnp.dot(p.astype(BF16), vw, preferred_element_type=F32)
    return m, l, pv


def _merge_heads(top, bottom, q):
    lane = lax.broadcasted_iota(jnp.int32, (q, LANES), 1)
    return jnp.where(lane < HEAD_DIM, jnp.broadcast_to(top, (q, LANES)),
                     jnp.broadcast_to(bottom, (q, LANES)))


_DQ = 128
_DW = 256
_UNROLL = 16


def _dilated_tables():
    slopes = 2.0 ** (-(np.arange(N_HEADS) + 1.0))
    q = np.arange(_DQ)[:, None]
    t12 = np.zeros((N_PAIRS, 2, 3, 2 * _DQ, _DW), np.float32)
    k = np.arange(_DW)[None, :]
    for pat, dil in enumerate((1, 4)):
        for case, off in enumerate((0, _DW // 4, _DW // 2)):
            delta = np.abs(k - (q + off))
            for h in range(N_HEADS):
                tab = np.where(delta <= DIL_RADIUS, -LOG2E * slopes[h] * dil * delta, NEG_INF)
                t12[h // 2, pat, case, (h % 2) * _DQ:(h % 2 + 1) * _DQ] = tab
    t3 = np.zeros((N_PAIRS, 2 * _DQ, _DQ), np.float32)
    delta = np.abs(np.arange(_DQ)[None, :] - q)
    for h in range(N_HEADS):
        t3[h // 2, (h % 2) * _DQ:(h % 2 + 1) * _DQ] = np.where(
            delta <= DIL_RADIUS, -LOG2E * slopes[h] * 16 * delta, NEG_INF)
    return t12, t3


def _dilated_kernel(q_ref, k_ref, v_ref, t12_ref, t3_ref, o_ref,
                    tmp, tmp4, q4, k4, v4, q16, k16, v16,
                    m1, l1, a1, m2, l2, a2, m3, l3, a3, onat):
    S = tmp.shape[0]
    L4, L16 = S // 4, S // 16

    for src, d4, d16 in ((q_ref, q4, q16), (k_ref, k4, k16), (v_ref, v4, v16)):
        tmp[...] = src[0, 0].astype(F32)
        for r in range(4):
            sub = tmp[pl.ds(r, L4, stride=4), :]
            tmp4[r * L4:(r + 1) * L4, :] = sub
            d4[r * L4:(r + 1) * L4, :] = sub.astype(BF16)
        for r16 in range(16):
            r4, c4 = r16 % 4, r16 // 4
            d16[r16 * L16:(r16 + 1) * L16, :] = tmp4[pl.ds(r4 * L4 + c4, L16, stride=4), :].astype(BF16)

    def block(qb, kw, vw, bias, m_ref, l_ref, a_ref, row):
        m, l, pv = _attend(qb, kw, vw, bias)
        m_ref[pl.ds(row, _DQ), :] = _merge_heads(m[:_DQ], m[_DQ:], _DQ)
        l_ref[pl.ds(row, _DQ), :] = _merge_heads(l[:_DQ], l[_DQ:], _DQ)
        a_ref[pl.ds(row, _DQ), :] = _merge_heads(pv[:_DQ], pv[_DQ:], _DQ)

    def case_of(blk, n_blk):
        return jnp.where(blk == 0, 0, jnp.where(blk == n_blk - 1, 2, 1))

    n1 = S // _DQ

    def p1_body(blk, carry):
        t0 = pl.multiple_of(blk * _DQ, _DQ)
        ws = pl.multiple_of(jnp.clip(t0 - DIL_RADIUS, 0, S - _DW), DIL_RADIUS)
        block(q_ref[0, 0, pl.ds(t0, _DQ), :], k_ref[0, 0, pl.ds(ws, _DW), :],
              v_ref[0, 0, pl.ds(ws, _DW), :], t12_ref[0, 0, case_of(blk, n1)], m1, l1, a1, t0)
        return carry

    lax.fori_loop(0, n1, p1_body, 0, unroll=_UNROLL)

    n2 = L4 // _DQ

    def p2_body(j, carry):
        r = j // n2
        blk = j % n2
        l0 = blk * _DQ
        ws = jnp.clip(l0 - DIL_RADIUS, 0, L4 - _DW)
        row = pl.multiple_of(r * L4 + l0, _DQ)
        krow = pl.multiple_of(r * L4 + ws, DIL_RADIUS)
        block(q4[pl.ds(row, _DQ), :], k4[pl.ds(krow, _DW), :], v4[pl.ds(krow, _DW), :],
              t12_ref[0, 1, case_of(blk, n2)], m2, l2, a2, row)
        return carry

    lax.fori_loop(0, 4 * n2, p2_body, 0, unroll=_UNROLL)

    def p3_body(r, carry):
        row = pl.multiple_of(r * L16, L16)
        block(q16[pl.ds(row, L16), :], k16[pl.ds(row, L16), :], v16[pl.ds(row, L16), :],
              t3_ref[0], m3, l3, a3, row)
        return carry

    lax.fori_loop(0, 16, p3_body, 0, unroll=_UNROLL)

    for r16 in range(16):
        r4, c4 = r16 % 4, r16 // 4
        via4 = pl.ds(r4 * L4 + c4, L16, stride=4)
        via16 = pl.ds(r16 * L16, L16)
        mb, mc = m2[via4, :], m3[via16, :]
        mx = jnp.maximum(mb, mc)
        wb, wc = jnp.exp2(mb - mx), jnp.exp2(mc - mx)
        l2[via4, :] = wb * l2[via4, :] + wc * l3[via16, :]
        a2[via4, :] = wb * a2[via4, :] + wc * a3[via16, :]
        m2[via4, :] = mx
    for r4 in range(4):
        for part in range(L4 // _DQ):
            nat = pl.ds(r4 + 4 * _DQ * part, _DQ, stride=4)
            via4 = pl.ds(r4 * L4 + _DQ * part, _DQ)
            ma, mb = m1[nat, :], m2[via4, :]
            mx = jnp.maximum(ma, mb)
            wa, wb = jnp.exp2(ma - mx), jnp.exp2(mb - mx)
            den = wa * l1[nat, :] + wb * l2[via4, :]
            num = wa * a1[nat, :] + wb * a2[via4, :]
            onat[nat, :] = num / den
    o_ref[0] = onat[...].astype(BF16)


def _dilated_attention(qkv, t12, t3):
    B, _, S, _ = qkv.shape
    f32_buf = pltpu.VMEM((S, LANES), F32)
    bf16_buf = pltpu.VMEM((S, LANES), BF16)
    slab = lambda off: pl.BlockSpec((1, 1, S, LANES), lambda p, b: (b, off + p, 0, 0))
    return pl.pallas_call(
        _dilated_kernel,
        grid=(N_PAIRS, B),
        in_specs=[
            slab(0), slab(N_PAIRS), slab(2 * N_PAIRS),
            pl.BlockSpec((1, 2, 3, 2 * _DQ, _DW), lambda p, b: (p, 0, 0, 0, 0)),
            pl.BlockSpec((1, 2 * _DQ, _DQ), lambda p, b: (p, 0, 0)),
        ],
        out_specs=pl.BlockSpec((1, S, LANES), lambda p, b: (b, 0, p)),
        out_shape=jax.ShapeDtypeStruct((B, S, WIDTH), BF16),
        scratch_shapes=[f32_buf] * 2 + [bf16_buf] * 6 + [f32_buf] * 10,
        compiler_params=pltpu.CompilerParams(
            dimension_semantics=("arbitrary", "arbitrary"), vmem_limit_bytes=VMEM_LIMIT),
        name="dilated_attn",
    )(qkv, qkv, qkv, t12, t3)


_NQ_ROWS = 4
_NK_ROWS = 12


def _na_row_select(rows):
    n_blk = rows // _NQ_ROWS
    sel = np.full((n_blk, _NQ_ROWS, _NK_ROWS), -1, np.int64)
    for i in range(n_blk):
        kr0 = min(max(_NQ_ROWS * i - NA_WIN_ROWS // 2, 0), rows - _NK_ROWS)
        for a in range(_NQ_ROWS):
            qr = _NQ_ROWS * i + a
            rs = min(max(qr - NA_WIN_ROWS // 2, 0), rows - NA_WIN_ROWS)
            for b in range(_NK_ROWS):
                kr = kr0 + b
                if rs <= kr < rs + NA_WIN_ROWS:
                    sel[i, a, b] = kr - qr + NA_WIN_ROWS - 1
    for i in range(2, n_blk - 1):
        assert np.array_equal(sel[1], sel[i])
    return sel[[0, 1, n_blk - 1]]


def _na_column_bias(rpb):
    n_dr, n_dc = 2 * NA_WIN_ROWS - 1, 2 * NA_WIN_COLS - 1
    qc = np.arange(GRID_W)[:, None]
    kc = np.arange(GRID_W)[None, :]
    cs = np.clip(qc - NA_WIN_COLS // 2, 0, GRID_W - NA_WIN_COLS)
    col_ok = (kc >= cs) & (kc < cs + NA_WIN_COLS)
    dc = np.clip(kc - qc + NA_WIN_COLS - 1, 0, n_dc - 1)
    onehot = (dc.reshape(1, -1) == np.arange(n_dc)[:, None]).astype(np.float32)
    t = jnp.dot(rpb.astype(F32).reshape(N_HEADS * n_dr, n_dc), onehot, precision=lax.Precision.HIGHEST)
    t = jnp.where(col_ok[None, None], LOG2E * t.reshape(N_HEADS, n_dr, GRID_W, GRID_W), NEG_INF)
    neg = jnp.full((N_HEADS, 1, GRID_W, GRID_W), NEG_INF, F32)
    ext = jnp.concatenate([neg, t, neg], axis=1)
    return jnp.concatenate([ext[:, :n_dr + 1], ext[:, 1:]], axis=-1)


def _na_kernel(q_ref, k_ref, v_ref, cb_ref, o_ref, tab_ref):
    S = q_ref.shape[2]
    rows = S // GRID_W
    n_blk = rows // _NQ_ROWS
    nq = _NQ_ROWS * GRID_W
    nk = _NK_ROWS * GRID_W

    @pl.when(pl.program_id(1) == 0)
    def _():
        sel = _na_row_select(rows)
        lane = lax.broadcasted_iota(jnp.int32, (GRID_W, LANES), 1)
        masked = jnp.full((GRID_W, LANES), NEG_INF, F32)
        for c in range(sel.shape[0]):
            for a in range(_NQ_ROWS):
                for j in range(_NK_ROWS // 2):
                    s0, s1 = int(sel[c, a, 2 * j]), int(sel[c, a, 2 * j + 1])
                    for h in range(2):
                        if s0 < 0 and s1 < 0:
                            tile = masked
                        else:
                            d = s0 if s0 >= 0 else s1 - 1
                            assert s1 < 0 or s1 == d + 1
                            tile = cb_ref[h, d + 1]
                            if s0 < 0:
                                tile = jnp.where(lane >= GRID_W, tile, NEG_INF)
                            elif s1 < 0:
                                tile = jnp.where(lane < GRID_W, tile, NEG_INF)
                        r0 = h * nq + a * GRID_W
                        tab_ref[c, r0:r0 + GRID_W, j * LANES:(j + 1) * LANES] = tile

    def body(i, carry):
        q0 = pl.multiple_of(i * nq, nq)
        kr0 = jnp.clip(_NQ_ROWS * i - NA_WIN_ROWS // 2, 0, rows - _NK_ROWS)
        k0 = pl.multiple_of(kr0 * GRID_W, GRID_W)
        case = jnp.where(i == 0, 0, jnp.where(i == n_blk - 1, 2, 1))
        m, l, pv = _attend(q_ref[0, 0, pl.ds(q0, nq), :], k_ref[0, 0, pl.ds(k0, nk), :],
                           v_ref[0, 0, pl.ds(k0, nk), :], tab_ref[case])
        o = pv / l
        lane = lax.broadcasted_iota(jnp.int32, (nq, LANES), 1)
        o_ref[0, pl.ds(q0, nq), :] = jnp.where(lane < HEAD_DIM, o[:nq], o[nq:]).astype(BF16)
        return carry

    lax.fori_loop(0, n_blk, body, 0, unroll=8)


def _na_attention(qkv, col_bias):
    B, _, S, _ = qkv.shape
    slab = lambda off: pl.BlockSpec((1, 1, S, LANES), lambda p, b: (b, off + p, 0, 0))
    nq, nk = _NQ_ROWS * GRID_W, _NK_ROWS * GRID_W
    return pl.pallas_call(
        _na_kernel,
        grid=(N_PAIRS, B),
        in_specs=[
            slab(3 * N_PAIRS), slab(4 * N_PAIRS), slab(5 * N_PAIRS),
            pl.BlockSpec((2,) + col_bias.shape[1:], lambda p, b: (p, 0, 0, 0)),
        ],
        out_specs=pl.BlockSpec((1, S, LANES), lambda p, b: (b, 0, p)),
        out_shape=jax.ShapeDtypeStruct((B, S, WIDTH), BF16),
        scratch_shapes=[pltpu.VMEM((3, 2 * nq, nk), F32)],
        compiler_params=pltpu.CompilerParams(
            dimension_semantics=("arbitrary", "arbitrary"), vmem_limit_bytes=VMEM_LIMIT),
        name="na_attn",
    )(qkv, qkv, qkv, col_bias)


_GROUP_LANE0 = N_EXPERTS
_ROUTE_GROUP_LANE = EXPERTS_PER_GROUP


def _route(logits):
    lane = lax.broadcasted_iota(jnp.int32, logits.shape, 1)
    big = jnp.int32(LANES)
    is_group = (lane >= _GROUP_LANE0) & (lane < _GROUP_LANE0 + N_GROUPS)
    gl = jnp.where(is_group, logits, NEG_INF)
    gmax = jnp.max(gl, axis=-1, keepdims=True)
    g_idx = jnp.min(jnp.where(is_group & (gl == gmax), lane, big), axis=-1, keepdims=True) - _GROUP_LANE0
    g_weight = 1.0 / jnp.sum(jnp.where(is_group, jnp.exp(gl - gmax), 0.0), axis=-1, keepdims=True)
    in_group = (lane < N_EXPERTS) & ((lane // EXPERTS_PER_GROUP) == g_idx)
    el = jnp.where(in_group, logits, NEG_INF)
    v1 = jnp.max(el, axis=-1, keepdims=True)
    i1 = jnp.min(jnp.where(in_group & (el == v1), lane, big), axis=-1, keepdims=True)
    rest = in_group & (lane != i1)
    el2 = jnp.where(rest, logits, NEG_INF)
    v2 = jnp.max(el2, axis=-1, keepdims=True)
    i2 = jnp.min(jnp.where(rest & (el2 == v2), lane, big), axis=-1, keepdims=True)
    e2 = jnp.exp(v2 - v1)
    w1 = g_weight / (1.0 + e2)
    w2 = g_weight * e2 / (1.0 + e2)
    base = g_idx * EXPERTS_PER_GROUP
    return jnp.where(lane == i1 - base, w1,
                     jnp.where(lane == i2 - base, w2,
                               jnp.where(lane == _ROUTE_GROUP_LANE, g_idx.astype(F32), 0.0)))


def _outproj_kernel(x_ref, yd_ref, yn_ref, gd_ref, gn_ref, wo_ref, gf_ref, wr_ref,
                    br_ref, h_ref, hn_ref, comb_ref):
    yd = _rms(yd_ref[...].astype(F32), gd_ref[...]).astype(BF16)
    yn = _rms(yn_ref[...].astype(F32), gn_ref[...]).astype(BF16)
    h = x_ref[...] + jnp.dot(jnp.concatenate([yd, yn], axis=-1), wo_ref[...], preferred_element_type=F32)
    h_ref[...] = h
    hn = _rms(h, gf_ref[...]).astype(BF16)
    hn_ref[...] = hn
    logits = jnp.dot(hn, wr_ref[...], preferred_element_type=F32) + br_ref[...]
    comb_ref[...] = _route(logits)


def _outproj(x2, yd2, yn2, gd, gn, wo, gf, wr, br, tm):
    N, D = x2.shape
    row = lambda w: pl.BlockSpec((tm, w), lambda i: (i, 0))
    full = lambda a, b: pl.BlockSpec((a, b), lambda i: (0, 0))
    return pl.pallas_call(
        _outproj_kernel,
        grid=(N // tm,),
        in_specs=[row(D), row(WIDTH), row(WIDTH), full(1, WIDTH), full(1, WIDTH), full(2 * WIDTH, D),
                  full(1, D), full(D, LANES), full(1, LANES)],
        out_specs=[row(D), row(D), row(LANES)],
        out_shape=[jax.ShapeDtypeStruct((N, D), F32), jax.ShapeDtypeStruct((N, D), BF16),
                   jax.ShapeDtypeStruct((N, LANES), F32)],
        compiler_params=pltpu.CompilerParams(
            dimension_semantics=("arbitrary",), vmem_limit_bytes=VMEM_LIMIT),
        name="outproj_route",
    )(x2, yd2, yn2, gd, gn, wo, gf, wr, br)


_MOE_TILE = 1024
_MOE_CHUNK = 144
_MOE_NCHUNK = _MOE_TILE // _MOE_CHUNK + N_GROUPS
_GROUP_WIDTH = EXPERTS_PER_GROUP * D_EXPERT
_ROUTE_PIECE = 8
_MOE_VMEM_LIMIT = 58 * 1024 * 1024


def _moe_kernel(hn_ref, route_ref, h_ref, wg_ref, wu_ref, wd_ref, gfin_ref, y_ref, ys_ref, xcat_ref):
    T, C = _MOE_TILE, _MOE_CHUNK
    route = route_ref[...]
    lane = lax.broadcasted_iota(jnp.int32, (T, LANES), 1)
    gid = jnp.sum(jnp.where(lane == _ROUTE_GROUP_LANE, route, 0.0), axis=-1, keepdims=True)
    onehot = jnp.where((lane < N_GROUPS) & (lane.astype(F32) == gid), 1.0, 0.0)

    before = (lax.broadcasted_iota(jnp.int32, (LANES, LANES), 1)
              < lax.broadcasted_iota(jnp.int32, (LANES, LANES), 0)).astype(BF16)
    count = jnp.zeros((1, LANES), F32)
    ranks = []
    for blk in range(T // LANES):
        oh = onehot[blk * LANES:(blk + 1) * LANES]
        ranks.append(jnp.dot(before, oh.astype(BF16), preferred_element_type=F32) + count)
        count = count + jnp.sum(oh, axis=0, keepdims=True)
    rank = jnp.concatenate(ranks, axis=0)
    nchunk = jnp.floor((count + (C - 1)) * (1.0 / C)).astype(jnp.int32)
    off1 = nchunk[0, 0]
    off2 = off1 + nchunk[0, 1]
    off3 = off2 + nchunk[0, 2]
    n_used = off3 + nchunk[0, 3]
    start = jnp.where(lane == 1, off1, jnp.where(lane == 2, off2, jnp.where(lane == 3, off3, 0)))
    pos = jnp.sum(onehot * (rank + (start * C).astype(F32)), axis=-1, keepdims=True)
    pos_i = pos.astype(jnp.int32)
    pos_row = jnp.transpose(jnp.broadcast_to(pos, (T, LANES)))[0:1, :].astype(jnp.int32)

    r_hi = route.astype(BF16).astype(F32)
    r_mid = (route - r_hi).astype(BF16).astype(F32)
    r_lo = (route - r_hi - r_mid).astype(BF16).astype(F32)
    packed = r_hi + pltpu.roll(r_mid, _ROUTE_PIECE, axis=1) + pltpu.roll(r_lo, 2 * _ROUTE_PIECE, axis=1)
    xcat_ref[:, :hn_ref.shape[1]] = hn_ref[...]
    xcat_ref[:, hn_ref.shape[1]:] = packed.astype(BF16)

    ys_ref[...] = jnp.zeros_like(ys_ref)

    def chunk_body(c, carry):
        g = ((c >= off1).astype(jnp.int32) + (c >= off2).astype(jnp.int32)
             + (c >= off3).astype(jnp.int32))
        row0 = pl.multiple_of(c * C, 16)
        sel = (pos_row == row0 + lax.broadcasted_iota(jnp.int32, (C, T), 0)).astype(BF16)
        xr = jnp.dot(sel, xcat_ref[...], preferred_element_type=F32)
        xs = xr[:, :hn_ref.shape[1]].astype(BF16)
        r3 = xr[:, hn_ref.shape[1]:]
        r = (r3 + pltpu.roll(r3, LANES - _ROUTE_PIECE, axis=1)
             + pltpu.roll(r3, LANES - 2 * _ROUTE_PIECE, axis=1))
        clane = lax.broadcasted_iota(jnp.int32, (C, LANES), 1)
        parts = []
        for j in range(EXPERTS_PER_GROUP):
            e = g * EXPERTS_PER_GROUP + j
            gate = jnp.dot(xs, wg_ref[e], preferred_element_type=F32)
            up = jnp.dot(xs, wu_ref[e], preferred_element_type=F32)
            wj = jnp.sum(jnp.where(clane == j, r, 0.0), axis=-1, keepdims=True)
            parts.append((gate / (1.0 + jnp.exp(-gate))) * up * wj)
        act = jnp.concatenate(parts, axis=-1).astype(BF16)
        ys_ref[pl.ds(row0, C), :] = jnp.dot(act, wd_ref[g], preferred_element_type=F32).astype(BF16)
        return carry

    lax.fori_loop(0, n_used, chunk_body, 0)

    n_main = N_GROUPS * (-(-(T // N_GROUPS) // C))
    n_tail = _MOE_NCHUNK - n_main
    back = (lax.broadcasted_iota(jnp.int32, (T, n_main * C), 1) == pos_i).astype(BF16)
    y_ref[...] = h_ref[...] + jnp.dot(back, ys_ref[:n_main * C, :], preferred_element_type=F32)

    @pl.when(n_used > n_main)
    def _():
        tail = (lax.broadcasted_iota(jnp.int32, (T, n_tail * C), 1) == pos_i - n_main * C).astype(BF16)
        y_ref[...] += jnp.dot(tail, ys_ref[n_main * C:, :], preferred_element_type=F32)

    y_ref[...] = _rms(y_ref[...], gfin_ref[...])


def _moe(hn, route, h, wg, wu, wd, gfin):
    N, D = hn.shape
    T = _MOE_TILE
    row = lambda w, **kw: pl.BlockSpec((T, w), lambda i: (i, 0), **kw)
    whole = lambda a: pl.BlockSpec(a.shape, lambda i: (0,) * a.ndim)
    return pl.pallas_call(
        _moe_kernel,
        grid=(N // T,),
        in_specs=[row(D), row(LANES), row(D),
                  whole(wg), whole(wu), whole(wd), whole(gfin)],
        out_specs=row(D),
        out_shape=jax.ShapeDtypeStruct((N, D), F32),
        scratch_shapes=[pltpu.VMEM((_MOE_NCHUNK * _MOE_CHUNK, D), BF16), pltpu.VMEM((T, D + LANES), BF16)],
        compiler_params=pltpu.CompilerParams(
            dimension_semantics=("arbitrary",), vmem_limit_bytes=_MOE_VMEM_LIMIT),
        name="moe_grouped",
    )(hn, route, h, wg, wu, wd, gfin)


def kernel(x, norm_mix_g, w_in, rpb, g_out_dil, g_out_na, w_out, norm_ffn_g, w_group, b_group,
           w_router, b_router, w_gate, w_up, w_down, norm_final_g):
    B, S, D = x.shape
    N = B * S
    depth = w_in.shape[0]
    assert depth == 1 and D == D_MODEL and S % (16 * _DQ) == 0

    t12, t3 = _dilated_tables()

    layer = 0
    col_scale = np.ones((6, WIDTH), np.float32)
    col_scale[0] = col_scale[3] = LOG2E * HEAD_DIM ** -0.5
    w_in_b = (w_in[layer] * col_scale.reshape(1, -1)).astype(BF16)

    qkv = _inproj(x, norm_mix_g[layer].reshape(1, D), w_in_b, tm=512)

    y_dil = _dilated_attention(qkv, jnp.asarray(t12), jnp.asarray(t3))

    y_na = _na_attention(qkv, _na_column_bias(rpb[layer]))

    n_route = N_EXPERTS + N_GROUPS
    w_r = jnp.concatenate([w_router[layer], w_group[layer], jnp.zeros((D, LANES - n_route), F32)], axis=1)
    b_r = jnp.concatenate([b_router[layer], b_group[layer], jnp.zeros((LANES - n_route,), F32)]).reshape(1, LANES)

    h, hn, route = _outproj(
        x.reshape(N, D), y_dil.reshape(N, WIDTH), y_na.reshape(N, WIDTH),
        g_out_dil[layer].reshape(1, WIDTH), g_out_na[layer].reshape(1, WIDTH),
        w_out[layer].astype(BF16), norm_ffn_g[layer].reshape(1, D), w_r.astype(BF16), b_r, tm=512)

    wd = w_down[layer].astype(BF16).reshape(N_GROUPS, _GROUP_WIDTH, D)
    y = _moe(hn, route, h, w_gate[layer].astype(BF16), w_up[layer].astype(BF16), wd,
             norm_final_g.reshape(1, D))
    return y.reshape(B, S, D)
```

```python
import numpy as np
import jax
import jax.numpy as jnp
from jax import lax
from jax.experimental import pallas as pl
from jax.experimental.pallas import tpu as pltpu

D_MODEL = 1024
HEAD_DIM = 64
N_HEADS = 8
N_PAIRS = N_HEADS // 2
WIDTH = N_HEADS * HEAD_DIM
N_SLABS = 6 * N_PAIRS
DIL_PATTERNS = ((128, 1), (512, 4), (2048, 16))
DIL_RADIUS = 64
GRID_W = 64
NA_WIN_ROWS = 8
NA_WIN_COLS = 16
N_GROUPS = 4
EXPERTS_PER_GROUP = 4
N_EXPERTS = 16
D_EXPERT = 256
RMS_EPS = 1e-6
NEG_INF = -1e30
LOG2E = 1.4426950408889634

LANES = 128
VMEM_LIMIT = 48 * 1024 * 1024

F32 = jnp.float32
BF16 = jnp.bfloat16


def _rms(x, gain):
    return x * lax.rsqrt(jnp.mean(x * x, axis=-1, keepdims=True) + RMS_EPS) * gain


def _inproj_kernel(x_ref, g_ref, w_ref, o_ref):
    xn = _rms(x_ref[0], g_ref[...]).astype(BF16)
    chunk = 4 * LANES
    for c in range(N_SLABS * LANES // chunk):
        acc = jnp.dot(xn, w_ref[:, c * chunk:(c + 1) * chunk], preferred_element_type=F32)
        for j in range(chunk // LANES):
            o_ref[0, c * (chunk // LANES) + j] = acc[:, j * LANES:(j + 1) * LANES].astype(BF16)


def _inproj(x, gain, w_bf16, tm):
    B, S, D = x.shape
    return pl.pallas_call(
        _inproj_kernel,
        grid=(B, S // tm),
        in_specs=[
            pl.BlockSpec((1, tm, D), lambda b, i: (b, i, 0)),
            pl.BlockSpec((1, D), lambda b, i: (0, 0)),
            pl.BlockSpec((D, N_SLABS * LANES), lambda b, i: (0, 0)),
        ],
        out_specs=pl.BlockSpec((1, N_SLABS, tm, LANES), lambda b, i: (b, 0, i, 0)),
        out_shape=jax.ShapeDtypeStruct((B, N_SLABS, S, LANES), BF16),
        compiler_params=pltpu.CompilerParams(
            dimension_semantics=("arbitrary", "arbitrary"), vmem_limit_bytes=VMEM_LIMIT),
        name="inproj",
    )(x, gain, w_bf16)


def _attend(qb, kw, vw, bias, sums_on_mxu):
    lane = lax.broadcasted_iota(jnp.int32, qb.shape, 1)
    zero = jnp.zeros_like(qb)
    qq = jnp.concatenate([jnp.where(lane < HEAD_DIM, qb, zero),
                          jnp.where(lane >= HEAD_DIM, qb, zero)], axis=0)
    s = lax.dot_general(qq, kw, (((1,), (1,)), ((), ())), preferred_element_type=F32) + bias
    m = jnp.max(s, axis=-1, keepdims=True)
    p = jnp.exp2(s - m)
    if not sums_on_mxu:
        l = jnp.sum(p, axis=-1, keepdims=True)
        return m, l, jnp.dot(p.astype(BF16), vw, preferred_element_type=F32)
    pv = jnp.dot(p.astype(BF16), jnp.concatenate([vw, jnp.ones_like(vw)], axis=1),
                 preferred_element_type=F32)
    return m, pv[:, LANES:], pv[:, :LANES]


def _merge_heads(top, bottom, q):
    lane = lax.broadcasted_iota(jnp.int32, (q, LANES), 1)
    return jnp.where(lane < HEAD_DIM, jnp.broadcast_to(top, (q, LANES)),
                     jnp.broadcast_to(bottom, (q, LANES)))


_DQ = 128
_DW = 256
_UNROLL = 16


def _dilated_tables():
    slopes = 2.0 ** (-(np.arange(N_HEADS) + 1.0))
    q = np.arange(_DQ)[:, None]
    t12 = np.zeros((N_PAIRS, 2, 3, 2 * _DQ, _DW), np.float32)
    k = np.arange(_DW)[None, :]
    for pat, dil in enumerate((1, 4)):
        for case, off in enumerate((0, _DW // 4, _DW // 2)):
            delta = np.abs(k - (q + off))
            for h in range(N_HEADS):
                tab = np.where(delta <= DIL_RADIUS, -LOG2E * slopes[h] * dil * delta, NEG_INF)
                t12[h // 2, pat, case, (h % 2) * _DQ:(h % 2 + 1) * _DQ] = tab
    t3 = np.zeros((N_PAIRS, 2 * _DQ, _DQ), np.float32)
    delta = np.abs(np.arange(_DQ)[None, :] - q)
    for h in range(N_HEADS):
        t3[h // 2, (h % 2) * _DQ:(h % 2 + 1) * _DQ] = np.where(
            delta <= DIL_RADIUS, -LOG2E * slopes[h] * 16 * delta, NEG_INF)
    return t12, t3


def _dilated_kernel(q_ref, k_ref, v_ref, t12_ref, t3_ref, o_ref,
                    tmp, tmp4, q4, k4, v4, q16, k16, v16,
                    m1, l1, a1, m2, l2, a2, m3, l3, a3, onat):
    S = tmp.shape[0]
    L4, L16 = S // 4, S // 16

    for src, d4, d16 in ((q_ref, q4, q16), (k_ref, k4, k16), (v_ref, v4, v16)):
        tmp[...] = src[0, 0].astype(F32)
        for r in range(4):
            sub = tmp[pl.ds(r, L4, stride=4), :]
            tmp4[r * L4:(r + 1) * L4, :] = sub
            d4[r * L4:(r + 1) * L4, :] = sub.astype(BF16)
        for r16 in range(16):
            r4, c4 = r16 % 4, r16 // 4
            d16[r16 * L16:(r16 + 1) * L16, :] = tmp4[pl.ds(r4 * L4 + c4, L16, stride=4), :].astype(BF16)

    def block(qb, kw, vw, bias, m_ref, l_ref, a_ref, row):
        m, l, pv = _attend(qb, kw, vw, bias, sums_on_mxu=False)
        m_ref[pl.ds(row, _DQ), :] = _merge_heads(m[:_DQ], m[_DQ:], _DQ)
        l_ref[pl.ds(row, _DQ), :] = _merge_heads(l[:_DQ], l[_DQ:], _DQ)
        a_ref[pl.ds(row, _DQ), :] = _merge_heads(pv[:_DQ], pv[_DQ:], _DQ)

    def case_of(blk, n_blk):
        return jnp.where(blk == 0, 0, jnp.where(blk == n_blk - 1, 2, 1))

    n1 = S // _DQ

    def p1_body(blk, carry):
        t0 = pl.multiple_of(blk * _DQ, _DQ)
        ws = pl.multiple_of(jnp.clip(t0 - DIL_RADIUS, 0, S - _DW), DIL_RADIUS)
        block(q_ref[0, 0, pl.ds(t0, _DQ), :], k_ref[0, 0, pl.ds(ws, _DW), :],
              v_ref[0, 0, pl.ds(ws, _DW), :], t12_ref[0, 0, case_of(blk, n1)], m1, l1, a1, t0)
        return carry

    lax.fori_loop(0, n1, p1_body, 0, unroll=_UNROLL)

    n2 = L4 // _DQ

    def p2_body(j, carry):
        r = j // n2
        blk = j % n2
        l0 = blk * _DQ
        ws = jnp.clip(l0 - DIL_RADIUS, 0, L4 - _DW)
        row = pl.multiple_of(r * L4 + l0, _DQ)
        krow = pl.multiple_of(r * L4 + ws, DIL_RADIUS)
        block(q4[pl.ds(row, _DQ), :], k4[pl.ds(krow, _DW), :], v4[pl.ds(krow, _DW), :],
              t12_ref[0, 1, case_of(blk, n2)], m2, l2, a2, row)
        return carry

    lax.fori_loop(0, 4 * n2, p2_body, 0, unroll=_UNROLL)

    def p3_body(r, carry):
        row = pl.multiple_of(r * L16, L16)
        block(q16[pl.ds(row, L16), :], k16[pl.ds(row, L16), :], v16[pl.ds(row, L16), :],
              t3_ref[0], m3, l3, a3, row)
        return carry

    lax.fori_loop(0, 16, p3_body, 0, unroll=_UNROLL)

    for r16 in range(16):
        r4, c4 = r16 % 4, r16 // 4
        via4 = pl.ds(r4 * L4 + c4, L16, stride=4)
        via16 = pl.ds(r16 * L16, L16)
        mb, mc = m2[via4, :], m3[via16, :]
        mx = jnp.maximum(mb, mc)
        wb, wc = jnp.exp2(mb - mx), jnp.exp2(mc - mx)
        l2[via4, :] = wb * l2[via4, :] + wc * l3[via16, :]
        a2[via4, :] = wb * a2[via4, :] + wc * a3[via16, :]
        m2[via4, :] = mx
    for r4 in range(4):
        for part in range(L4 // _DQ):
            nat = pl.ds(r4 + 4 * _DQ * part, _DQ, stride=4)
            via4 = pl.ds(r4 * L4 + _DQ * part, _DQ)
            ma, mb = m1[nat, :], m2[via4, :]
            mx = jnp.maximum(ma, mb)
            wa, wb = jnp.exp2(ma - mx), jnp.exp2(mb - mx)
            den = wa * l1[nat, :] + wb * l2[via4, :]
            num = wa * a1[nat, :] + wb * a2[via4, :]
            onat[nat, :] = num / den
    o_ref[0] = onat[...].astype(BF16)


def _dilated_attention(qkv, t12, t3):
    B, _, S, _ = qkv.shape
    f32_buf = pltpu.VMEM((S, LANES), F32)
    bf16_buf = pltpu.VMEM((S, LANES), BF16)
    slab = lambda off: pl.BlockSpec((1, 1, S, LANES), lambda p, b: (b, off + p, 0, 0))
    return pl.pallas_call(
        _dilated_kernel,
        grid=(N_PAIRS, B),
        in_specs=[
            slab(0), slab(N_PAIRS), slab(2 * N_PAIRS),
            pl.BlockSpec((1, 2, 3, 2 * _DQ, _DW), lambda p, b: (p, 0, 0, 0, 0)),
            pl.BlockSpec((1, 2 * _DQ, _DQ), lambda p, b: (p, 0, 0)),
        ],
        out_specs=pl.BlockSpec((1, S, LANES), lambda p, b: (b, 0, p)),
        out_shape=jax.ShapeDtypeStruct((B, S, WIDTH), BF16),
        scratch_shapes=[f32_buf] * 2 + [bf16_buf] * 6 + [f32_buf] * 10,
        compiler_params=pltpu.CompilerParams(
            dimension_semantics=("arbitrary", "arbitrary"), vmem_limit_bytes=VMEM_LIMIT),
        name="dilated_attn",
    )(qkv, qkv, qkv, t12, t3)


_NQ_ROWS = 4
_NK_ROWS = 12


def _na_row_select(rows):
    n_blk = rows // _NQ_ROWS
    sel = np.full((n_blk, _NQ_ROWS, _NK_ROWS), -1, np.int64)
    for i in range(n_blk):
        kr0 = min(max(_NQ_ROWS * i - NA_WIN_ROWS // 2, 0), rows - _NK_ROWS)
        for a in range(_NQ_ROWS):
            qr = _NQ_ROWS * i + a
            rs = min(max(qr - NA_WIN_ROWS // 2, 0), rows - NA_WIN_ROWS)
            for b in range(_NK_ROWS):
                kr = kr0 + b
                if rs <= kr < rs + NA_WIN_ROWS:
                    sel[i, a, b] = kr - qr + NA_WIN_ROWS - 1
    for i in range(2, n_blk - 1):
        assert np.array_equal(sel[1], sel[i])
    return sel[[0, 1, n_blk - 1]]


def _na_column_bias(rpb):
    n_dr, n_dc = 2 * NA_WIN_ROWS - 1, 2 * NA_WIN_COLS - 1
    qc = np.arange(GRID_W)[:, None]
    kc = np.arange(GRID_W)[None, :]
    cs = np.clip(qc - NA_WIN_COLS // 2, 0, GRID_W - NA_WIN_COLS)
    col_ok = (kc >= cs) & (kc < cs + NA_WIN_COLS)
    dc = np.clip(kc - qc + NA_WIN_COLS - 1, 0, n_dc - 1)
    onehot = (dc.reshape(1, -1) == np.arange(n_dc)[:, None]).astype(np.float32)
    t = jnp.dot(rpb.astype(F32).reshape(N_HEADS * n_dr, n_dc), onehot, precision=lax.Precision.HIGHEST)
    t = jnp.where(col_ok[None, None], LOG2E * t.reshape(N_HEADS, n_dr, GRID_W, GRID_W), NEG_INF)
    neg = jnp.full((N_HEADS, 1, GRID_W, GRID_W), NEG_INF, F32)
    ext = jnp.concatenate([neg, t, neg], axis=1)
    return jnp.concatenate([ext[:, :n_dr + 1], ext[:, 1:]], axis=-1)


def _na_kernel(q_ref, k_ref, v_ref, cb_ref, o_ref, tab_ref):
    S = q_ref.shape[2]
    rows = S // GRID_W
    n_blk = rows // _NQ_ROWS
    nq = _NQ_ROWS * GRID_W
    nk = _NK_ROWS * GRID_W

    @pl.when(pl.program_id(1) == 0)
    def _():
        sel = _na_row_select(rows)
        lane = lax.broadcasted_iota(jnp.int32, (GRID_W, LANES), 1)
        masked = jnp.full((GRID_W, LANES), NEG_INF, F32)
        for c in range(sel.shape[0]):
            for a in range(_NQ_ROWS):
                for j in range(_NK_ROWS // 2):
                    s0, s1 = int(sel[c, a, 2 * j]), int(sel[c, a, 2 * j + 1])
                    for h in range(2):
                        if s0 < 0 and s1 < 0:
                            tile = masked
                        else:
                            d = s0 if s0 >= 0 else s1 - 1
                            assert s1 < 0 or s1 == d + 1
                            tile = cb_ref[h, d + 1]
                            if s0 < 0:
                                tile = jnp.where(lane >= GRID_W, tile, NEG_INF)
                            elif s1 < 0:
                                tile = jnp.where(lane < GRID_W, tile, NEG_INF)
                        r0 = h * nq + a * GRID_W
                        tab_ref[c, r0:r0 + GRID_W, j * LANES:(j + 1) * LANES] = tile

    def body(i, carry):
        q0 = pl.multiple_of(i * nq, nq)
        kr0 = jnp.clip(_NQ_ROWS * i - NA_WIN_ROWS // 2, 0, rows - _NK_ROWS)
        k0 = pl.multiple_of(kr0 * GRID_W, GRID_W)
        case = jnp.where(i == 0, 0, jnp.where(i == n_blk - 1, 2, 1))
        m, l, pv = _attend(q_ref[0, 0, pl.ds(q0, nq), :], k_ref[0, 0, pl.ds(k0, nk), :],
                           v_ref[0, 0, pl.ds(k0, nk), :], tab_ref[case], sums_on_mxu=True)
        o = pv / l
        lane = lax.broadcasted_iota(jnp.int32, (nq, LANES), 1)
        o_ref[0, pl.ds(q0, nq), :] = jnp.where(lane < HEAD_DIM, o[:nq], o[nq:]).astype(BF16)
        return carry

    lax.fori_loop(0, n_blk, body, 0, unroll=8)


def _na_attention(qkv, col_bias):
    B, _, S, _ = qkv.shape
    slab = lambda off: pl.BlockSpec((1, 1, S, LANES), lambda p, b: (b, off + p, 0, 0))
    nq, nk = _NQ_ROWS * GRID_W, _NK_ROWS * GRID_W
    return pl.pallas_call(
        _na_kernel,
        grid=(N_PAIRS, B),
        in_specs=[
            slab(3 * N_PAIRS), slab(4 * N_PAIRS), slab(5 * N_PAIRS),
            pl.BlockSpec((2,) + col_bias.shape[1:], lambda p, b: (p, 0, 0, 0)),
        ],
        out_specs=pl.BlockSpec((1, S, LANES), lambda p, b: (b, 0, p)),
        out_shape=jax.ShapeDtypeStruct((B, S, WIDTH), BF16),
        scratch_shapes=[pltpu.VMEM((3, 2 * nq, nk), F32)],
        compiler_params=pltpu.CompilerParams(
            dimension_semantics=("arbitrary", "arbitrary"), vmem_limit_bytes=VMEM_LIMIT),
        name="na_attn",
    )(qkv, qkv, qkv, col_bias)


_GROUP_LANE0 = N_EXPERTS
_ROUTE_GROUP_LANE = EXPERTS_PER_GROUP


def _route(logits):
    lane_i = lax.broadcasted_iota(jnp.int32, logits.shape, 1)
    lane = lane_i.astype(F32)
    big = float(LANES)
    is_group = (lane_i >= _GROUP_LANE0) & (lane_i < _GROUP_LANE0 + N_GROUPS)
    gl = jnp.where(is_group, logits, NEG_INF)
    gmax = jnp.max(gl, axis=-1, keepdims=True)
    g_idx = jnp.min(jnp.where(is_group & (gl == gmax), lane, big), axis=-1, keepdims=True) - _GROUP_LANE0
    g_weight = 1.0 / jnp.sum(jnp.where(is_group, jnp.exp(gl - gmax), 0.0), axis=-1, keepdims=True)
    in_group = (lane_i < N_EXPERTS) & ((lane_i // EXPERTS_PER_GROUP).astype(F32) == g_idx)
    el = jnp.where(in_group, logits, NEG_INF)
    v1 = jnp.max(el, axis=-1, keepdims=True)
    i1 = jnp.min(jnp.where(in_group & (el == v1), lane, big), axis=-1, keepdims=True)
    rest = in_group & (lane != i1)
    el2 = jnp.where(rest, logits, NEG_INF)
    v2 = jnp.max(el2, axis=-1, keepdims=True)
    i2 = jnp.min(jnp.where(rest & (el2 == v2), lane, big), axis=-1, keepdims=True)
    e2 = jnp.exp(v2 - v1)
    w1 = g_weight / (1.0 + e2)
    w2 = g_weight * e2 / (1.0 + e2)
    base = g_idx * EXPERTS_PER_GROUP
    return jnp.where(lane == i1 - base, w1,
                     jnp.where(lane == i2 - base, w2,
                               jnp.where(lane_i == _ROUTE_GROUP_LANE, g_idx, 0.0)))


def _outproj_kernel(x_ref, yd_ref, yn_ref, gd_ref, gn_ref, wo_ref, gf_ref, wr_ref,
                    br_ref, h_ref, hn_ref, comb_ref):
    yd = _rms(yd_ref[...].astype(F32), gd_ref[...]).astype(BF16)
    yn = _rms(yn_ref[...].astype(F32), gn_ref[...]).astype(BF16)
    h = x_ref[...] + jnp.dot(jnp.concatenate([yd, yn], axis=-1), wo_ref[...], preferred_element_type=F32)
    h_ref[...] = h
    hn = _rms(h, gf_ref[...]).astype(BF16)
    hn_ref[...] = hn
    logits = jnp.dot(hn, wr_ref[...], preferred_element_type=F32) + br_ref[...]
    comb_ref[...] = _route(logits)


def _outproj(x2, yd2, yn2, gd, gn, wo, gf, wr, br, tm):
    N, D = x2.shape
    row = lambda w: pl.BlockSpec((tm, w), lambda i: (i, 0))
    full = lambda a, b: pl.BlockSpec((a, b), lambda i: (0, 0))
    return pl.pallas_call(
        _outproj_kernel,
        grid=(N // tm,),
        in_specs=[row(D), row(WIDTH), row(WIDTH), full(1, WIDTH), full(1, WIDTH), full(2 * WIDTH, D),
                  full(1, D), full(D, LANES), full(1, LANES)],
        out_specs=[row(D), row(D), row(LANES)],
        out_shape=[jax.ShapeDtypeStruct((N, D), F32), jax.ShapeDtypeStruct((N, D), BF16),
                   jax.ShapeDtypeStruct((N, LANES), F32)],
        compiler_params=pltpu.CompilerParams(
            dimension_semantics=("arbitrary",), vmem_limit_bytes=VMEM_LIMIT),
        name="outproj_route",
    )(x2, yd2, yn2, gd, gn, wo, gf, wr, br)


_MOE_TILE = 1024
_MOE_CHUNK = 144
_MOE_NCHUNK = _MOE_TILE // _MOE_CHUNK + N_GROUPS
_GROUP_WIDTH = EXPERTS_PER_GROUP * D_EXPERT
_ROUTE_PIECE = 8
_MOE_VMEM_LIMIT = 58 * 1024 * 1024


def _moe_kernel(hn_ref, route_ref, h_ref, wg_ref, wu_ref, wd_ref, gfin_ref, y_ref, ys_ref, xcat_ref):
    T, C = _MOE_TILE, _MOE_CHUNK
    route = route_ref[...]
    lane = lax.broadcasted_iota(jnp.int32, (T, LANES), 1)
    gid = jnp.sum(jnp.where(lane == _ROUTE_GROUP_LANE, route, 0.0), axis=-1, keepdims=True)
    onehot = jnp.where((lane < N_GROUPS) & (lane.astype(F32) == gid), 1.0, 0.0)

    before = (lax.broadcasted_iota(jnp.int32, (LANES, LANES), 1)
              < lax.broadcasted_iota(jnp.int32, (LANES, LANES), 0)).astype(BF16)
    count = jnp.zeros((1, LANES), F32)
    ranks = []
    for blk in range(T // LANES):
        oh = onehot[blk * LANES:(blk + 1) * LANES]
        ranks.append(jnp.dot(before, oh.astype(BF16), preferred_element_type=F32) + count)
        count = count + jnp.sum(oh, axis=0, keepdims=True)
    rank = jnp.concatenate(ranks, axis=0)
    nchunk = jnp.floor((count + (C - 1)) * (1.0 / C)).astype(jnp.int32)
    off1 = nchunk[0, 0]
    off2 = off1 + nchunk[0, 1]
    off3 = off2 + nchunk[0, 2]
    n_used = off3 + nchunk[0, 3]
    start = jnp.where(lane == 1, off1, jnp.where(lane == 2, off2, jnp.where(lane == 3, off3, 0)))
    pos = jnp.sum(onehot * (rank + (start * C).astype(F32)), axis=-1, keepdims=True)
    pos_i = pos.astype(jnp.int32)
    pos_row = jnp.transpose(jnp.broadcast_to(pos, (T, LANES)))[0:1, :].astype(jnp.int32)

    r_hi = route.astype(BF16).astype(F32)
    r_mid = (route - r_hi).astype(BF16).astype(F32)
    r_lo = (route - r_hi - r_mid).astype(BF16).astype(F32)
    packed = r_hi + pltpu.roll(r_mid, _ROUTE_PIECE, axis=1) + pltpu.roll(r_lo, 2 * _ROUTE_PIECE, axis=1)
    xcat_ref[:, :hn_ref.shape[1]] = hn_ref[...]
    xcat_ref[:, hn_ref.shape[1]:] = packed.astype(BF16)

    ys_ref[...] = jnp.zeros_like(ys_ref)

    def chunk_body(c, carry):
        g = ((c >= off1).astype(jnp.int32) + (c >= off2).astype(jnp.int32)
             + (c >= off3).astype(jnp.int32))
        row0 = pl.multiple_of(c * C, 16)
        sel = (pos_row == row0 + lax.broadcasted_iota(jnp.int32, (C, T), 0)).astype(BF16)
        xr = jnp.dot(sel, xcat_ref[...], preferred_element_type=F32)
        xs = xr[:, :hn_ref.shape[1]].astype(BF16)
        r3 = xr[:, hn_ref.shape[1]:]
        r = (r3 + pltpu.roll(r3, LANES - _ROUTE_PIECE, axis=1)
             + pltpu.roll(r3, LANES - 2 * _ROUTE_PIECE, axis=1))
        clane = lax.broadcasted_iota(jnp.int32, (C, LANES), 1)
        parts = []
        for j in range(EXPERTS_PER_GROUP):
            e = g * EXPERTS_PER_GROUP + j
            gate = jnp.dot(xs, wg_ref[e], preferred_element_type=F32)
            up = jnp.dot(xs, wu_ref[e], preferred_element_type=F32)
            wj = jnp.sum(jnp.where(clane == j, r, 0.0), axis=-1, keepdims=True)
            parts.append((gate / (1.0 + jnp.exp(-gate))) * up * wj)
        act = jnp.concatenate(parts, axis=-1).astype(BF16)
        ys_ref[pl.ds(row0, C), :] = jnp.dot(act, wd_ref[g], preferred_element_type=F32).astype(BF16)
        return carry

    lax.fori_loop(0, n_used, chunk_body, 0)

    n_main = N_GROUPS * (-(-(T // N_GROUPS) // C))
    n_tail = _MOE_NCHUNK - n_main
    back = (lax.broadcasted_iota(jnp.int32, (T, n_main * C), 1) == pos_i).astype(BF16)
    y_ref[...] = h_ref[...] + jnp.dot(back, ys_ref[:n_main * C, :], preferred_element_type=F32)

    @pl.when(n_used > n_main)
    def _():
        tail = (lax.broadcasted_iota(jnp.int32, (T, n_tail * C), 1) == pos_i - n_main * C).astype(BF16)
        y_ref[...] += jnp.dot(tail, ys_ref[n_main * C:, :], preferred_element_type=F32)

    y_ref[...] = _rms(y_ref[...], gfin_ref[...])


def _moe(hn, route, h, wg, wu, wd, gfin):
    N, D = hn.shape
    T = _MOE_TILE
    row = lambda w: pl.BlockSpec((T, w), lambda i: (i, 0))
    whole = lambda a: pl.BlockSpec(a.shape, lambda i: (0,) * a.ndim)
    return pl.pallas_call(
        _moe_kernel,
        grid=(N // T,),
        in_specs=[row(D), row(LANES), row(D),
                  whole(wg), whole(wu), whole(wd), whole(gfin)],
        out_specs=row(D),
        out_shape=jax.ShapeDtypeStruct((N, D), F32),
        scratch_shapes=[pltpu.VMEM((_MOE_NCHUNK * _MOE_CHUNK, D), BF16), pltpu.VMEM((T, D + LANES), BF16)],
        compiler_params=pltpu.CompilerParams(
            dimension_semantics=("arbitrary",), vmem_limit_bytes=_MOE_VMEM_LIMIT),
        name="moe_grouped",
    )(hn, route, h, wg, wu, wd, gfin)


def kernel(x, norm_mix_g, w_in, rpb, g_out_dil, g_out_na, w_out, norm_ffn_g, w_group, b_group,
           w_router, b_router, w_gate, w_up, w_down, norm_final_g):
    B, S, D = x.shape
    N = B * S
    depth = w_in.shape[0]
    assert depth == 1 and D == D_MODEL and S % (16 * _DQ) == 0

    t12, t3 = _dilated_tables()

    layer = 0
    col_scale = np.ones((6, WIDTH), np.float32)
    col_scale[0] = col_scale[3] = LOG2E * HEAD_DIM ** -0.5
    w_in_b = (w_in[layer] * col_scale.reshape(1, -1)).astype(BF16)

    qkv = _inproj(x, norm_mix_g[layer].reshape(1, D), w_in_b, tm=512)

    y_dil = _dilated_attention(qkv, jnp.asarray(t12), jnp.asarray(t3))

    y_na = _na_attention(qkv, _na_column_bias(rpb[layer]))

    n_route = N_EXPERTS + N_GROUPS
    w_r = jnp.concatenate([w_router[layer], w_group[layer], jnp.zeros((D, LANES - n_route), F32)], axis=1)
    b_r = jnp.concatenate([b_router[layer], b_group[layer], jnp.zeros((LANES - n_route,), F32)]).reshape(1, LANES)

    h, hn, route = _outproj(
        x.reshape(N, D), y_dil.reshape(N, WIDTH), y_na.reshape(N, WIDTH),
        g_out_dil[layer].reshape(1, WIDTH), g_out_na[layer].reshape(1, WIDTH),
        w_out[layer].astype(BF16), norm_ffn_g[layer].reshape(1, D), w_r.astype(BF16), b_r, tm=512)

    wd = w_down[layer].astype(BF16).reshape(N_GROUPS, _GROUP_WIDTH, D)
    y = _moe(hn, route, h, w_gate[layer].astype(BF16), w_up[layer].astype(BF16), wd,
             norm_final_g.reshape(1, D))
    return y.reshape(B, S, D)
```

```python
import numpy as np
import jax
import jax.numpy as jnp
from jax import lax
from jax.experimental import pallas as pl
from jax.experimental.pallas import tpu as pltpu

D_MODEL = 1024
HEAD_DIM = 64
N_HEADS = 8
N_PAIRS = N_HEADS // 2
WIDTH = N_HEADS * HEAD_DIM
N_SLABS = 6 * N_PAIRS
DIL_PATTERNS = ((128, 1), (512, 4), (2048, 16))
DIL_RADIUS = 64
GRID_W = 64
NA_WIN_ROWS = 8
NA_WIN_COLS = 16
N_GROUPS = 4
EXPERTS_PER_GROUP = 4
N_EXPERTS = 16
D_EXPERT = 256
RMS_EPS = 1e-6
NEG_INF = -1e30
LOG2E = 1.4426950408889634

LANES = 128
VMEM_LIMIT = 48 * 1024 * 1024

F32 = jnp.float32
BF16 = jnp.bfloat16


def _rms(x, gain):
    return x * lax.rsqrt(jnp.mean(x * x, axis=-1, keepdims=True) + RMS_EPS) * gain


def _inproj_kernel(x_ref, g_ref, w_ref, o_ref):
    xn = _rms(x_ref[0], g_ref[...]).astype(BF16)
    chunk = 4 * LANES
    for c in range(N_SLABS * LANES // chunk):
        acc = jnp.dot(xn, w_ref[:, c * chunk:(c + 1) * chunk], preferred_element_type=F32)
        for j in range(chunk // LANES):
            o_ref[0, c * (chunk // LANES) + j] = acc[:, j * LANES:(j + 1) * LANES].astype(BF16)


def _inproj(x, gain, w_bf16, tm):
    B, S, D = x.shape
    return pl.pallas_call(
        _inproj_kernel,
        grid=(B, S // tm),
        in_specs=[
            pl.BlockSpec((1, tm, D), lambda b, i: (b, i, 0)),
            pl.BlockSpec((1, D), lambda b, i: (0, 0)),
            pl.BlockSpec((D, N_SLABS * LANES), lambda b, i: (0, 0)),
        ],
        out_specs=pl.BlockSpec((1, N_SLABS, tm, LANES), lambda b, i: (b, 0, i, 0)),
        out_shape=jax.ShapeDtypeStruct((B, N_SLABS, S, LANES), BF16),
        compiler_params=pltpu.CompilerParams(
            dimension_semantics=("arbitrary", "arbitrary"), vmem_limit_bytes=VMEM_LIMIT),
        name="inproj",
    )(x, gain, w_bf16)


def _attend(qb, kw, vw, bias, sums_on_mxu):
    lane = lax.broadcasted_iota(jnp.int32, qb.shape, 1)
    zero = jnp.zeros_like(qb)
    qq = jnp.concatenate([jnp.where(lane < HEAD_DIM, qb, zero),
                          jnp.where(lane >= HEAD_DIM, qb, zero)], axis=0)
    s = lax.dot_general(qq, kw, (((1,), (1,)), ((), ())), preferred_element_type=F32) + bias
    m = jnp.max(s, axis=-1, keepdims=True)
    p = jnp.exp2(s - m)
    if not sums_on_mxu:
        l = jnp.sum(p, axis=-1, keepdims=True)
        return m, l, jnp.dot(p.astype(BF16), vw, preferred_element_type=F32)
    pv = jnp.dot(p.astype(BF16), jnp.concatenate([vw, jnp.ones_like(vw)], axis=1),
                 preferred_element_type=F32)
    return m, pv[:, LANES:], pv[:, :LANES]


def _merge_heads(top, bottom, q):
    lane = lax.broadcasted_iota(jnp.int32, (q, LANES), 1)
    return jnp.where(lane < HEAD_DIM, jnp.broadcast_to(top, (q, LANES)),
                     jnp.broadcast_to(bottom, (q, LANES)))


_DQ = 128
_DW = 256
_UNROLL = 16


def _dilated_tables():
    slopes = 2.0 ** (-(np.arange(N_HEADS) + 1.0))
    q = np.arange(_DQ)[:, None]
    t12 = np.zeros((N_PAIRS, 2, 3, 2 * _DQ, _DW), np.float32)
    k = np.arange(_DW)[None, :]
    for pat, dil in enumerate((1, 4)):
        for case, off in enumerate((0, _DW // 4, _DW // 2)):
            delta = np.abs(k - (q + off))
            for h in range(N_HEADS):
                tab = np.where(delta <= DIL_RADIUS, -LOG2E * slopes[h] * dil * delta, NEG_INF)
                t12[h // 2, pat, case, (h % 2) * _DQ:(h % 2 + 1) * _DQ] = tab
    t3 = np.zeros((N_PAIRS, 2 * _DQ, _DQ), np.float32)
    delta = np.abs(np.arange(_DQ)[None, :] - q)
    for h in range(N_HEADS):
        t3[h // 2, (h % 2) * _DQ:(h % 2 + 1) * _DQ] = np.where(
            delta <= DIL_RADIUS, -LOG2E * slopes[h] * 16 * delta, NEG_INF)
    return t12, t3


def _dilated_kernel(q_ref, k_ref, v_ref, t12_ref, t3_ref, o_ref,
                    tmp, tmp4, q4, k4, v4, q16, k16, v16,
                    m1, l1, a1, m2, l2, a2, m3, l3, a3, onat):
    S = tmp.shape[0]
    L4, L16 = S // 4, S // 16

    for src, d4, d16 in ((q_ref, q4, q16), (k_ref, k4, k16), (v_ref, v4, v16)):
        tmp[...] = src[0, 0].astype(F32)
        for r in range(4):
            sub = tmp[pl.ds(r, L4, stride=4), :]
            tmp4[r * L4:(r + 1) * L4, :] = sub
            d4[r * L4:(r + 1) * L4, :] = sub.astype(BF16)
        for r16 in range(16):
            r4, c4 = r16 % 4, r16 // 4
            d16[r16 * L16:(r16 + 1) * L16, :] = tmp4[pl.ds(r4 * L4 + c4, L16, stride=4), :].astype(BF16)

    def block(qb, kw, vw, bias, m_ref, l_ref, a_ref, row):
        m, l, pv = _attend(qb, kw, vw, bias, sums_on_mxu=False)
        m_ref[pl.ds(row, _DQ), :] = _merge_heads(m[:_DQ], m[_DQ:], _DQ)
        l_ref[pl.ds(row, _DQ), :] = _merge_heads(l[:_DQ], l[_DQ:], _DQ)
        a_ref[pl.ds(row, _DQ), :] = _merge_heads(pv[:_DQ], pv[_DQ:], _DQ)

    def case_of(blk, n_blk):
        return jnp.where(blk == 0, 0, jnp.where(blk == n_blk - 1, 2, 1))

    n1 = S // _DQ

    def p1_body(blk, carry):
        t0 = pl.multiple_of(blk * _DQ, _DQ)
        ws = pl.multiple_of(jnp.clip(t0 - DIL_RADIUS, 0, S - _DW), DIL_RADIUS)
        block(q_ref[0, 0, pl.ds(t0, _DQ), :], k_ref[0, 0, pl.ds(ws, _DW), :],
              v_ref[0, 0, pl.ds(ws, _DW), :], t12_ref[0, 0, case_of(blk, n1)], m1, l1, a1, t0)
        return carry

    lax.fori_loop(0, n1, p1_body, 0, unroll=_UNROLL)

    n2 = L4 // _DQ

    def p2_body(j, carry):
        r = j // n2
        blk = j % n2
        l0 = blk * _DQ
        ws = jnp.clip(l0 - DIL_RADIUS, 0, L4 - _DW)
        row = pl.multiple_of(r * L4 + l0, _DQ)
        krow = pl.multiple_of(r * L4 + ws, DIL_RADIUS)
        block(q4[pl.ds(row, _DQ), :], k4[pl.ds(krow, _DW), :], v4[pl.ds(krow, _DW), :],
              t12_ref[0, 1, case_of(blk, n2)], m2, l2, a2, row)
        return carry

    lax.fori_loop(0, 4 * n2, p2_body, 0, unroll=_UNROLL)

    def p3_body(r, carry):
        row = pl.multiple_of(r * L16, L16)
        block(q16[pl.ds(row, L16), :], k16[pl.ds(row, L16), :], v16[pl.ds(row, L16), :],
              t3_ref[0], m3, l3, a3, row)
        return carry

    lax.fori_loop(0, 16, p3_body, 0, unroll=_UNROLL)

    for r16 in range(16):
        r4, c4 = r16 % 4, r16 // 4
        via4 = pl.ds(r4 * L4 + c4, L16, stride=4)
        via16 = pl.ds(r16 * L16, L16)
        mb, mc = m2[via4, :], m3[via16, :]
        mx = jnp.maximum(mb, mc)
        wb, wc = jnp.exp2(mb - mx), jnp.exp2(mc - mx)
        l2[via4, :] = wb * l2[via4, :] + wc * l3[via16, :]
        a2[via4, :] = wb * a2[via4, :] + wc * a3[via16, :]
        m2[via4, :] = mx
    for r4 in range(4):
        for part in range(L4 // _DQ):
            nat = pl.ds(r4 + 4 * _DQ * part, _DQ, stride=4)
            via4 = pl.ds(r4 * L4 + _DQ * part, _DQ)
            ma, mb = m1[nat, :], m2[via4, :]
            mx = jnp.maximum(ma, mb)
            wa, wb = jnp.exp2(ma - mx), jnp.exp2(mb - mx)
            den = wa * l1[nat, :] + wb * l2[via4, :]
            num = wa * a1[nat, :] + wb * a2[via4, :]
            onat[nat, :] = num / den
    o_ref[0] = onat[...].astype(BF16)


_NQ_ROWS = 4
_NK_ROWS = 12


def _na_row_select(rows):
    n_blk = rows // _NQ_ROWS
    sel = np.full((n_blk, _NQ_ROWS, _NK_ROWS), -1, np.int64)
    for i in range(n_blk):
        kr0 = min(max(_NQ_ROWS * i - NA_WIN_ROWS // 2, 0), rows - _NK_ROWS)
        for a in range(_NQ_ROWS):
            qr = _NQ_ROWS * i + a
            rs = min(max(qr - NA_WIN_ROWS // 2, 0), rows - NA_WIN_ROWS)
            for b in range(_NK_ROWS):
                kr = kr0 + b
                if rs <= kr < rs + NA_WIN_ROWS:
                    sel[i, a, b] = kr - qr + NA_WIN_ROWS - 1
    for i in range(2, n_blk - 1):
        assert np.array_equal(sel[1], sel[i])
    return sel[[0, 1, n_blk - 1]]


def _na_column_bias(rpb):
    n_dr, n_dc = 2 * NA_WIN_ROWS - 1, 2 * NA_WIN_COLS - 1
    qc = np.arange(GRID_W)[:, None]
    kc = np.arange(GRID_W)[None, :]
    cs = np.clip(qc - NA_WIN_COLS // 2, 0, GRID_W - NA_WIN_COLS)
    col_ok = (kc >= cs) & (kc < cs + NA_WIN_COLS)
    dc = np.clip(kc - qc + NA_WIN_COLS - 1, 0, n_dc - 1)
    onehot = (dc.reshape(1, -1) == np.arange(n_dc)[:, None]).astype(np.float32)
    t = jnp.dot(rpb.astype(F32).reshape(N_HEADS * n_dr, n_dc), onehot, precision=lax.Precision.HIGHEST)
    t = jnp.where(col_ok[None, None], LOG2E * t.reshape(N_HEADS, n_dr, GRID_W, GRID_W), NEG_INF)
    neg = jnp.full((N_HEADS, 1, GRID_W, GRID_W), NEG_INF, F32)
    ext = jnp.concatenate([neg, t, neg], axis=1)
    return jnp.concatenate([ext[:, :n_dr + 1], ext[:, 1:]], axis=-1)


def _na_expand_table(cb_ref, tab_ref, rows):
    nq = _NQ_ROWS * GRID_W

    @pl.when(pl.program_id(1) == 0)
    def _():
        sel = _na_row_select(rows)
        lane = lax.broadcasted_iota(jnp.int32, (GRID_W, LANES), 1)
        masked = jnp.full((GRID_W, LANES), NEG_INF, F32)
        for c in range(sel.shape[0]):
            for a in range(_NQ_ROWS):
                for j in range(_NK_ROWS // 2):
                    s0, s1 = int(sel[c, a, 2 * j]), int(sel[c, a, 2 * j + 1])
                    for h in range(2):
                        if s0 < 0 and s1 < 0:
                            tile = masked
                        else:
                            d = s0 if s0 >= 0 else s1 - 1
                            assert s1 < 0 or s1 == d + 1
                            tile = cb_ref[h, d + 1]
                            if s0 < 0:
                                tile = jnp.where(lane >= GRID_W, tile, NEG_INF)
                            elif s1 < 0:
                                tile = jnp.where(lane < GRID_W, tile, NEG_INF)
                        r0 = h * nq + a * GRID_W
                        tab_ref[c, r0:r0 + GRID_W, j * LANES:(j + 1) * LANES] = tile


def _na_blocks(q_ref, k_ref, v_ref, tab_ref, o_ref):
    S = q_ref.shape[2]
    rows = S // GRID_W
    n_blk = rows // _NQ_ROWS
    nq = _NQ_ROWS * GRID_W
    nk = _NK_ROWS * GRID_W

    def body(i, carry):
        q0 = pl.multiple_of(i * nq, nq)
        kr0 = jnp.clip(_NQ_ROWS * i - NA_WIN_ROWS // 2, 0, rows - _NK_ROWS)
        k0 = pl.multiple_of(kr0 * GRID_W, GRID_W)
        case = jnp.where(i == 0, 0, jnp.where(i == n_blk - 1, 2, 1))
        m, l, pv = _attend(q_ref[0, 0, pl.ds(q0, nq), :], k_ref[0, 0, pl.ds(k0, nk), :],
                           v_ref[0, 0, pl.ds(k0, nk), :], tab_ref[case], sums_on_mxu=True)
        o = pv / l
        lane = lax.broadcasted_iota(jnp.int32, (nq, LANES), 1)
        o_ref[0, pl.ds(q0, nq), :] = jnp.where(lane < HEAD_DIM, o[:nq], o[nq:]).astype(BF16)
        return carry

    lax.fori_loop(0, n_blk, body, 0, unroll=8)


def _attention_kernel(qd_ref, kd_ref, vd_ref, t12_ref, t3_ref, qn_ref, kn_ref, vn_ref, cb_ref,
                      od_ref, on_ref, tab_ref, *dilated_scratch):
    _na_expand_table(cb_ref, tab_ref, qn_ref.shape[2] // GRID_W)
    _dilated_kernel(qd_ref, kd_ref, vd_ref, t12_ref, t3_ref, od_ref, *dilated_scratch)
    _na_blocks(qn_ref, kn_ref, vn_ref, tab_ref, on_ref)


def _attention(qkv, t12, t3, col_bias):
    B, _, S, _ = qkv.shape
    f32_buf = pltpu.VMEM((S, LANES), F32)
    bf16_buf = pltpu.VMEM((S, LANES), BF16)
    slab = lambda off: pl.BlockSpec((1, 1, S, LANES), lambda p, b: (b, off + p, 0, 0))
    nq, nk = _NQ_ROWS * GRID_W, _NK_ROWS * GRID_W
    out = pl.BlockSpec((1, S, LANES), lambda p, b: (b, 0, p))
    return pl.pallas_call(
        _attention_kernel,
        grid=(N_PAIRS, B),
        in_specs=[
            slab(0), slab(N_PAIRS), slab(2 * N_PAIRS),
            pl.BlockSpec((1, 2, 3, 2 * _DQ, _DW), lambda p, b: (p, 0, 0, 0, 0)),
            pl.BlockSpec((1, 2 * _DQ, _DQ), lambda p, b: (p, 0, 0)),
            slab(3 * N_PAIRS), slab(4 * N_PAIRS), slab(5 * N_PAIRS),
            pl.BlockSpec((2,) + col_bias.shape[1:], lambda p, b: (p, 0, 0, 0)),
        ],
        out_specs=[out, out],
        out_shape=[jax.ShapeDtypeStruct((B, S, WIDTH), BF16)] * 2,
        scratch_shapes=([pltpu.VMEM((3, 2 * nq, nk), F32)]
                        + [f32_buf] * 2 + [bf16_buf] * 6 + [f32_buf] * 10),
        compiler_params=pltpu.CompilerParams(
            dimension_semantics=("arbitrary", "arbitrary"), vmem_limit_bytes=VMEM_LIMIT),
        name="attention",
    )(qkv, qkv, qkv, t12, t3, qkv, qkv, qkv, col_bias)


_GROUP_LANE0 = N_EXPERTS
_ROUTE_GROUP_LANE = EXPERTS_PER_GROUP


def _route(logits):
    lane_i = lax.broadcasted_iota(jnp.int32, logits.shape, 1)
    lane = lane_i.astype(F32)
    big = float(LANES)
    is_group = (lane_i >= _GROUP_LANE0) & (lane_i < _GROUP_LANE0 + N_GROUPS)
    gl = jnp.where(is_group, logits, NEG_INF)
    gmax = jnp.max(gl, axis=-1, keepdims=True)
    g_idx = jnp.min(jnp.where(is_group & (gl == gmax), lane, big), axis=-1, keepdims=True) - _GROUP_LANE0
    g_weight = 1.0 / jnp.sum(jnp.where(is_group, jnp.exp(gl - gmax), 0.0), axis=-1, keepdims=True)
    in_group = (lane_i < N_EXPERTS) & ((lane_i // EXPERTS_PER_GROUP).astype(F32) == g_idx)
    el = jnp.where(in_group, logits, NEG_INF)
    v1 = jnp.max(el, axis=-1, keepdims=True)
    i1 = jnp.min(jnp.where(in_group & (el == v1), lane, big), axis=-1, keepdims=True)
    rest = in_group & (lane != i1)
    el2 = jnp.where(rest, logits, NEG_INF)
    v2 = jnp.max(el2, axis=-1, keepdims=True)
    i2 = jnp.min(jnp.where(rest & (el2 == v2), lane, big), axis=-1, keepdims=True)
    e2 = jnp.exp(v2 - v1)
    w1 = g_weight / (1.0 + e2)
    w2 = g_weight * e2 / (1.0 + e2)
    base = g_idx * EXPERTS_PER_GROUP
    return jnp.where(lane == i1 - base, w1,
                     jnp.where(lane == i2 - base, w2,
                               jnp.where(lane_i == _ROUTE_GROUP_LANE, g_idx, 0.0)))


def _outproj_kernel(x_ref, yd_ref, yn_ref, gd_ref, gn_ref, wo_ref, gf_ref, wr_ref,
                    br_ref, h_ref, hn_ref, comb_ref):
    yd = _rms(yd_ref[...].astype(F32), gd_ref[...]).astype(BF16)
    yn = _rms(yn_ref[...].astype(F32), gn_ref[...]).astype(BF16)
    h = x_ref[...] + jnp.dot(jnp.concatenate([yd, yn], axis=-1), wo_ref[...], preferred_element_type=F32)
    h_ref[...] = h
    hn = _rms(h, gf_ref[...]).astype(BF16)
    hn_ref[...] = hn
    logits = jnp.dot(hn, wr_ref[...], preferred_element_type=F32) + br_ref[...]
    comb_ref[...] = _route(logits)


def _outproj(x2, yd2, yn2, gd, gn, wo, gf, wr, br, tm):
    N, D = x2.shape
    row = lambda w: pl.BlockSpec((tm, w), lambda i: (i, 0))
    full = lambda a, b: pl.BlockSpec((a, b), lambda i: (0, 0))
    return pl.pallas_call(
        _outproj_kernel,
        grid=(N // tm,),
        in_specs=[row(D), row(WIDTH), row(WIDTH), full(1, WIDTH), full(1, WIDTH), full(2 * WIDTH, D),
                  full(1, D), full(D, LANES), full(1, LANES)],
        out_specs=[row(D), row(D), row(LANES)],
        out_shape=[jax.ShapeDtypeStruct((N, D), F32), jax.ShapeDtypeStruct((N, D), BF16),
                   jax.ShapeDtypeStruct((N, LANES), F32)],
        compiler_params=pltpu.CompilerParams(
            dimension_semantics=("arbitrary",), vmem_limit_bytes=VMEM_LIMIT),
        name="outproj_route",
    )(x2, yd2, yn2, gd, gn, wo, gf, wr, br)


_MOE_TILE = 1024
_MOE_CHUNK = 144
_MOE_NCHUNK = _MOE_TILE // _MOE_CHUNK + N_GROUPS
_GROUP_WIDTH = EXPERTS_PER_GROUP * D_EXPERT
_ROUTE_PIECE = 8
_MOE_VMEM_LIMIT = 58 * 1024 * 1024


def _moe_kernel(hn_ref, route_ref, h_ref, wg_ref, wu_ref, wd_ref, gfin_ref, y_ref, ys_ref, xcat_ref):
    T, C = _MOE_TILE, _MOE_CHUNK
    route = route_ref[...]
    lane = lax.broadcasted_iota(jnp.int32, (T, LANES), 1)
    gid = jnp.sum(jnp.where(lane == _ROUTE_GROUP_LANE, route, 0.0), axis=-1, keepdims=True)
    onehot = jnp.where((lane < N_GROUPS) & (lane.astype(F32) == gid), 1.0, 0.0)

    before = (lax.broadcasted_iota(jnp.int32, (LANES, LANES), 1)
              < lax.broadcasted_iota(jnp.int32, (LANES, LANES), 0)).astype(BF16)
    count = jnp.zeros((1, LANES), F32)
    ranks = []
    for blk in range(T // LANES):
        oh = onehot[blk * LANES:(blk + 1) * LANES]
        ranks.append(jnp.dot(before, oh.astype(BF16), preferred_element_type=F32) + count)
        count = count + jnp.sum(oh, axis=0, keepdims=True)
    rank = jnp.concatenate(ranks, axis=0)
    nchunk = jnp.floor((count + (C - 1)) * (1.0 / C)).astype(jnp.int32)
    off1 = nchunk[0, 0]
    off2 = off1 + nchunk[0, 1]
    off3 = off2 + nchunk[0, 2]
    n_used = off3 + nchunk[0, 3]
    start = jnp.where(lane == 1, off1, jnp.where(lane == 2, off2, jnp.where(lane == 3, off3, 0)))
    pos = jnp.sum(onehot * (rank + (start * C).astype(F32)), axis=-1, keepdims=True)
    pos_i = pos.astype(jnp.int32)
    pos_row = jnp.transpose(jnp.broadcast_to(pos, (T, LANES)))[0:1, :].astype(jnp.int32)

    r_hi = route.astype(BF16).astype(F32)
    r_mid = (route - r_hi).astype(BF16).astype(F32)
    r_lo = (route - r_hi - r_mid).astype(BF16).astype(F32)
    packed = r_hi + pltpu.roll(r_mid, _ROUTE_PIECE, axis=1) + pltpu.roll(r_lo, 2 * _ROUTE_PIECE, axis=1)
    xcat_ref[:, :hn_ref.shape[1]] = hn_ref[...]
    xcat_ref[:, hn_ref.shape[1]:] = packed.astype(BF16)

    ys_ref[...] = jnp.zeros_like(ys_ref)

    def chunk_body(c, carry):
        g = ((c >= off1).astype(jnp.int32) + (c >= off2).astype(jnp.int32)
             + (c >= off3).astype(jnp.int32))
        row0 = pl.multiple_of(c * C, 16)
        sel = (pos_row == row0 + lax.broadcasted_iota(jnp.int32, (C, T), 0)).astype(BF16)
        xr = jnp.dot(sel, xcat_ref[...], preferred_element_type=F32)
        xs = xr[:, :hn_ref.shape[1]].astype(BF16)
        r3 = xr[:, hn_ref.shape[1]:]
        r = (r3 + pltpu.roll(r3, LANES - _ROUTE_PIECE, axis=1)
             + pltpu.roll(r3, LANES - 2 * _ROUTE_PIECE, axis=1))
        clane = lax.broadcasted_iota(jnp.int32, (C, LANES), 1)
        parts = []
        for j in range(EXPERTS_PER_GROUP):
            e = g * EXPERTS_PER_GROUP + j
            gate = jnp.dot(xs, wg_ref[e], preferred_element_type=F32)
            up = jnp.dot(xs, wu_ref[e], preferred_element_type=F32)
            wj = jnp.sum(jnp.where(clane == j, r, 0.0), axis=-1, keepdims=True)
            parts.append((gate / (1.0 + jnp.exp(-gate))) * up * wj)
        act = jnp.concatenate(parts, axis=-1).astype(BF16)
        ys_ref[pl.ds(row0, C), :] = jnp.dot(act, wd_ref[g], preferred_element_type=F32).astype(BF16)
        return carry

    lax.fori_loop(0, n_used, chunk_body, 0)

    n_main = N_GROUPS * (-(-(T // N_GROUPS) // C))
    n_tail = _MOE_NCHUNK - n_main
    back = (lax.broadcasted_iota(jnp.int32, (T, n_main * C), 1) == pos_i).astype(BF16)
    y_ref[...] = h_ref[...] + jnp.dot(back, ys_ref[:n_main * C, :], preferred_element_type=F32)

    @pl.when(n_used > n_main)
    def _():
        tail = (lax.broadcasted_iota(jnp.int32, (T, n_tail * C), 1) == pos_i - n_main * C).astype(BF16)
        y_ref[...] += jnp.dot(tail, ys_ref[n_main * C:, :], preferred_element_type=F32)

    y_ref[...] = _rms(y_ref[...], gfin_ref[...])


def _moe(hn, route, h, wg, wu, wd, gfin):
    N, D = hn.shape
    T = _MOE_TILE
    row = lambda w: pl.BlockSpec((T, w), lambda i: (i, 0))
    whole = lambda a: pl.BlockSpec(a.shape, lambda i: (0,) * a.ndim)
    return pl.pallas_call(
        _moe_kernel,
        grid=(N // T,),
        in_specs=[row(D), row(LANES), row(D),
                  whole(wg), whole(wu), whole(wd), whole(gfin)],
        out_specs=row(D),
        out_shape=jax.ShapeDtypeStruct((N, D), F32),
        scratch_shapes=[pltpu.VMEM((_MOE_NCHUNK * _MOE_CHUNK, D), BF16), pltpu.VMEM((T, D + LANES), BF16)],
        compiler_params=pltpu.CompilerParams(
            dimension_semantics=("arbitrary",), vmem_limit_bytes=_MOE_VMEM_LIMIT),
        name="moe_grouped",
    )(hn, route, h, wg, wu, wd, gfin)


def kernel(x, norm_mix_g, w_in, rpb, g_out_dil, g_out_na, w_out, norm_ffn_g, w_group, b_group,
           w_router, b_router, w_gate, w_up, w_down, norm_final_g):
    B, S, D = x.shape
    N = B * S
    depth = w_in.shape[0]
    assert depth == 1 and D == D_MODEL and S % (16 * _DQ) == 0

    t12, t3 = _dilated_tables()

    layer = 0
    col_scale = np.ones((6, WIDTH), np.float32)
    col_scale[0] = col_scale[3] = LOG2E * HEAD_DIM ** -0.5
    w_in_b = (w_in[layer] * col_scale.reshape(1, -1)).astype(BF16)

    qkv = _inproj(x, norm_mix_g[layer].reshape(1, D), w_in_b, tm=512)

    y_dil, y_na = _attention(qkv, jnp.asarray(t12), jnp.asarray(t3), _na_column_bias(rpb[layer]))

    n_route = N_EXPERTS + N_GROUPS
    w_r = jnp.concatenate([w_router[layer], w_group[layer], jnp.zeros((D, LANES - n_route), F32)], axis=1)
    b_r = jnp.concatenate([b_router[layer], b_group[layer], jnp.zeros((LANES - n_route,), F32)]).reshape(1, LANES)

    h, hn, route = _outproj(
        x.reshape(N, D), y_dil.reshape(N, WIDTH), y_na.reshape(N, WIDTH),
        g_out_dil[layer].reshape(1, WIDTH), g_out_na[layer].reshape(1, WIDTH),
        w_out[layer].astype(BF16), norm_ffn_g[layer].reshape(1, D), w_r.astype(BF16), b_r, tm=512)

    wd = w_down[layer].astype(BF16).reshape(N_GROUPS, _GROUP_WIDTH, D)
    y = _moe(hn, route, h, w_gate[layer].astype(BF16), w_up[layer].astype(BF16), wd,
             norm_final_g.reshape(1, D))
    return y.reshape(B, S, D)
```

```python
import numpy as np
import jax
import jax.numpy as jnp
from jax import lax
from jax.experimental import pallas as pl
from jax.experimental.pallas import tpu as pltpu

D_MODEL = 1024
HEAD_DIM = 64
N_HEADS = 8
N_PAIRS = N_HEADS // 2
WIDTH = N_HEADS * HEAD_DIM
N_SLABS = 6 * N_PAIRS
DIL_PATTERNS = ((128, 1), (512, 4), (2048, 16))
DIL_RADIUS = 64
GRID_W = 64
NA_WIN_ROWS = 8
NA_WIN_COLS = 16
N_GROUPS = 4
EXPERTS_PER_GROUP = 4
N_EXPERTS = 16
D_EXPERT = 256
RMS_EPS = 1e-6
NEG_INF = -1e30
LOG2E = 1.4426950408889634

LANES = 128
VMEM_LIMIT = 48 * 1024 * 1024

F32 = jnp.float32
BF16 = jnp.bfloat16


def _rms(x, gain):
    return x * lax.rsqrt(jnp.mean(x * x, axis=-1, keepdims=True) + RMS_EPS) * gain


def _inproj_kernel(x_ref, g_ref, w_ref, o_ref):
    xn = _rms(x_ref[0], g_ref[...]).astype(BF16)
    chunk = 4 * LANES
    for c in range(N_SLABS * LANES // chunk):
        acc = jnp.dot(xn, w_ref[:, c * chunk:(c + 1) * chunk], preferred_element_type=F32)
        for j in range(chunk // LANES):
            o_ref[0, c * (chunk // LANES) + j] = acc[:, j * LANES:(j + 1) * LANES].astype(BF16)


def _inproj(x, gain, w_bf16, tm):
    B, S, D = x.shape
    return pl.pallas_call(
        _inproj_kernel,
        grid=(B, S // tm),
        in_specs=[
            pl.BlockSpec((1, tm, D), lambda b, i: (b, i, 0)),
            pl.BlockSpec((1, D), lambda b, i: (0, 0)),
            pl.BlockSpec((D, N_SLABS * LANES), lambda b, i: (0, 0)),
        ],
        out_specs=pl.BlockSpec((1, N_SLABS, tm, LANES), lambda b, i: (b, 0, i, 0)),
        out_shape=jax.ShapeDtypeStruct((B, N_SLABS, S, LANES), BF16),
        compiler_params=pltpu.CompilerParams(
            dimension_semantics=("arbitrary", "arbitrary"), vmem_limit_bytes=VMEM_LIMIT),
        name="inproj",
    )(x, gain, w_bf16)


def _attend(qb, kw, vw, bias, sums_on_mxu):
    lane = lax.broadcasted_iota(jnp.int32, qb.shape, 1)
    zero = jnp.zeros_like(qb)
    qq = jnp.concatenate([jnp.where(lane < HEAD_DIM, qb, zero),
                          jnp.where(lane >= HEAD_DIM, qb, zero)], axis=0)
    s = lax.dot_general(qq, kw, (((1,), (1,)), ((), ())), preferred_element_type=F32) + bias
    m = jnp.max(s, axis=-1, keepdims=True)
    p = jnp.exp2(s - m)
    if not sums_on_mxu:
        l = jnp.sum(p, axis=-1, keepdims=True)
        return m, l, jnp.dot(p.astype(BF16), vw, preferred_element_type=F32)
    pv = jnp.dot(p.astype(BF16), jnp.concatenate([vw, jnp.ones_like(vw)], axis=1),
                 preferred_element_type=F32)
    return m, pv[:, LANES:], pv[:, :LANES]


def _merge_heads(top, bottom, q):
    lane = lax.broadcasted_iota(jnp.int32, (q, LANES), 1)
    return jnp.where(lane < HEAD_DIM, jnp.broadcast_to(top, (q, LANES)),
                     jnp.broadcast_to(bottom, (q, LANES)))


_DQ = 128
_DW = 256
_UNROLL = 16


def _dilated_tables():
    slopes = 2.0 ** (-(np.arange(N_HEADS) + 1.0))
    q = np.arange(_DQ)[:, None]
    t12 = np.zeros((N_PAIRS, 2, 3, 2 * _DQ, _DW), np.float32)
    k = np.arange(_DW)[None, :]
    for pat, dil in enumerate((1, 4)):
        for case, off in enumerate((0, _DW // 4, _DW // 2)):
            delta = np.abs(k - (q + off))
            for h in range(N_HEADS):
                tab = np.where(delta <= DIL_RADIUS, -LOG2E * slopes[h] * dil * delta, NEG_INF)
                t12[h // 2, pat, case, (h % 2) * _DQ:(h % 2 + 1) * _DQ] = tab
    t3 = np.zeros((N_PAIRS, 2 * _DQ, _DQ), np.float32)
    delta = np.abs(np.arange(_DQ)[None, :] - q)
    for h in range(N_HEADS):
        t3[h // 2, (h % 2) * _DQ:(h % 2 + 1) * _DQ] = np.where(
            delta <= DIL_RADIUS, -LOG2E * slopes[h] * 16 * delta, NEG_INF)
    return t12, t3


def _dilated_kernel(q_ref, k_ref, v_ref, t12_ref, t3_ref, o_ref,
                    tmp, tmp4, q4, k4, v4, q16, k16, v16,
                    m1, l1, a1, m2, l2, a2, m3, l3, a3, onat):
    S = tmp.shape[0]
    L4, L16 = S // 4, S // 16

    for src, d4, d16 in ((q_ref, q4, q16), (k_ref, k4, k16), (v_ref, v4, v16)):
        tmp[...] = src[0, 0].astype(F32)
        for r in range(4):
            sub = tmp[pl.ds(r, L4, stride=4), :]
            tmp4[r * L4:(r + 1) * L4, :] = sub
            d4[r * L4:(r + 1) * L4, :] = sub.astype(BF16)
        for r16 in range(16):
            r4, c4 = r16 % 4, r16 // 4
            d16[r16 * L16:(r16 + 1) * L16, :] = tmp4[pl.ds(r4 * L4 + c4, L16, stride=4), :].astype(BF16)

    def block(qb, kw, vw, bias, m_ref, l_ref, a_ref, row):
        m, l, pv = _attend(qb, kw, vw, bias, sums_on_mxu=False)
        m_ref[pl.ds(row, _DQ), :] = _merge_heads(m[:_DQ], m[_DQ:], _DQ)
        l_ref[pl.ds(row, _DQ), :] = _merge_heads(l[:_DQ], l[_DQ:], _DQ)
        a_ref[pl.ds(row, _DQ), :] = _merge_heads(pv[:_DQ], pv[_DQ:], _DQ)

    def case_of(blk, n_blk):
        return jnp.where(blk == 0, 0, jnp.where(blk == n_blk - 1, 2, 1))

    n1 = S // _DQ

    def p1_body(blk, carry):
        t0 = pl.multiple_of(blk * _DQ, _DQ)
        ws = pl.multiple_of(jnp.clip(t0 - DIL_RADIUS, 0, S - _DW), DIL_RADIUS)
        block(q_ref[0, 0, pl.ds(t0, _DQ), :], k_ref[0, 0, pl.ds(ws, _DW), :],
              v_ref[0, 0, pl.ds(ws, _DW), :], t12_ref[0, 0, case_of(blk, n1)], m1, l1, a1, t0)
        return carry

    lax.fori_loop(0, n1, p1_body, 0, unroll=_UNROLL)

    n2 = L4 // _DQ

    def p2_body(j, carry):
        r = j // n2
        blk = j % n2
        l0 = blk * _DQ
        ws = jnp.clip(l0 - DIL_RADIUS, 0, L4 - _DW)
        row = pl.multiple_of(r * L4 + l0, _DQ)
        krow = pl.multiple_of(r * L4 + ws, DIL_RADIUS)
        block(q4[pl.ds(row, _DQ), :], k4[pl.ds(krow, _DW), :], v4[pl.ds(krow, _DW), :],
              t12_ref[0, 1, case_of(blk, n2)], m2, l2, a2, row)
        return carry

    lax.fori_loop(0, 4 * n2, p2_body, 0, unroll=_UNROLL)

    def p3_body(r, carry):
        row = pl.multiple_of(r * L16, L16)
        block(q16[pl.ds(row, L16), :], k16[pl.ds(row, L16), :], v16[pl.ds(row, L16), :],
              t3_ref[0], m3, l3, a3, row)
        return carry

    lax.fori_loop(0, 16, p3_body, 0, unroll=_UNROLL)

    for r16 in range(16):
        r4, c4 = r16 % 4, r16 // 4
        via4 = pl.ds(r4 * L4 + c4, L16, stride=4)
        via16 = pl.ds(r16 * L16, L16)
        mb, mc = m2[via4, :], m3[via16, :]
        mx = jnp.maximum(mb, mc)
        wb, wc = jnp.exp2(mb - mx), jnp.exp2(mc - mx)
        l2[via4, :] = wb * l2[via4, :] + wc * l3[via16, :]
        a2[via4, :] = wb * a2[via4, :] + wc * a3[via16, :]
        m2[via4, :] = mx
    for r4 in range(4):
        for part in range(L4 // _DQ):
            nat = pl.ds(r4 + 4 * _DQ * part, _DQ, stride=4)
            via4 = pl.ds(r4 * L4 + _DQ * part, _DQ)
            ma, mb = m1[nat, :], m2[via4, :]
            mx = jnp.maximum(ma, mb)
            wa, wb = jnp.exp2(ma - mx), jnp.exp2(mb - mx)
            den = wa * l1[nat, :] + wb * l2[via4, :]
            num = wa * a1[nat, :] + wb * a2[via4, :]
            onat[nat, :] = num / den
    o_ref[0] = onat[...].astype(BF16)


_NQ_ROWS = 4
_NK_ROWS = 12


def _na_row_select(rows):
    n_blk = rows // _NQ_ROWS
    sel = np.full((n_blk, _NQ_ROWS, _NK_ROWS), -1, np.int64)
    for i in range(n_blk):
        kr0 = min(max(_NQ_ROWS * i - NA_WIN_ROWS // 2, 0), rows - _NK_ROWS)
        for a in range(_NQ_ROWS):
            qr = _NQ_ROWS * i + a
            rs = min(max(qr - NA_WIN_ROWS // 2, 0), rows - NA_WIN_ROWS)
            for b in range(_NK_ROWS):
                kr = kr0 + b
                if rs <= kr < rs + NA_WIN_ROWS:
                    sel[i, a, b] = kr - qr + NA_WIN_ROWS - 1
    for i in range(2, n_blk - 1):
        assert np.array_equal(sel[1], sel[i])
    return sel[[0, 1, n_blk - 1]]


_NA_QCOLS = 16
_NA_KCOLS = 32
_NA_COLUMN_SETS = tuple(
    (((8 + 16 * j, 16),), ((16 * j, 32),), 0) for j in range(3)
) + ((((0, 8), (GRID_W - 8, 8)), ((0, 16), (GRID_W - 16, 16)), 1),)


def _na_bias_tables(rpb, rows):
    n_dr, n_dc = 2 * NA_WIN_ROWS - 1, 2 * NA_WIN_COLS - 1
    sel = _na_row_select(rows)
    kinds = []
    for kind in (0, 1):
        sets = [cs for cs in _NA_COLUMN_SETS if cs[2] == kind]
        layouts = []
        for q_runs, k_runs, _ in sets:
            qc = np.concatenate([np.arange(c0, c0 + n) for c0, n in q_runs])[:, None]
            kc = np.concatenate([np.arange(c0, c0 + n) for c0, n in k_runs])[None, :]
            start = np.clip(qc - NA_WIN_COLS // 2, 0, GRID_W - NA_WIN_COLS)
            col_ok = (kc >= start) & (kc < start + NA_WIN_COLS)
            assert (col_ok.sum(axis=1) == NA_WIN_COLS).all()
            layouts.append((col_ok, np.clip(kc - qc + NA_WIN_COLS - 1, 0, n_dc - 1)))
        col_ok, dc = layouts[0]
        assert all(np.array_equal(col_ok, o) and np.array_equal(dc, d) for o, d in layouts)
        onehot = (dc.reshape(1, -1) == np.arange(n_dc)[:, None]).astype(np.float32)
        t = jnp.dot(rpb.astype(F32).reshape(N_HEADS * n_dr, n_dc), onehot, precision=lax.Precision.HIGHEST)
        t = jnp.where(col_ok[None, None], LOG2E * t.reshape(N_HEADS, n_dr, _NA_QCOLS, _NA_KCOLS), NEG_INF)
        neg = jnp.full((N_HEADS, _NA_QCOLS, _NA_KCOLS), NEG_INF, F32)
        cases = []
        for c in range(sel.shape[0]):
            per_a = [jnp.stack([neg if s < 0 else t[:, s] for s in sel[c, a]], axis=2)
                     for a in range(_NQ_ROWS)]
            cases.append(jnp.stack(per_a, axis=1))
        tab = jnp.stack(cases, axis=1).reshape(N_PAIRS, 2, sel.shape[0], _NQ_ROWS * _NA_QCOLS,
                                               _NK_ROWS * _NA_KCOLS)
        kinds.append(tab.transpose(0, 2, 1, 3, 4).reshape(
            N_PAIRS, sel.shape[0], 2 * _NQ_ROWS * _NA_QCOLS, _NK_ROWS * _NA_KCOLS))
    return jnp.stack(kinds, axis=1)


def _na_blocks(q_ref, k_ref, v_ref, tab_ref, o_ref, qf, of):
    S = q_ref.shape[2]
    rows = S // GRID_W
    n_blk = rows // _NQ_ROWS
    nq = _NQ_ROWS * _NA_QCOLS
    lane = lax.broadcasted_iota(jnp.int32, (nq, LANES), 1)
    qf[...] = q_ref[0, 0].astype(F32)
    for i in range(n_blk):
        kr0 = min(max(_NQ_ROWS * i - NA_WIN_ROWS // 2, 0), rows - _NK_ROWS)
        case = 0 if i == 0 else (2 if i == n_blk - 1 else 1)
        for q_runs, k_runs, kind in _NA_COLUMN_SETS:
            q_rows = [((_NQ_ROWS * i + a) * GRID_W + c0, n) for a in range(_NQ_ROWS) for c0, n in q_runs]
            k_rows = [((kr0 + b) * GRID_W + c0, n) for b in range(_NK_ROWS) for c0, n in k_runs]
            qb = jnp.concatenate([qf[r0:r0 + n, :] for r0, n in q_rows], axis=0).astype(BF16)
            kw = jnp.concatenate([k_ref[0, 0, r0:r0 + n, :] for r0, n in k_rows], axis=0)
            vw = jnp.concatenate([v_ref[0, 0, r0:r0 + n, :] for r0, n in k_rows], axis=0)
            m, l, pv = _attend(qb, kw, vw, tab_ref[0, kind, case], sums_on_mxu=True)
            o = pv / l
            o = jnp.where(lane < HEAD_DIM, o[:nq], o[nq:])
            at = 0
            for r0, n in q_rows:
                of[r0:r0 + n, :] = o[at:at + n]
                at += n
    o_ref[0] = of[...].astype(BF16)


def _attention_kernel(qd_ref, kd_ref, vd_ref, t12_ref, t3_ref, qn_ref, kn_ref, vn_ref, tna_ref,
                      od_ref, on_ref, qf, of, *dilated_scratch):
    _dilated_kernel(qd_ref, kd_ref, vd_ref, t12_ref, t3_ref, od_ref, *dilated_scratch)
    _na_blocks(qn_ref, kn_ref, vn_ref, tna_ref, on_ref, qf, of)


def _attention(qkv, t12, t3, tna):
    B, _, S, _ = qkv.shape
    f32_buf = pltpu.VMEM((S, LANES), F32)
    bf16_buf = pltpu.VMEM((S, LANES), BF16)
    slab = lambda off: pl.BlockSpec((1, 1, S, LANES), lambda p, b: (b, off + p, 0, 0))
    per_pair = lambda a: pl.BlockSpec((1,) + a.shape[1:], lambda p, b: (p,) + (0,) * (a.ndim - 1))
    out = pl.BlockSpec((1, S, LANES), lambda p, b: (b, 0, p))
    return pl.pallas_call(
        _attention_kernel,
        grid=(N_PAIRS, B),
        in_specs=[slab(0), slab(N_PAIRS), slab(2 * N_PAIRS), per_pair(t12), per_pair(t3),
                  slab(3 * N_PAIRS), slab(4 * N_PAIRS), slab(5 * N_PAIRS), per_pair(tna)],
        out_specs=[out, out],
        out_shape=[jax.ShapeDtypeStruct((B, S, WIDTH), BF16)] * 2,
        scratch_shapes=[f32_buf] * 2 + [f32_buf] * 2 + [bf16_buf] * 6 + [f32_buf] * 10,
        compiler_params=pltpu.CompilerParams(
            dimension_semantics=("arbitrary", "arbitrary"), vmem_limit_bytes=VMEM_LIMIT),
        name="attention",
    )(qkv, qkv, qkv, t12, t3, qkv, qkv, qkv, tna)


_GROUP_LANE0 = N_EXPERTS
_ROUTE_GROUP_LANE = EXPERTS_PER_GROUP


def _route(logits):
    lane_i = lax.broadcasted_iota(jnp.int32, logits.shape, 1)
    lane = lane_i.astype(F32)
    big = float(LANES)
    is_group = (lane_i >= _GROUP_LANE0) & (lane_i < _GROUP_LANE0 + N_GROUPS)
    gl = jnp.where(is_group, logits, NEG_INF)
    gmax = jnp.max(gl, axis=-1, keepdims=True)
    g_idx = jnp.min(jnp.where(is_group & (gl == gmax), lane, big), axis=-1, keepdims=True) - _GROUP_LANE0
    g_weight = 1.0 / jnp.sum(jnp.where(is_group, jnp.exp(gl - gmax), 0.0), axis=-1, keepdims=True)
    in_group = (lane_i < N_EXPERTS) & ((lane_i // EXPERTS_PER_GROUP).astype(F32) == g_idx)
    el = jnp.where(in_group, logits, NEG_INF)
    v1 = jnp.max(el, axis=-1, keepdims=True)
    i1 = jnp.min(jnp.where(in_group & (el == v1), lane, big), axis=-1, keepdims=True)
    rest = in_group & (lane != i1)
    el2 = jnp.where(rest, logits, NEG_INF)
    v2 = jnp.max(el2, axis=-1, keepdims=True)
    i2 = jnp.min(jnp.where(rest & (el2 == v2), lane, big), axis=-1, keepdims=True)
    e2 = jnp.exp(v2 - v1)
    w1 = g_weight / (1.0 + e2)
    w2 = g_weight * e2 / (1.0 + e2)
    base = g_idx * EXPERTS_PER_GROUP
    return jnp.where(lane == i1 - base, w1,
                     jnp.where(lane == i2 - base, w2,
                               jnp.where(lane_i == _ROUTE_GROUP_LANE, g_idx, 0.0)))


def _outproj_kernel(x_ref, yd_ref, yn_ref, gd_ref, gn_ref, wo_ref, gf_ref, wr_ref,
                    br_ref, h_ref, hn_ref, comb_ref):
    yd = _rms(yd_ref[...].astype(F32), gd_ref[...]).astype(BF16)
    yn = _rms(yn_ref[...].astype(F32), gn_ref[...]).astype(BF16)
    h = x_ref[...] + jnp.dot(jnp.concatenate([yd, yn], axis=-1), wo_ref[...], preferred_element_type=F32)
    h_ref[...] = h
    hn = _rms(h, gf_ref[...]).astype(BF16)
    hn_ref[...] = hn
    logits = jnp.dot(hn, wr_ref[...], preferred_element_type=F32) + br_ref[...]
    comb_ref[...] = _route(logits)


def _outproj(x2, yd2, yn2, gd, gn, wo, gf, wr, br, tm):
    N, D = x2.shape
    row = lambda w: pl.BlockSpec((tm, w), lambda i: (i, 0))
    full = lambda a, b: pl.BlockSpec((a, b), lambda i: (0, 0))
    return pl.pallas_call(
        _outproj_kernel,
        grid=(N // tm,),
        in_specs=[row(D), row(WIDTH), row(WIDTH), full(1, WIDTH), full(1, WIDTH), full(2 * WIDTH, D),
                  full(1, D), full(D, LANES), full(1, LANES)],
        out_specs=[row(D), row(D), row(LANES)],
        out_shape=[jax.ShapeDtypeStruct((N, D), F32), jax.ShapeDtypeStruct((N, D), BF16),
                   jax.ShapeDtypeStruct((N, LANES), F32)],
        compiler_params=pltpu.CompilerParams(
            dimension_semantics=("arbitrary",), vmem_limit_bytes=VMEM_LIMIT),
        name="outproj_route",
    )(x2, yd2, yn2, gd, gn, wo, gf, wr, br)


_MOE_TILE = 1024
_MOE_CHUNK = 144
_MOE_NCHUNK = _MOE_TILE // _MOE_CHUNK + N_GROUPS
_GROUP_WIDTH = EXPERTS_PER_GROUP * D_EXPERT
_ROUTE_PIECE = 8
_MOE_VMEM_LIMIT = 58 * 1024 * 1024


def _moe_kernel(hn_ref, route_ref, h_ref, wg_ref, wu_ref, wd_ref, gfin_ref, y_ref, ys_ref, xcat_ref):
    T, C = _MOE_TILE, _MOE_CHUNK
    route = route_ref[...]
    lane = lax.broadcasted_iota(jnp.int32, (T, LANES), 1)
    gid = jnp.sum(jnp.where(lane == _ROUTE_GROUP_LANE, route, 0.0), axis=-1, keepdims=True)
    onehot = jnp.where((lane < N_GROUPS) & (lane.astype(F32) == gid), 1.0, 0.0)

    before = (lax.broadcasted_iota(jnp.int32, (LANES, LANES), 1)
              < lax.broadcasted_iota(jnp.int32, (LANES, LANES), 0)).astype(BF16)
    count = jnp.zeros((1, LANES), F32)
    ranks = []
    for blk in range(T // LANES):
        oh = onehot[blk * LANES:(blk + 1) * LANES]
        ranks.append(jnp.dot(before, oh.astype(BF16), preferred_element_type=F32) + count)
        count = count + jnp.sum(oh, axis=0, keepdims=True)
    rank = jnp.concatenate(ranks, axis=0)
    nchunk = jnp.floor((count + (C - 1)) * (1.0 / C)).astype(jnp.int32)
    off1 = nchunk[0, 0]
    off2 = off1 + nchunk[0, 1]
    off3 = off2 + nchunk[0, 2]
    n_used = off3 + nchunk[0, 3]
    start = jnp.where(lane == 1, off1, jnp.where(lane == 2, off2, jnp.where(lane == 3, off3, 0)))
    pos = jnp.sum(onehot * (rank + (start * C).astype(F32)), axis=-1, keepdims=True)
    pos_i = pos.astype(jnp.int32)
    pos_row = jnp.transpose(jnp.broadcast_to(pos, (T, LANES)))[0:1, :].astype(jnp.int32)

    r_hi = route.astype(BF16).astype(F32)
    r_mid = (route - r_hi).astype(BF16).astype(F32)
    r_lo = (route - r_hi - r_mid).astype(BF16).astype(F32)
    packed = r_hi + pltpu.roll(r_mid, _ROUTE_PIECE, axis=1) + pltpu.roll(r_lo, 2 * _ROUTE_PIECE, axis=1)
    xcat_ref[:, :hn_ref.shape[1]] = hn_ref[...]
    xcat_ref[:, hn_ref.shape[1]:] = packed.astype(BF16)

    ys_ref[...] = jnp.zeros_like(ys_ref)

    def chunk_body(c, carry):
        g = ((c >= off1).astype(jnp.int32) + (c >= off2).astype(jnp.int32)
             + (c >= off3).astype(jnp.int32))
        row0 = pl.multiple_of(c * C, 16)
        sel = (pos_row == row0 + lax.broadcasted_iota(jnp.int32, (C, T), 0)).astype(BF16)
        xr = jnp.dot(sel, xcat_ref[...], preferred_element_type=F32)
        xs = xr[:, :hn_ref.shape[1]].astype(BF16)
        r3 = xr[:, hn_ref.shape[1]:]
        r = (r3 + pltpu.roll(r3, LANES - _ROUTE_PIECE, axis=1)
             + pltpu.roll(r3, LANES - 2 * _ROUTE_PIECE, axis=1))
        clane = lax.broadcasted_iota(jnp.int32, (C, LANES), 1)
        parts = []
        for j in range(EXPERTS_PER_GROUP):
            e = g * EXPERTS_PER_GROUP + j
            gate = jnp.dot(xs, wg_ref[e], preferred_element_type=F32)
            up = jnp.dot(xs, wu_ref[e], preferred_element_type=F32)
            wj = jnp.sum(jnp.where(clane == j, r, 0.0), axis=-1, keepdims=True)
            parts.append((gate / (1.0 + jnp.exp(-gate))) * up * wj)
        act = jnp.concatenate(parts, axis=-1).astype(BF16)
        ys_ref[pl.ds(row0, C), :] = jnp.dot(act, wd_ref[g], preferred_element_type=F32).astype(BF16)
        return carry

    lax.fori_loop(0, n_used, chunk_body, 0)

    n_main = N_GROUPS * (-(-(T // N_GROUPS) // C))
    n_tail = _MOE_NCHUNK - n_main
    back = (lax.broadcasted_iota(jnp.int32, (T, n_main * C), 1) == pos_i).astype(BF16)
    y_ref[...] = h_ref[...] + jnp.dot(back, ys_ref[:n_main * C, :], preferred_element_type=F32)

    @pl.when(n_used > n_main)
    def _():
        tail = (lax.broadcasted_iota(jnp.int32, (T, n_tail * C), 1) == pos_i - n_main * C).astype(BF16)
        y_ref[...] += jnp.dot(tail, ys_ref[n_main * C:, :], preferred_element_type=F32)

    y_ref[...] = _rms(y_ref[...], gfin_ref[...])


def _moe(hn, route, h, wg, wu, wd, gfin):
    N, D = hn.shape
    T = _MOE_TILE
    row = lambda w: pl.BlockSpec((T, w), lambda i: (i, 0))
    whole = lambda a: pl.BlockSpec(a.shape, lambda i: (0,) * a.ndim)
    return pl.pallas_call(
        _moe_kernel,
        grid=(N // T,),
        in_specs=[row(D), row(LANES), row(D),
                  whole(wg), whole(wu), whole(wd), whole(gfin)],
        out_specs=row(D),
        out_shape=jax.ShapeDtypeStruct((N, D), F32),
        scratch_shapes=[pltpu.VMEM((_MOE_NCHUNK * _MOE_CHUNK, D), BF16), pltpu.VMEM((T, D + LANES), BF16)],
        compiler_params=pltpu.CompilerParams(
            dimension_semantics=("arbitrary",), vmem_limit_bytes=_MOE_VMEM_LIMIT),
        name="moe_grouped",
    )(hn, route, h, wg, wu, wd, gfin)


def kernel(x, norm_mix_g, w_in, rpb, g_out_dil, g_out_na, w_out, norm_ffn_g, w_group, b_group,
           w_router, b_router, w_gate, w_up, w_down, norm_final_g):
    B, S, D = x.shape
    N = B * S
    depth = w_in.shape[0]
    assert depth == 1 and D == D_MODEL and S % (16 * _DQ) == 0

    t12, t3 = _dilated_tables()

    layer = 0
    col_scale = np.ones((6, WIDTH), np.float32)
    col_scale[0] = col_scale[3] = LOG2E * HEAD_DIM ** -0.5
    w_in_b = (w_in[layer] * col_scale.reshape(1, -1)).astype(BF16)

    qkv = _inproj(x, norm_mix_g[layer].reshape(1, D), w_in_b, tm=512)

    y_dil, y_na = _attention(qkv, jnp.asarray(t12), jnp.asarray(t3),
                             _na_bias_tables(rpb[layer], S // GRID_W))

    n_route = N_EXPERTS + N_GROUPS
    w_r = jnp.concatenate([w_router[layer], w_group[layer], jnp.zeros((D, LANES - n_route), F32)], axis=1)
    b_r = jnp.concatenate([b_router[layer], b_group[layer], jnp.zeros((LANES - n_route,), F32)]).reshape(1, LANES)

    h, hn, route = _outproj(
        x.reshape(N, D), y_dil.reshape(N, WIDTH), y_na.reshape(N, WIDTH),
        g_out_dil[layer].reshape(1, WIDTH), g_out_na[layer].reshape(1, WIDTH),
        w_out[layer].astype(BF16), norm_ffn_g[layer].reshape(1, D), w_r.astype(BF16), b_r, tm=512)

    wd = w_down[layer].astype(BF16).reshape(N_GROUPS, _GROUP_WIDTH, D)
    y = _moe(hn, route, h, w_gate[layer].astype(BF16), w_up[layer].astype(BF16), wd,
             norm_final_g.reshape(1, D))
    return y.reshape(B, S, D)
```

```python
import numpy as np
import jax
import jax.numpy as jnp
from jax import lax
from jax.experimental import pallas as pl
from jax.experimental.pallas import tpu as pltpu

D_MODEL = 1024
HEAD_DIM = 64
N_HEADS = 8
N_PAIRS = N_HEADS // 2
WIDTH = N_HEADS * HEAD_DIM
N_SLABS = 6 * N_PAIRS
DIL_PATTERNS = ((128, 1), (512, 4), (2048, 16))
DIL_RADIUS = 64
GRID_W = 64
NA_WIN_ROWS = 8
NA_WIN_COLS = 16
N_GROUPS = 4
EXPERTS_PER_GROUP = 4
N_EXPERTS = 16
D_EXPERT = 256
RMS_EPS = 1e-6
NEG_INF = -1e30
LOG2E = 1.4426950408889634

LANES = 128
VMEM_LIMIT = 48 * 1024 * 1024

F32 = jnp.float32
BF16 = jnp.bfloat16


def _rms(x, gain):
    return x * lax.rsqrt(jnp.mean(x * x, axis=-1, keepdims=True) + RMS_EPS) * gain


def _inproj_kernel(x_ref, g_ref, w_ref, o_ref):
    xn = _rms(x_ref[0], g_ref[...]).astype(BF16)
    chunk = 4 * LANES
    for c in range(N_SLABS * LANES // chunk):
        acc = jnp.dot(xn, w_ref[:, c * chunk:(c + 1) * chunk], preferred_element_type=F32)
        for j in range(chunk // LANES):
            o_ref[0, c * (chunk // LANES) + j] = acc[:, j * LANES:(j + 1) * LANES].astype(BF16)


def _inproj(x, gain, w_bf16, tm):
    B, S, D = x.shape
    return pl.pallas_call(
        _inproj_kernel,
        grid=(B, S // tm),
        in_specs=[
            pl.BlockSpec((1, tm, D), lambda b, i: (b, i, 0)),
            pl.BlockSpec((1, D), lambda b, i: (0, 0)),
            pl.BlockSpec((D, N_SLABS * LANES), lambda b, i: (0, 0)),
        ],
        out_specs=pl.BlockSpec((1, N_SLABS, tm, LANES), lambda b, i: (b, 0, i, 0)),
        out_shape=jax.ShapeDtypeStruct((B, N_SLABS, S, LANES), BF16),
        compiler_params=pltpu.CompilerParams(
            dimension_semantics=("arbitrary", "arbitrary"), vmem_limit_bytes=VMEM_LIMIT),
        name="inproj",
    )(x, gain, w_bf16)


def _attend(qb, kw, vw, bias, sums_on_mxu):
    lane = lax.broadcasted_iota(jnp.int32, qb.shape, 1)
    zero = jnp.zeros_like(qb)
    qq = jnp.concatenate([jnp.where(lane < HEAD_DIM, qb, zero),
                          jnp.where(lane >= HEAD_DIM, qb, zero)], axis=0)
    s = lax.dot_general(qq, kw, (((1,), (1,)), ((), ())), preferred_element_type=F32) + bias
    m = jnp.max(s, axis=-1, keepdims=True)
    p = jnp.exp2(s - m)
    if not sums_on_mxu:
        l = jnp.sum(p, axis=-1, keepdims=True)
        return m, l, jnp.dot(p.astype(BF16), vw, preferred_element_type=F32)
    pv = jnp.dot(p.astype(BF16), jnp.concatenate([vw, jnp.ones_like(vw)], axis=1),
                 preferred_element_type=F32)
    return m, pv[:, LANES:], pv[:, :LANES]


def _merge_heads(top, bottom, q):
    lane = lax.broadcasted_iota(jnp.int32, (q, LANES), 1)
    return jnp.where(lane < HEAD_DIM, jnp.broadcast_to(top, (q, LANES)),
                     jnp.broadcast_to(bottom, (q, LANES)))


_DQ = 128
_DW = 256
_UNROLL = 16


def _dilated_tables():
    slopes = 2.0 ** (-(np.arange(N_HEADS) + 1.0))
    q = np.arange(_DQ)[:, None]
    t12 = np.zeros((N_PAIRS, 2, 3, 2 * _DQ, _DW), np.float32)
    k = np.arange(_DW)[None, :]
    for pat, dil in enumerate((1, 4)):
        for case, off in enumerate((0, _DW // 4, _DW // 2)):
            delta = np.abs(k - (q + off))
            for h in range(N_HEADS):
                tab = np.where(delta <= DIL_RADIUS, -LOG2E * slopes[h] * dil * delta, NEG_INF)
                t12[h // 2, pat, case, (h % 2) * _DQ:(h % 2 + 1) * _DQ] = tab
    t3 = np.zeros((N_PAIRS, 2 * _DQ, _DQ), np.float32)
    delta = np.abs(np.arange(_DQ)[None, :] - q)
    for h in range(N_HEADS):
        t3[h // 2, (h % 2) * _DQ:(h % 2 + 1) * _DQ] = np.where(
            delta <= DIL_RADIUS, -LOG2E * slopes[h] * 16 * delta, NEG_INF)
    return t12, t3


def _dilated_kernel(q_ref, k_ref, v_ref, t12_ref, t3_ref, o_ref,
                    tmp, tmp4, q4, k4, v4, q16, k16, v16,
                    m1, l1, a1, m2, l2, a2, m3, l3, a3, onat):
    S = tmp.shape[0]
    L4, L16 = S // 4, S // 16

    for src, d4, d16 in ((q_ref, q4, q16), (k_ref, k4, k16), (v_ref, v4, v16)):
        tmp[...] = src[0, 0].astype(F32)
        for r in range(4):
            sub = tmp[pl.ds(r, L4, stride=4), :]
            tmp4[r * L4:(r + 1) * L4, :] = sub
            d4[r * L4:(r + 1) * L4, :] = sub.astype(BF16)
        for r16 in range(16):
            r4, c4 = r16 % 4, r16 // 4
            d16[r16 * L16:(r16 + 1) * L16, :] = tmp4[pl.ds(r4 * L4 + c4, L16, stride=4), :].astype(BF16)

    def block(qb, kw, vw, bias, m_ref, l_ref, a_ref, row):
        m, l, pv = _attend(qb, kw, vw, bias, sums_on_mxu=False)
        m_ref[pl.ds(row, _DQ), :] = _merge_heads(m[:_DQ], m[_DQ:], _DQ)
        l_ref[pl.ds(row, _DQ), :] = _merge_heads(l[:_DQ], l[_DQ:], _DQ)
        a_ref[pl.ds(row, _DQ), :] = _merge_heads(pv[:_DQ], pv[_DQ:], _DQ)

    def case_of(blk, n_blk):
        return jnp.where(blk == 0, 0, jnp.where(blk == n_blk - 1, 2, 1))

    n1 = S // _DQ

    def p1_body(blk, carry):
        t0 = pl.multiple_of(blk * _DQ, _DQ)
        ws = pl.multiple_of(jnp.clip(t0 - DIL_RADIUS, 0, S - _DW), DIL_RADIUS)
        block(q_ref[0, 0, pl.ds(t0, _DQ), :], k_ref[0, 0, pl.ds(ws, _DW), :],
              v_ref[0, 0, pl.ds(ws, _DW), :], t12_ref[0, 0, case_of(blk, n1)], m1, l1, a1, t0)
        return carry

    lax.fori_loop(0, n1, p1_body, 0, unroll=_UNROLL)

    n2 = L4 // _DQ

    def p2_body(j, carry):
        r = j // n2
        blk = j % n2
        l0 = blk * _DQ
        ws = jnp.clip(l0 - DIL_RADIUS, 0, L4 - _DW)
        row = pl.multiple_of(r * L4 + l0, _DQ)
        krow = pl.multiple_of(r * L4 + ws, DIL_RADIUS)
        block(q4[pl.ds(row, _DQ), :], k4[pl.ds(krow, _DW), :], v4[pl.ds(krow, _DW), :],
              t12_ref[0, 1, case_of(blk, n2)], m2, l2, a2, row)
        return carry

    lax.fori_loop(0, 4 * n2, p2_body, 0, unroll=_UNROLL)

    def p3_body(r, carry):
        row = pl.multiple_of(r * L16, L16)
        block(q16[pl.ds(row, L16), :], k16[pl.ds(row, L16), :], v16[pl.ds(row, L16), :],
              t3_ref[0], m3, l3, a3, row)
        return carry

    lax.fori_loop(0, 16, p3_body, 0, unroll=_UNROLL)

    for r16 in range(16):
        r4, c4 = r16 % 4, r16 // 4
        via4 = pl.ds(r4 * L4 + c4, L16, stride=4)
        via16 = pl.ds(r16 * L16, L16)
        mb, mc = m2[via4, :], m3[via16, :]
        mx = jnp.maximum(mb, mc)
        wb, wc = jnp.exp2(mb - mx), jnp.exp2(mc - mx)
        l2[via4, :] = wb * l2[via4, :] + wc * l3[via16, :]
        a2[via4, :] = wb * a2[via4, :] + wc * a3[via16, :]
        m2[via4, :] = mx
    for r4 in range(4):
        for part in range(L4 // _DQ):
            nat = pl.ds(r4 + 4 * _DQ * part, _DQ, stride=4)
            via4 = pl.ds(r4 * L4 + _DQ * part, _DQ)
            ma, mb = m1[nat, :], m2[via4, :]
            mx = jnp.maximum(ma, mb)
            wa, wb = jnp.exp2(ma - mx), jnp.exp2(mb - mx)
            den = wa * l1[nat, :] + wb * l2[via4, :]
            num = wa * a1[nat, :] + wb * a2[via4, :]
            onat[nat, :] = num / den
    o_ref[0] = onat[...].astype(BF16)


_NQ_ROWS = 4
_NK_ROWS = 12


def _na_row_select(rows):
    n_blk = rows // _NQ_ROWS
    sel = np.full((n_blk, _NQ_ROWS, _NK_ROWS), -1, np.int64)
    for i in range(n_blk):
        kr0 = min(max(_NQ_ROWS * i - NA_WIN_ROWS // 2, 0), rows - _NK_ROWS)
        for a in range(_NQ_ROWS):
            qr = _NQ_ROWS * i + a
            rs = min(max(qr - NA_WIN_ROWS // 2, 0), rows - NA_WIN_ROWS)
            for b in range(_NK_ROWS):
                kr = kr0 + b
                if rs <= kr < rs + NA_WIN_ROWS:
                    sel[i, a, b] = kr - qr + NA_WIN_ROWS - 1
    for i in range(2, n_blk - 1):
        assert np.array_equal(sel[1], sel[i])
    return sel[[0, 1, n_blk - 1]]


_NA_QCOLS = 16
_NA_KCOLS = 32
_NA_COLUMN_SETS = tuple(
    (((8 + 16 * j, 16),), ((16 * j, 32),), 0) for j in range(3)
) + ((((0, 8), (GRID_W - 8, 8)), ((0, 16), (GRID_W - 16, 16)), 1),)


def _na_bias_tables(rpb, rows):
    n_dr, n_dc = 2 * NA_WIN_ROWS - 1, 2 * NA_WIN_COLS - 1
    sel = _na_row_select(rows)
    kinds = []
    for kind in (0, 1):
        sets = [cs for cs in _NA_COLUMN_SETS if cs[2] == kind]
        layouts = []
        for q_runs, k_runs, _ in sets:
            qc = np.concatenate([np.arange(c0, c0 + n) for c0, n in q_runs])[:, None]
            kc = np.concatenate([np.arange(c0, c0 + n) for c0, n in k_runs])[None, :]
            start = np.clip(qc - NA_WIN_COLS // 2, 0, GRID_W - NA_WIN_COLS)
            col_ok = (kc >= start) & (kc < start + NA_WIN_COLS)
            assert (col_ok.sum(axis=1) == NA_WIN_COLS).all()
            layouts.append((col_ok, np.clip(kc - qc + NA_WIN_COLS - 1, 0, n_dc - 1)))
        col_ok, dc = layouts[0]
        assert all(np.array_equal(col_ok, o) and np.array_equal(dc, d) for o, d in layouts)
        onehot = (dc.reshape(1, -1) == np.arange(n_dc)[:, None]).astype(np.float32)
        t = jnp.dot(rpb.astype(F32).reshape(N_HEADS * n_dr, n_dc), onehot, precision=lax.Precision.HIGHEST)
        t = jnp.where(col_ok.reshape(1, 1, -1), LOG2E * t.reshape(N_HEADS, n_dr, -1), NEG_INF)
        flat = sel.reshape(-1)
        pick = (flat[:, None] == np.arange(n_dr)[None, :]).astype(np.float32)
        tab = jnp.einsum('pr,hrx->hpx', pick, t, precision=lax.Precision.HIGHEST)
        tab = jnp.where((flat >= 0)[None, :, None], tab, NEG_INF)
        tab = tab.reshape(N_PAIRS, 2, sel.shape[0], _NQ_ROWS, _NK_ROWS, _NA_QCOLS, _NA_KCOLS)
        kinds.append(tab.transpose(0, 2, 1, 3, 5, 4, 6).reshape(
            N_PAIRS, sel.shape[0], 2 * _NQ_ROWS * _NA_QCOLS, _NK_ROWS * _NA_KCOLS))
    return jnp.stack(kinds, axis=1)


def _na_blocks(q_ref, k_ref, v_ref, tab_ref, o_ref, qf, of):
    S = q_ref.shape[2]
    rows = S // GRID_W
    n_blk = rows // _NQ_ROWS
    nq = _NQ_ROWS * _NA_QCOLS
    lane = lax.broadcasted_iota(jnp.int32, (nq, LANES), 1)
    qf[...] = q_ref[0, 0].astype(F32)
    for i in range(n_blk):
        kr0 = min(max(_NQ_ROWS * i - NA_WIN_ROWS // 2, 0), rows - _NK_ROWS)
        case = 0 if i == 0 else (2 if i == n_blk - 1 else 1)
        for q_runs, k_runs, kind in _NA_COLUMN_SETS:
            q_rows = [((_NQ_ROWS * i + a) * GRID_W + c0, n) for a in range(_NQ_ROWS) for c0, n in q_runs]
            k_rows = [((kr0 + b) * GRID_W + c0, n) for b in range(_NK_ROWS) for c0, n in k_runs]
            qb = jnp.concatenate([qf[r0:r0 + n, :] for r0, n in q_rows], axis=0).astype(BF16)
            kw = jnp.concatenate([k_ref[0, 0, r0:r0 + n, :] for r0, n in k_rows], axis=0)
            vw = jnp.concatenate([v_ref[0, 0, r0:r0 + n, :] for r0, n in k_rows], axis=0)
            m, l, pv = _attend(qb, kw, vw, tab_ref[0, kind, case], sums_on_mxu=True)
            o = pv / l
            o = jnp.where(lane < HEAD_DIM, o[:nq], o[nq:])
            at = 0
            for r0, n in q_rows:
                of[r0:r0 + n, :] = o[at:at + n]
                at += n
    o_ref[0] = of[...].astype(BF16)


def _attention_kernel(qd_ref, kd_ref, vd_ref, t12_ref, t3_ref, qn_ref, kn_ref, vn_ref, tna_ref,
                      od_ref, on_ref, qf, of, *dilated_scratch):
    _dilated_kernel(qd_ref, kd_ref, vd_ref, t12_ref, t3_ref, od_ref, *dilated_scratch)
    _na_blocks(qn_ref, kn_ref, vn_ref, tna_ref, on_ref, qf, of)


def _attention(qkv, t12, t3, tna):
    B, _, S, _ = qkv.shape
    f32_buf = pltpu.VMEM((S, LANES), F32)
    bf16_buf = pltpu.VMEM((S, LANES), BF16)
    slab = lambda off: pl.BlockSpec((1, 1, S, LANES), lambda p, b: (b, off + p, 0, 0))
    per_pair = lambda a: pl.BlockSpec((1,) + a.shape[1:], lambda p, b: (p,) + (0,) * (a.ndim - 1))
    out = pl.BlockSpec((1, S, LANES), lambda p, b: (b, 0, p))
    return pl.pallas_call(
        _attention_kernel,
        grid=(N_PAIRS, B),
        in_specs=[slab(0), slab(N_PAIRS), slab(2 * N_PAIRS), per_pair(t12), per_pair(t3),
                  slab(3 * N_PAIRS), slab(4 * N_PAIRS), slab(5 * N_PAIRS), per_pair(tna)],
        out_specs=[out, out],
        out_shape=[jax.ShapeDtypeStruct((B, S, WIDTH), BF16)] * 2,
        scratch_shapes=[f32_buf] * 2 + [f32_buf] * 2 + [bf16_buf] * 6 + [f32_buf] * 10,
        compiler_params=pltpu.CompilerParams(
            dimension_semantics=("arbitrary", "arbitrary"), vmem_limit_bytes=VMEM_LIMIT),
        name="attention",
    )(qkv, qkv, qkv, t12, t3, qkv, qkv, qkv, tna)


_GROUP_LANE0 = N_EXPERTS
_ROUTE_GROUP_LANE = EXPERTS_PER_GROUP


def _route(logits):
    lane_i = lax.broadcasted_iota(jnp.int32, logits.shape, 1)
    lane = lane_i.astype(F32)
    big = float(LANES)
    is_group = (lane_i >= _GROUP_LANE0) & (lane_i < _GROUP_LANE0 + N_GROUPS)
    gl = jnp.where(is_group, logits, NEG_INF)
    gmax = jnp.max(gl, axis=-1, keepdims=True)
    g_idx = jnp.min(jnp.where(is_group & (gl == gmax), lane, big), axis=-1, keepdims=True) - _GROUP_LANE0
    g_weight = 1.0 / jnp.sum(jnp.where(is_group, jnp.exp(gl - gmax), 0.0), axis=-1, keepdims=True)
    in_group = (lane_i < N_EXPERTS) & ((lane_i // EXPERTS_PER_GROUP).astype(F32) == g_idx)
    el = jnp.where(in_group, logits, NEG_INF)
    v1 = jnp.max(el, axis=-1, keepdims=True)
    i1 = jnp.min(jnp.where(in_group & (el == v1), lane, big), axis=-1, keepdims=True)
    rest = in_group & (lane != i1)
    el2 = jnp.where(rest, logits, NEG_INF)
    v2 = jnp.max(el2, axis=-1, keepdims=True)
    i2 = jnp.min(jnp.where(rest & (el2 == v2), lane, big), axis=-1, keepdims=True)
    e2 = jnp.exp(v2 - v1)
    w1 = g_weight / (1.0 + e2)
    w2 = g_weight * e2 / (1.0 + e2)
    base = g_idx * EXPERTS_PER_GROUP
    return jnp.where(lane == i1 - base, w1,
                     jnp.where(lane == i2 - base, w2,
                               jnp.where(lane_i == _ROUTE_GROUP_LANE, g_idx, 0.0)))


def _outproj_kernel(x_ref, yd_ref, yn_ref, gd_ref, gn_ref, wo_ref, gf_ref, wr_ref,
                    br_ref, h_ref, hn_ref, comb_ref):
    yd = _rms(yd_ref[...].astype(F32), gd_ref[...]).astype(BF16)
    yn = _rms(yn_ref[...].astype(F32), gn_ref[...]).astype(BF16)
    h = x_ref[...] + jnp.dot(jnp.concatenate([yd, yn], axis=-1), wo_ref[...], preferred_element_type=F32)
    h_ref[...] = h
    hn = _rms(h, gf_ref[...]).astype(BF16)
    hn_ref[...] = hn
    logits = jnp.dot(hn, wr_ref[...], preferred_element_type=F32) + br_ref[...]
    comb_ref[...] = _route(logits)


def _outproj(x2, yd2, yn2, gd, gn, wo, gf, wr, br, tm):
    N, D = x2.shape
    row = lambda w: pl.BlockSpec((tm, w), lambda i: (i, 0))
    full = lambda a, b: pl.BlockSpec((a, b), lambda i: (0, 0))
    return pl.pallas_call(
        _outproj_kernel,
        grid=(N // tm,),
        in_specs=[row(D), row(WIDTH), row(WIDTH), full(1, WIDTH), full(1, WIDTH), full(2 * WIDTH, D),
                  full(1, D), full(D, LANES), full(1, LANES)],
        out_specs=[row(D), row(D), row(LANES)],
        out_shape=[jax.ShapeDtypeStruct((N, D), F32), jax.ShapeDtypeStruct((N, D), BF16),
                   jax.ShapeDtypeStruct((N, LANES), F32)],
        compiler_params=pltpu.CompilerParams(
            dimension_semantics=("arbitrary",), vmem_limit_bytes=VMEM_LIMIT),
        name="outproj_route",
    )(x2, yd2, yn2, gd, gn, wo, gf, wr, br)


_MOE_TILE = 1024
_MOE_CHUNK = 144
_MOE_NCHUNK = _MOE_TILE // _MOE_CHUNK + N_GROUPS
_GROUP_WIDTH = EXPERTS_PER_GROUP * D_EXPERT
_ROUTE_PIECE = 8
_MOE_VMEM_LIMIT = 58 * 1024 * 1024


def _moe_kernel(hn_ref, route_ref, h_ref, wg_ref, wu_ref, wd_ref, gfin_ref, y_ref, ys_ref, xcat_ref):
    T, C = _MOE_TILE, _MOE_CHUNK
    route = route_ref[...]
    lane = lax.broadcasted_iota(jnp.int32, (T, LANES), 1)
    gid = jnp.sum(jnp.where(lane == _ROUTE_GROUP_LANE, route, 0.0), axis=-1, keepdims=True)
    onehot = jnp.where((lane < N_GROUPS) & (lane.astype(F32) == gid), 1.0, 0.0)

    before = (lax.broadcasted_iota(jnp.int32, (LANES, LANES), 1)
              < lax.broadcasted_iota(jnp.int32, (LANES, LANES), 0)).astype(BF16)
    count = jnp.zeros((1, LANES), F32)
    ranks = []
    for blk in range(T // LANES):
        oh = onehot[blk * LANES:(blk + 1) * LANES]
        ranks.append(jnp.dot(before, oh.astype(BF16), preferred_element_type=F32) + count)
        count = count + jnp.sum(oh, axis=0, keepdims=True)
    rank = jnp.concatenate(ranks, axis=0)
    nchunk = jnp.floor((count + (C - 1)) * (1.0 / C)).astype(jnp.int32)
    off1 = nchunk[0, 0]
    off2 = off1 + nchunk[0, 1]
    off3 = off2 + nchunk[0, 2]
    n_used = off3 + nchunk[0, 3]
    start = jnp.where(lane == 1, off1, jnp.where(lane == 2, off2, jnp.where(lane == 3, off3, 0)))
    pos = jnp.sum(onehot * (rank + (start * C).astype(F32)), axis=-1, keepdims=True)
    pos_i = pos.astype(jnp.int32)
    pos_row = jnp.transpose(jnp.broadcast_to(pos, (T, LANES)))[0:1, :].astype(jnp.int32)

    r_hi = route.astype(BF16).astype(F32)
    r_mid = (route - r_hi).astype(BF16).astype(F32)
    r_lo = (route - r_hi - r_mid).astype(BF16).astype(F32)
    packed = r_hi + pltpu.roll(r_mid, _ROUTE_PIECE, axis=1) + pltpu.roll(r_lo, 2 * _ROUTE_PIECE, axis=1)
    xcat_ref[:, :hn_ref.shape[1]] = hn_ref[...]
    xcat_ref[:, hn_ref.shape[1]:] = packed.astype(BF16)

    def chunk_body(c, carry):
        g = ((c >= off1).astype(jnp.int32) + (c >= off2).astype(jnp.int32)
             + (c >= off3).astype(jnp.int32))
        row0 = pl.multiple_of(c * C, 16)
        sel = (pos_row == row0 + lax.broadcasted_iota(jnp.int32, (C, T), 0)).astype(BF16)
        xr = jnp.dot(sel, xcat_ref[...], preferred_element_type=F32)
        xs = xr[:, :hn_ref.shape[1]].astype(BF16)
        r3 = xr[:, hn_ref.shape[1]:]
        r = (r3 + pltpu.roll(r3, LANES - _ROUTE_PIECE, axis=1)
             + pltpu.roll(r3, LANES - 2 * _ROUTE_PIECE, axis=1))
        clane = lax.broadcasted_iota(jnp.int32, (C, LANES), 1)
        parts = []
        for j in range(EXPERTS_PER_GROUP):
            e = g * EXPERTS_PER_GROUP + j
            gate = jnp.dot(xs, wg_ref[e], preferred_element_type=F32)
            up = jnp.dot(xs, wu_ref[e], preferred_element_type=F32)
            wj = jnp.sum(jnp.where(clane == j, r, 0.0), axis=-1, keepdims=True)
            parts.append((gate / (1.0 + jnp.exp(-gate))) * up * wj)
        act = jnp.concatenate(parts, axis=-1).astype(BF16)
        ys_ref[pl.ds(row0, C), :] = jnp.dot(act, wd_ref[g], preferred_element_type=F32).astype(BF16)
        return carry

    lax.fori_loop(0, n_used, chunk_body, 0)

    def zero_body(c, carry):
        ys_ref[pl.ds(pl.multiple_of(c * C, 16), C), :] = jnp.zeros((C, ys_ref.shape[1]), BF16)
        return carry

    lax.fori_loop(n_used, _MOE_NCHUNK, zero_body, 0)

    n_main = N_GROUPS * (-(-(T // N_GROUPS) // C))
    n_tail = _MOE_NCHUNK - n_main
    back = (lax.broadcasted_iota(jnp.int32, (T, n_main * C), 1) == pos_i).astype(BF16)
    y_ref[...] = h_ref[...] + jnp.dot(back, ys_ref[:n_main * C, :], preferred_element_type=F32)

    @pl.when(n_used > n_main)
    def _():
        tail = (lax.broadcasted_iota(jnp.int32, (T, n_tail * C), 1) == pos_i - n_main * C).astype(BF16)
        y_ref[...] += jnp.dot(tail, ys_ref[n_main * C:, :], preferred_element_type=F32)

    y_ref[...] = _rms(y_ref[...], gfin_ref[...])


def _moe(hn, route, h, wg, wu, wd, gfin):
    N, D = hn.shape
    T = _MOE_TILE
    row = lambda w: pl.BlockSpec((T, w), lambda i: (i, 0))
    whole = lambda a: pl.BlockSpec(a.shape, lambda i: (0,) * a.ndim)
    return pl.pallas_call(
        _moe_kernel,
        grid=(N // T,),
        in_specs=[row(D), row(LANES), row(D),
                  whole(wg), whole(wu), whole(wd), whole(gfin)],
        out_specs=row(D),
        out_shape=jax.ShapeDtypeStruct((N, D), F32),
        scratch_shapes=[pltpu.VMEM((_MOE_NCHUNK * _MOE_CHUNK, D), BF16), pltpu.VMEM((T, D + LANES), BF16)],
        compiler_params=pltpu.CompilerParams(
            dimension_semantics=("arbitrary",), vmem_limit_bytes=_MOE_VMEM_LIMIT),
        name="moe_grouped",
    )(hn, route, h, wg, wu, wd, gfin)


def kernel(x, norm_mix_g, w_in, rpb, g_out_dil, g_out_na, w_out, norm_ffn_g, w_group, b_group,
           w_router, b_router, w_gate, w_up, w_down, norm_final_g):
    B, S, D = x.shape
    N = B * S
    depth = w_in.shape[0]
    assert depth == 1 and D == D_MODEL and S % (16 * _DQ) == 0

    t12, t3 = _dilated_tables()

    layer = 0
    col_scale = np.ones((6, WIDTH), np.float32)
    col_scale[0] = col_scale[3] = LOG2E * HEAD_DIM ** -0.5
    w_in_b = (w_in[layer] * col_scale.reshape(1, -1)).astype(BF16)

    qkv = _inproj(x, norm_mix_g[layer].reshape(1, D), w_in_b, tm=512)

    y_dil, y_na = _attention(qkv, jnp.asarray(t12), jnp.asarray(t3),
                             _na_bias_tables(rpb[layer], S // GRID_W))

    n_route = N_EXPERTS + N_GROUPS
    w_r = jnp.concatenate([w_router[layer], w_group[layer], jnp.zeros((D, LANES - n_route), F32)], axis=1)
    b_r = jnp.concatenate([b_router[layer], b_group[layer], jnp.zeros((LANES - n_route,), F32)]).reshape(1, LANES)

    h, hn, route = _outproj(
        x.reshape(N, D), y_dil.reshape(N, WIDTH), y_na.reshape(N, WIDTH),
        g_out_dil[layer].reshape(1, WIDTH), g_out_na[layer].reshape(1, WIDTH),
        w_out[layer].astype(BF16), norm_ffn_g[layer].reshape(1, D), w_r.astype(BF16), b_r, tm=512)

    wd = w_down[layer].astype(BF16).reshape(N_GROUPS, _GROUP_WIDTH, D)
    y = _moe(hn, route, h, w_gate[layer].astype(BF16), w_up[layer].astype(BF16), wd,
             norm_final_g.reshape(1, D))
    return y.reshape(B, S, D)
```

```python
import numpy as np
import jax
import jax.numpy as jnp
from jax import lax
from jax.experimental import pallas as pl
from jax.experimental.pallas import tpu as pltpu

D_MODEL = 1024
HEAD_DIM = 64
N_HEADS = 8
N_PAIRS = N_HEADS // 2
WIDTH = N_HEADS * HEAD_DIM
N_SLABS = 6 * N_PAIRS
DIL_PATTERNS = ((128, 1), (512, 4), (2048, 16))
DIL_RADIUS = 64
GRID_W = 64
NA_WIN_ROWS = 8
NA_WIN_COLS = 16
N_GROUPS = 4
EXPERTS_PER_GROUP = 4
N_EXPERTS = 16
D_EXPERT = 256
RMS_EPS = 1e-6
NEG_INF = -1e30
LOG2E = 1.4426950408889634

LANES = 128
VMEM_LIMIT = 48 * 1024 * 1024

F32 = jnp.float32
BF16 = jnp.bfloat16


def _rms(x, gain):
    return x * lax.rsqrt(jnp.mean(x * x, axis=-1, keepdims=True) + RMS_EPS) * gain


def _inproj_kernel(x_ref, g_ref, w_ref, s_ref, o_ref, wb_ref):
    @pl.when((pl.program_id(0) == 0) & (pl.program_id(1) == 0))
    def _():
        for j in range(N_SLABS):
            cols = slice(j * LANES, (j + 1) * LANES)
            wb_ref[:, cols] = (w_ref[:, cols] * s_ref[:, cols]).astype(BF16)

    xn = _rms(x_ref[0], g_ref[...]).astype(BF16)
    chunk = 4 * LANES
    for c in range(N_SLABS * LANES // chunk):
        acc = jnp.dot(xn, wb_ref[:, c * chunk:(c + 1) * chunk], preferred_element_type=F32)
        for j in range(chunk // LANES):
            o_ref[0, c * (chunk // LANES) + j] = acc[:, j * LANES:(j + 1) * LANES].astype(BF16)


def _inproj(x, gain, w, col_scale, tm):
    B, S, D = x.shape
    return pl.pallas_call(
        _inproj_kernel,
        grid=(B, S // tm),
        in_specs=[
            pl.BlockSpec((1, tm, D), lambda b, i: (b, i, 0)),
            pl.BlockSpec((1, D), lambda b, i: (0, 0)),
            pl.BlockSpec((D, N_SLABS * LANES), lambda b, i: (0, 0)),
            pl.BlockSpec((1, N_SLABS * LANES), lambda b, i: (0, 0)),
        ],
        out_specs=pl.BlockSpec((1, N_SLABS, tm, LANES), lambda b, i: (b, 0, i, 0)),
        out_shape=jax.ShapeDtypeStruct((B, N_SLABS, S, LANES), BF16),
        scratch_shapes=[pltpu.VMEM((D, N_SLABS * LANES), BF16)],
        compiler_params=pltpu.CompilerParams(
            dimension_semantics=("arbitrary", "arbitrary"), vmem_limit_bytes=VMEM_LIMIT),
        name="inproj",
    )(x, gain, w, col_scale)


def _attend(qb, kw, vw, bias, sums_on_mxu):
    lane = lax.broadcasted_iota(jnp.int32, qb.shape, 1)
    zero = jnp.zeros_like(qb)
    qq = jnp.concatenate([jnp.where(lane < HEAD_DIM, qb, zero),
                          jnp.where(lane >= HEAD_DIM, qb, zero)], axis=0)
    s = lax.dot_general(qq, kw, (((1,), (1,)), ((), ())), preferred_element_type=F32) + bias
    m = jnp.max(s, axis=-1, keepdims=True)
    p = jnp.exp2(s - m)
    if not sums_on_mxu:
        l = jnp.sum(p, axis=-1, keepdims=True)
        return m, l, jnp.dot(p.astype(BF16), vw, preferred_element_type=F32)
    pv = jnp.dot(p.astype(BF16), jnp.concatenate([vw, jnp.ones_like(vw)], axis=1),
                 preferred_element_type=F32)
    return m, pv[:, LANES:], pv[:, :LANES]


def _merge_heads(top, bottom, q):
    lane = lax.broadcasted_iota(jnp.int32, (q, LANES), 1)
    return jnp.where(lane < HEAD_DIM, jnp.broadcast_to(top, (q, LANES)),
                     jnp.broadcast_to(bottom, (q, LANES)))


_DQ = 128
_DW = 256
_UNROLL = 16


def _dilated_tables():
    slopes = 2.0 ** (-(np.arange(N_HEADS) + 1.0))
    q = np.arange(_DQ)[:, None]
    t12 = np.zeros((N_PAIRS, 2, 3, 2 * _DQ, _DW), np.float32)
    k = np.arange(_DW)[None, :]
    for pat, dil in enumerate((1, 4)):
        for case, off in enumerate((0, _DW // 4, _DW // 2)):
            delta = np.abs(k - (q + off))
            for h in range(N_HEADS):
                tab = np.where(delta <= DIL_RADIUS, -LOG2E * slopes[h] * dil * delta, NEG_INF)
                t12[h // 2, pat, case, (h % 2) * _DQ:(h % 2 + 1) * _DQ] = tab
    t3 = np.zeros((N_PAIRS, 2 * _DQ, _DQ), np.float32)
    delta = np.abs(np.arange(_DQ)[None, :] - q)
    for h in range(N_HEADS):
        t3[h // 2, (h % 2) * _DQ:(h % 2 + 1) * _DQ] = np.where(
            delta <= DIL_RADIUS, -LOG2E * slopes[h] * 16 * delta, NEG_INF)
    return t12, t3


def _dilated_kernel(q_ref, k_ref, v_ref, t12_ref, t3_ref, o_ref,
                    tmp, tmp4, q4, k4, v4, q16, k16, v16,
                    m1, l1, a1, m2, l2, a2, m3, l3, a3, onat):
    S = tmp.shape[0]
    L4, L16 = S // 4, S // 16

    for src, d4, d16 in ((q_ref, q4, q16), (k_ref, k4, k16), (v_ref, v4, v16)):
        tmp[...] = src[0, 0].astype(F32)
        for r in range(4):
            sub = tmp[pl.ds(r, L4, stride=4), :]
            tmp4[r * L4:(r + 1) * L4, :] = sub
            d4[r * L4:(r + 1) * L4, :] = sub.astype(BF16)
        for r16 in range(16):
            r4, c4 = r16 % 4, r16 // 4
            d16[r16 * L16:(r16 + 1) * L16, :] = tmp4[pl.ds(r4 * L4 + c4, L16, stride=4), :].astype(BF16)

    def block(qb, kw, vw, bias, m_ref, l_ref, a_ref, row):
        m, l, pv = _attend(qb, kw, vw, bias, sums_on_mxu=False)
        m_ref[pl.ds(row, _DQ), :] = _merge_heads(m[:_DQ], m[_DQ:], _DQ)
        l_ref[pl.ds(row, _DQ), :] = _merge_heads(l[:_DQ], l[_DQ:], _DQ)
        a_ref[pl.ds(row, _DQ), :] = _merge_heads(pv[:_DQ], pv[_DQ:], _DQ)

    def case_of(blk, n_blk):
        return jnp.where(blk == 0, 0, jnp.where(blk == n_blk - 1, 2, 1))

    n1 = S // _DQ

    def p1_body(blk, carry):
        t0 = pl.multiple_of(blk * _DQ, _DQ)
        ws = pl.multiple_of(jnp.clip(t0 - DIL_RADIUS, 0, S - _DW), DIL_RADIUS)
        block(q_ref[0, 0, pl.ds(t0, _DQ), :], k_ref[0, 0, pl.ds(ws, _DW), :],
              v_ref[0, 0, pl.ds(ws, _DW), :], t12_ref[0, 0, case_of(blk, n1)], m1, l1, a1, t0)
        return carry

    lax.fori_loop(0, n1, p1_body, 0, unroll=_UNROLL)

    n2 = L4 // _DQ

    def p2_body(j, carry):
        r = j // n2
        blk = j % n2
        l0 = blk * _DQ
        ws = jnp.clip(l0 - DIL_RADIUS, 0, L4 - _DW)
        row = pl.multiple_of(r * L4 + l0, _DQ)
        krow = pl.multiple_of(r * L4 + ws, DIL_RADIUS)
        block(q4[pl.ds(row, _DQ), :], k4[pl.ds(krow, _DW), :], v4[pl.ds(krow, _DW), :],
              t12_ref[0, 1, case_of(blk, n2)], m2, l2, a2, row)
        return carry

    lax.fori_loop(0, 4 * n2, p2_body, 0, unroll=_UNROLL)

    def p3_body(r, carry):
        row = pl.multiple_of(r * L16, L16)
        block(q16[pl.ds(row, L16), :], k16[pl.ds(row, L16), :], v16[pl.ds(row, L16), :],
              t3_ref[0], m3, l3, a3, row)
        return carry

    lax.fori_loop(0, 16, p3_body, 0, unroll=_UNROLL)

    for r16 in range(16):
        r4, c4 = r16 % 4, r16 // 4
        via4 = pl.ds(r4 * L4 + c4, L16, stride=4)
        via16 = pl.ds(r16 * L16, L16)
        mb, mc = m2[via4, :], m3[via16, :]
        mx = jnp.maximum(mb, mc)
        wb, wc = jnp.exp2(mb - mx), jnp.exp2(mc - mx)
        l2[via4, :] = wb * l2[via4, :] + wc * l3[via16, :]
        a2[via4, :] = wb * a2[via4, :] + wc * a3[via16, :]
        m2[via4, :] = mx
    for r4 in range(4):
        for part in range(L4 // _DQ):
            nat = pl.ds(r4 + 4 * _DQ * part, _DQ, stride=4)
            via4 = pl.ds(r4 * L4 + _DQ * part, _DQ)
            ma, mb = m1[nat, :], m2[via4, :]
            mx = jnp.maximum(ma, mb)
            wa, wb = jnp.exp2(ma - mx), jnp.exp2(mb - mx)
            den = wa * l1[nat, :] + wb * l2[via4, :]
            num = wa * a1[nat, :] + wb * a2[via4, :]
            onat[nat, :] = num / den
    o_ref[0] = onat[...].astype(BF16)


_NQ_ROWS = 4
_NK_ROWS = 12


def _na_row_select(rows):
    n_blk = rows // _NQ_ROWS
    sel = np.full((n_blk, _NQ_ROWS, _NK_ROWS), -1, np.int64)
    for i in range(n_blk):
        kr0 = min(max(_NQ_ROWS * i - NA_WIN_ROWS // 2, 0), rows - _NK_ROWS)
        for a in range(_NQ_ROWS):
            qr = _NQ_ROWS * i + a
            rs = min(max(qr - NA_WIN_ROWS // 2, 0), rows - NA_WIN_ROWS)
            for b in range(_NK_ROWS):
                kr = kr0 + b
                if rs <= kr < rs + NA_WIN_ROWS:
                    sel[i, a, b] = kr - qr + NA_WIN_ROWS - 1
    for i in range(2, n_blk - 1):
        assert np.array_equal(sel[1], sel[i])
    return sel[[0, 1, n_blk - 1]]


_NA_QCOLS = 16
_NA_KCOLS = 32
_NA_COLUMN_SETS = tuple(
    (((8 + 16 * j, 16),), ((16 * j, 32),), 0) for j in range(3)
) + ((((0, 8), (GRID_W - 8, 8)), ((0, 16), (GRID_W - 16, 16)), 1),)


def _na_column_bias(rpb):
    n_dr, n_dc = 2 * NA_WIN_ROWS - 1, 2 * NA_WIN_COLS - 1
    kinds = []
    for kind in (0, 1):
        sets = [cs for cs in _NA_COLUMN_SETS if cs[2] == kind]
        layouts = []
        for q_runs, k_runs, _ in sets:
            qc = np.concatenate([np.arange(c0, c0 + n) for c0, n in q_runs])[:, None]
            kc = np.concatenate([np.arange(c0, c0 + n) for c0, n in k_runs])[None, :]
            start = np.clip(qc - NA_WIN_COLS // 2, 0, GRID_W - NA_WIN_COLS)
            col_ok = (kc >= start) & (kc < start + NA_WIN_COLS)
            assert (col_ok.sum(axis=1) == NA_WIN_COLS).all()
            layouts.append((col_ok, np.clip(kc - qc + NA_WIN_COLS - 1, 0, n_dc - 1)))
        col_ok, dc = layouts[0]
        assert all(np.array_equal(col_ok, o) and np.array_equal(dc, d) for o, d in layouts)
        onehot = (dc.reshape(1, -1) == np.arange(n_dc)[:, None]).astype(np.float32)
        t = jnp.dot(rpb.astype(F32).reshape(N_HEADS * n_dr, n_dc), onehot, precision=lax.Precision.HIGHEST)
        t = jnp.where(col_ok[None, None], LOG2E * t.reshape(N_HEADS, n_dr, _NA_QCOLS, _NA_KCOLS), NEG_INF)
        kinds.append(jnp.tile(t, (1, 1, 1, LANES // _NA_KCOLS)).reshape(
            N_PAIRS, 2, n_dr, _NA_QCOLS, LANES))
    return jnp.stack(kinds, axis=1)


def _na_expand_table(cb_ref, tab_ref, rows):
    n_blk = rows // _NQ_ROWS
    nq, nk = _NQ_ROWS * _NA_QCOLS, _NK_ROWS * _NA_KCOLS
    sel = _na_row_select(rows)

    @pl.when(pl.program_id(1) == 0)
    def _():
        a_idx = lax.broadcasted_iota(jnp.int32, (nq, nk), 0) // _NA_QCOLS
        b_idx = lax.broadcasted_iota(jnp.int32, (nq, nk), 1) // _NA_KCOLS
        for c, i in enumerate((0, 1, n_blk - 1)):
            kr0 = min(max(_NQ_ROWS * i - NA_WIN_ROWS // 2, 0), rows - _NK_ROWS)
            qr = _NQ_ROWS * i + a_idx
            kr = kr0 + b_idx
            first = jnp.clip(qr - NA_WIN_ROWS // 2, 0, rows - NA_WIN_ROWS)
            dr = jnp.where((kr >= first) & (kr < first + NA_WIN_ROWS), kr - qr + NA_WIN_ROWS - 1, -1)
            for kind in range(cb_ref.shape[1]):
                for h in range(2):
                    out = jnp.full((nq, nk), NEG_INF, F32)
                    for r in sorted(set(int(s) for s in sel[c].reshape(-1)) - {-1}):
                        tile = jnp.tile(cb_ref[0, kind, h, r], (_NQ_ROWS, nk // LANES))
                        out = jnp.where(dr == r, tile, out)
                    tab_ref[kind, c, h * nq:(h + 1) * nq, :] = out


def _na_blocks(q_ref, k_ref, v_ref, tab_ref, o_ref, qf, of):
    S = q_ref.shape[2]
    rows = S // GRID_W
    n_blk = rows // _NQ_ROWS
    nq = _NQ_ROWS * _NA_QCOLS
    lane = lax.broadcasted_iota(jnp.int32, (nq, LANES), 1)
    qf[...] = q_ref[0, 0].astype(F32)
    for i in range(n_blk):
        kr0 = min(max(_NQ_ROWS * i - NA_WIN_ROWS // 2, 0), rows - _NK_ROWS)
        case = 0 if i == 0 else (2 if i == n_blk - 1 else 1)
        for q_runs, k_runs, kind in _NA_COLUMN_SETS:
            q_rows = [((_NQ_ROWS * i + a) * GRID_W + c0, n) for a in range(_NQ_ROWS) for c0, n in q_runs]
            k_rows = [((kr0 + b) * GRID_W + c0, n) for b in range(_NK_ROWS) for c0, n in k_runs]
            qb = jnp.concatenate([qf[r0:r0 + n, :] for r0, n in q_rows], axis=0).astype(BF16)
            kw = jnp.concatenate([k_ref[0, 0, r0:r0 + n, :] for r0, n in k_rows], axis=0)
            vw = jnp.concatenate([v_ref[0, 0, r0:r0 + n, :] for r0, n in k_rows], axis=0)
            m, l, pv = _attend(qb, kw, vw, tab_ref[kind, case], sums_on_mxu=True)
            o = pv / l
            o = jnp.where(lane < HEAD_DIM, o[:nq], o[nq:])
            at = 0
            for r0, n in q_rows:
                of[r0:r0 + n, :] = o[at:at + n]
                at += n
    o_ref[0] = of[...].astype(BF16)


def _attention_kernel(qd_ref, kd_ref, vd_ref, t12_ref, t3_ref, qn_ref, kn_ref, vn_ref, cb_ref,
                      od_ref, on_ref, tab_ref, qf, of, *dilated_scratch):
    _na_expand_table(cb_ref, tab_ref, qn_ref.shape[2] // GRID_W)
    _dilated_kernel(qd_ref, kd_ref, vd_ref, t12_ref, t3_ref, od_ref, *dilated_scratch)
    _na_blocks(qn_ref, kn_ref, vn_ref, tab_ref, on_ref, qf, of)


def _attention(qkv, t12, t3, col_bias):
    B, _, S, _ = qkv.shape
    f32_buf = pltpu.VMEM((S, LANES), F32)
    bf16_buf = pltpu.VMEM((S, LANES), BF16)
    slab = lambda off: pl.BlockSpec((1, 1, S, LANES), lambda p, b: (b, off + p, 0, 0))
    per_pair = lambda a: pl.BlockSpec((1,) + a.shape[1:], lambda p, b: (p,) + (0,) * (a.ndim - 1))
    out = pl.BlockSpec((1, S, LANES), lambda p, b: (b, 0, p))
    return pl.pallas_call(
        _attention_kernel,
        grid=(N_PAIRS, B),
        in_specs=[slab(0), slab(N_PAIRS), slab(2 * N_PAIRS), per_pair(t12), per_pair(t3),
                  slab(3 * N_PAIRS), slab(4 * N_PAIRS), slab(5 * N_PAIRS), per_pair(col_bias)],
        out_specs=[out, out],
        out_shape=[jax.ShapeDtypeStruct((B, S, WIDTH), BF16)] * 2,
        scratch_shapes=([pltpu.VMEM((col_bias.shape[1], 3, 2 * _NQ_ROWS * _NA_QCOLS,
                                     _NK_ROWS * _NA_KCOLS), F32)]
                        + [f32_buf] * 2 + [f32_buf] * 2 + [bf16_buf] * 6 + [f32_buf] * 10),
        compiler_params=pltpu.CompilerParams(
            dimension_semantics=("arbitrary", "arbitrary"), vmem_limit_bytes=VMEM_LIMIT),
        name="attention",
    )(qkv, qkv, qkv, t12, t3, qkv, qkv, qkv, col_bias)


_GROUP_LANE0 = N_EXPERTS
_ROUTE_GROUP_LANE = EXPERTS_PER_GROUP


def _route(logits):
    lane_i = lax.broadcasted_iota(jnp.int32, logits.shape, 1)
    lane = lane_i.astype(F32)
    big = float(LANES)
    is_group = (lane_i >= _GROUP_LANE0) & (lane_i < _GROUP_LANE0 + N_GROUPS)
    gl = jnp.where(is_group, logits, NEG_INF)
    gmax = jnp.max(gl, axis=-1, keepdims=True)
    g_idx = jnp.min(jnp.where(is_group & (gl == gmax), lane, big), axis=-1, keepdims=True) - _GROUP_LANE0
    g_weight = 1.0 / jnp.sum(jnp.where(is_group, jnp.exp(gl - gmax), 0.0), axis=-1, keepdims=True)
    in_group = (lane_i < N_EXPERTS) & ((lane_i // EXPERTS_PER_GROUP).astype(F32) == g_idx)
    el = jnp.where(in_group, logits, NEG_INF)
    v1 = jnp.max(el, axis=-1, keepdims=True)
    i1 = jnp.min(jnp.where(in_group & (el == v1), lane, big), axis=-1, keepdims=True)
    rest = in_group & (lane != i1)
    el2 = jnp.where(rest, logits, NEG_INF)
    v2 = jnp.max(el2, axis=-1, keepdims=True)
    i2 = jnp.min(jnp.where(rest & (el2 == v2), lane, big), axis=-1, keepdims=True)
    e2 = jnp.exp(v2 - v1)
    w1 = g_weight / (1.0 + e2)
    w2 = g_weight * e2 / (1.0 + e2)
    base = g_idx * EXPERTS_PER_GROUP
    return jnp.where(lane == i1 - base, w1,
                     jnp.where(lane == i2 - base, w2,
                               jnp.where(lane_i == _ROUTE_GROUP_LANE, g_idx, 0.0)))


def _outproj_kernel(x_ref, yd_ref, yn_ref, gd_ref, gn_ref, wo_ref, gf_ref, wr_ref,
                    br_ref, h_ref, hn_ref, comb_ref):
    yd = _rms(yd_ref[...].astype(F32), gd_ref[...]).astype(BF16)
    yn = _rms(yn_ref[...].astype(F32), gn_ref[...]).astype(BF16)
    h = x_ref[...] + jnp.dot(jnp.concatenate([yd, yn], axis=-1), wo_ref[...], preferred_element_type=F32)
    h_ref[...] = h
    hn = _rms(h, gf_ref[...]).astype(BF16)
    hn_ref[...] = hn
    logits = jnp.dot(hn, wr_ref[...], preferred_element_type=F32) + br_ref[...]
    comb_ref[...] = _route(logits)


def _outproj(x2, yd2, yn2, gd, gn, wo, gf, wr, br, tm):
    N, D = x2.shape
    row = lambda w: pl.BlockSpec((tm, w), lambda i: (i, 0))
    full = lambda a, b: pl.BlockSpec((a, b), lambda i: (0, 0))
    return pl.pallas_call(
        _outproj_kernel,
        grid=(N // tm,),
        in_specs=[row(D), row(WIDTH), row(WIDTH), full(1, WIDTH), full(1, WIDTH), full(2 * WIDTH, D),
                  full(1, D), full(D, LANES), full(1, LANES)],
        out_specs=[row(D), row(D), row(LANES)],
        out_shape=[jax.ShapeDtypeStruct((N, D), F32), jax.ShapeDtypeStruct((N, D), BF16),
                   jax.ShapeDtypeStruct((N, LANES), F32)],
        compiler_params=pltpu.CompilerParams(
            dimension_semantics=("arbitrary",), vmem_limit_bytes=VMEM_LIMIT),
        name="outproj_route",
    )(x2, yd2, yn2, gd, gn, wo, gf, wr, br)


_MOE_TILE = 1024
_MOE_CHUNK = 144
_MOE_NCHUNK = _MOE_TILE // _MOE_CHUNK + N_GROUPS
_GROUP_WIDTH = EXPERTS_PER_GROUP * D_EXPERT
_ROUTE_PIECE = 8
_MOE_VMEM_LIMIT = 58 * 1024 * 1024


def _moe_kernel(hn_ref, route_ref, h_ref, wg_ref, wu_ref, wd_ref, gfin_ref, y_ref, ys_ref, xcat_ref):
    T, C = _MOE_TILE, _MOE_CHUNK
    route = route_ref[...]
    lane = lax.broadcasted_iota(jnp.int32, (T, LANES), 1)
    gid = jnp.sum(jnp.where(lane == _ROUTE_GROUP_LANE, route, 0.0), axis=-1, keepdims=True)
    onehot = jnp.where((lane < N_GROUPS) & (lane.astype(F32) == gid), 1.0, 0.0)

    before = (lax.broadcasted_iota(jnp.int32, (LANES, LANES), 1)
              < lax.broadcasted_iota(jnp.int32, (LANES, LANES), 0)).astype(BF16)
    count = jnp.zeros((1, LANES), F32)
    ranks = []
    for blk in range(T // LANES):
        oh = onehot[blk * LANES:(blk + 1) * LANES]
        ranks.append(jnp.dot(before, oh.astype(BF16), preferred_element_type=F32) + count)
        count = count + jnp.sum(oh, axis=0, keepdims=True)
    rank = jnp.concatenate(ranks, axis=0)
    nchunk = jnp.floor((count + (C - 1)) * (1.0 / C)).astype(jnp.int32)
    off1 = nchunk[0, 0]
    off2 = off1 + nchunk[0, 1]
    off3 = off2 + nchunk[0, 2]
    n_used = off3 + nchunk[0, 3]
    start = jnp.where(lane == 1, off1, jnp.where(lane == 2, off2, jnp.where(lane == 3, off3, 0)))
    pos = jnp.sum(onehot * (rank + (start * C).astype(F32)), axis=-1, keepdims=True)
    pos_i = pos.astype(jnp.int32)
    pos_row = jnp.transpose(jnp.broadcast_to(pos, (T, LANES)))[0:1, :].astype(jnp.int32)

    r_hi = route.astype(BF16).astype(F32)
    r_mid = (route - r_hi).astype(BF16).astype(F32)
    r_lo = (route - r_hi - r_mid).astype(BF16).astype(F32)
    packed = r_hi + pltpu.roll(r_mid, _ROUTE_PIECE, axis=1) + pltpu.roll(r_lo, 2 * _ROUTE_PIECE, axis=1)
    xcat_ref[:, :hn_ref.shape[1]] = hn_ref[...]
    xcat_ref[:, hn_ref.shape[1]:] = packed.astype(BF16)

    def chunk_body(c, carry):
        g = ((c >= off1).astype(jnp.int32) + (c >= off2).astype(jnp.int32)
             + (c >= off3).astype(jnp.int32))
        row0 = pl.multiple_of(c * C, 16)
        sel = (pos_row == row0 + lax.broadcasted_iota(jnp.int32, (C, T), 0)).astype(BF16)
        xr = jnp.dot(sel, xcat_ref[...], preferred_element_type=F32)
        xs = xr[:, :hn_ref.shape[1]].astype(BF16)
        r3 = xr[:, hn_ref.shape[1]:]
        r = (r3 + pltpu.roll(r3, LANES - _ROUTE_PIECE, axis=1)
             + pltpu.roll(r3, LANES - 2 * _ROUTE_PIECE, axis=1))
        clane = lax.broadcasted_iota(jnp.int32, (C, LANES), 1)
        parts = []
        for j in range(EXPERTS_PER_GROUP):
            e = g * EXPERTS_PER_GROUP + j
            gate = jnp.dot(xs, wg_ref[e], preferred_element_type=F32)
            up = jnp.dot(xs, wu_ref[e], preferred_element_type=F32)
            wj = jnp.sum(jnp.where(clane == j, r, 0.0), axis=-1, keepdims=True)
            parts.append((gate / (1.0 + jnp.exp(-gate))) * up * wj)
        act = jnp.concatenate(parts, axis=-1).astype(BF16)
        ys_ref[pl.ds(row0, C), :] = jnp.dot(act, wd_ref[g], preferred_element_type=F32).astype(BF16)
        return carry

    lax.fori_loop(0, n_used, chunk_body, 0)

    def zero_body(c, carry):
        ys_ref[pl.ds(pl.multiple_of(c * C, 16), C), :] = jnp.zeros((C, ys_ref.shape[1]), BF16)
        return carry

    lax.fori_loop(n_used, _MOE_NCHUNK, zero_body, 0)

    n_main = N_GROUPS * (-(-(T // N_GROUPS) // C))
    n_tail = _MOE_NCHUNK - n_main
    back = (lax.broadcasted_iota(jnp.int32, (T, n_main * C), 1) == pos_i).astype(BF16)
    y_ref[...] = h_ref[...] + jnp.dot(back, ys_ref[:n_main * C, :], preferred_element_type=F32)

    @pl.when(n_used > n_main)
    def _():
        tail = (lax.broadcasted_iota(jnp.int32, (T, n_tail * C), 1) == pos_i - n_main * C).astype(BF16)
        y_ref[...] += jnp.dot(tail, ys_ref[n_main * C:, :], preferred_element_type=F32)

    y_ref[...] = _rms(y_ref[...], gfin_ref[...])


def _moe(hn, route, h, wg, wu, wd, gfin):
    N, D = hn.shape
    T = _MOE_TILE
    row = lambda w: pl.BlockSpec((T, w), lambda i: (i, 0))
    whole = lambda a: pl.BlockSpec(a.shape, lambda i: (0,) * a.ndim)
    return pl.pallas_call(
        _moe_kernel,
        grid=(N // T,),
        in_specs=[row(D), row(LANES), row(D),
                  whole(wg), whole(wu), whole(wd), whole(gfin)],
        out_specs=row(D),
        out_shape=jax.ShapeDtypeStruct((N, D), F32),
        scratch_shapes=[pltpu.VMEM((_MOE_NCHUNK * _MOE_CHUNK, D), BF16), pltpu.VMEM((T, D + LANES), BF16)],
        compiler_params=pltpu.CompilerParams(
            dimension_semantics=("arbitrary",), vmem_limit_bytes=_MOE_VMEM_LIMIT),
        name="moe_grouped",
    )(hn, route, h, wg, wu, wd, gfin)


def kernel(x, norm_mix_g, w_in, rpb, g_out_dil, g_out_na, w_out, norm_ffn_g, w_group, b_group,
           w_router, b_router, w_gate, w_up, w_down, norm_final_g):
    B, S, D = x.shape
    N = B * S
    depth = w_in.shape[0]
    assert depth == 1 and D == D_MODEL and S % (16 * _DQ) == 0

    t12, t3 = _dilated_tables()

    layer = 0
    col_scale = np.ones((6, WIDTH), np.float32)
    col_scale[0] = col_scale[3] = LOG2E * HEAD_DIM ** -0.5
    qkv = _inproj(x, norm_mix_g[layer].reshape(1, D), w_in[layer], jnp.asarray(col_scale.reshape(1, -1)),
                  tm=512)

    y_dil, y_na = _attention(qkv, jnp.asarray(t12), jnp.asarray(t3), _na_column_bias(rpb[layer]))

    n_route = N_EXPERTS + N_GROUPS
    w_r = jnp.concatenate([w_router[layer], w_group[layer], jnp.zeros((D, LANES - n_route), F32)], axis=1)
    b_r = jnp.concatenate([b_router[layer], b_group[layer], jnp.zeros((LANES - n_route,), F32)]).reshape(1, LANES)

    h, hn, route = _outproj(
        x.reshape(N, D), y_dil.reshape(N, WIDTH), y_na.reshape(N, WIDTH),
        g_out_dil[layer].reshape(1, WIDTH), g_out_na[layer].reshape(1, WIDTH),
        w_out[layer].astype(BF16), norm_ffn_g[layer].reshape(1, D), w_r.astype(BF16), b_r, tm=512)

    wd = w_down[layer].astype(BF16).reshape(N_GROUPS, _GROUP_WIDTH, D)
    y = _moe(hn, route, h, w_gate[layer].astype(BF16), w_up[layer].astype(BF16), wd,
             norm_final_g.reshape(1, D))
    return y.reshape(B, S, D)
```

```python
import numpy as np
import jax
import jax.numpy as jnp
from jax import lax
from jax.experimental import pallas as pl
from jax.experimental.pallas import tpu as pltpu

D_MODEL = 1024
HEAD_DIM = 64
N_HEADS = 8
N_PAIRS = N_HEADS // 2
WIDTH = N_HEADS * HEAD_DIM
N_SLABS = 6 * N_PAIRS
DIL_PATTERNS = ((128, 1), (512, 4), (2048, 16))
DIL_RADIUS = 64
GRID_W = 64
NA_WIN_ROWS = 8
NA_WIN_COLS = 16
N_GROUPS = 4
EXPERTS_PER_GROUP = 4
N_EXPERTS = 16
D_EXPERT = 256
RMS_EPS = 1e-6
NEG_INF = -1e30
LOG2E = 1.4426950408889634

LANES = 128
VMEM_LIMIT = 48 * 1024 * 1024

F32 = jnp.float32
BF16 = jnp.bfloat16


def _rms(x, gain):
    return x * lax.rsqrt(jnp.mean(x * x, axis=-1, keepdims=True) + RMS_EPS) * gain


def _inproj_kernel(x_ref, g_ref, w_ref, s_ref, o_ref, wb_ref):
    @pl.when((pl.program_id(0) == 0) & (pl.program_id(1) == 0))
    def _():
        for j in range(N_SLABS):
            cols = slice(j * LANES, (j + 1) * LANES)
            wb_ref[:, cols] = (w_ref[:, cols] * s_ref[:, cols]).astype(BF16)

    xn = _rms(x_ref[0], g_ref[...]).astype(BF16)
    chunk = 4 * LANES
    for c in range(N_SLABS * LANES // chunk):
        acc = jnp.dot(xn, wb_ref[:, c * chunk:(c + 1) * chunk], preferred_element_type=F32)
        for j in range(chunk // LANES):
            o_ref[0, c * (chunk // LANES) + j] = acc[:, j * LANES:(j + 1) * LANES].astype(BF16)


def _inproj(x, gain, w, col_scale, tm):
    B, S, D = x.shape
    return pl.pallas_call(
        _inproj_kernel,
        grid=(B, S // tm),
        in_specs=[
            pl.BlockSpec((1, tm, D), lambda b, i: (b, i, 0)),
            pl.BlockSpec((1, D), lambda b, i: (0, 0)),
            pl.BlockSpec((D, N_SLABS * LANES), lambda b, i: (0, 0)),
            pl.BlockSpec((1, N_SLABS * LANES), lambda b, i: (0, 0)),
        ],
        out_specs=pl.BlockSpec((1, N_SLABS, tm, LANES), lambda b, i: (b, 0, i, 0)),
        out_shape=jax.ShapeDtypeStruct((B, N_SLABS, S, LANES), BF16),
        scratch_shapes=[pltpu.VMEM((D, N_SLABS * LANES), BF16)],
        compiler_params=pltpu.CompilerParams(
            dimension_semantics=("arbitrary", "arbitrary"), vmem_limit_bytes=VMEM_LIMIT),
        name="inproj",
    )(x, gain, w, col_scale)


def _attend(qb, kw, vw, bias, sums_on_mxu):
    lane = lax.broadcasted_iota(jnp.int32, qb.shape, 1)
    zero = jnp.zeros_like(qb)
    qq = jnp.concatenate([jnp.where(lane < HEAD_DIM, qb, zero),
                          jnp.where(lane >= HEAD_DIM, qb, zero)], axis=0)
    s = lax.dot_general(qq, kw, (((1,), (1,)), ((), ())), preferred_element_type=F32) + bias
    m = jnp.max(s, axis=-1, keepdims=True)
    p = jnp.exp2(s - m)
    if not sums_on_mxu:
        l = jnp.sum(p, axis=-1, keepdims=True)
        return m, l, jnp.dot(p.astype(BF16), vw, preferred_element_type=F32)
    pv = jnp.dot(p.astype(BF16), jnp.concatenate([vw, jnp.ones_like(vw)], axis=1),
                 preferred_element_type=F32)
    return m, pv[:, LANES:], pv[:, :LANES]


def _merge_heads(top, bottom, q):
    lane = lax.broadcasted_iota(jnp.int32, (q, LANES), 1)
    return jnp.where(lane < HEAD_DIM, jnp.broadcast_to(top, (q, LANES)),
                     jnp.broadcast_to(bottom, (q, LANES)))


_DQ = 128
_DW = 256
_UNROLL = 16


def _dilated_tables():
    slopes = 2.0 ** (-(np.arange(N_HEADS) + 1.0))
    q = np.arange(_DQ)[:, None]
    t12 = np.zeros((N_PAIRS, 2, 3, 2 * _DQ, _DW), np.float32)
    k = np.arange(_DW)[None, :]
    for pat, dil in enumerate((1, 4)):
        for case, off in enumerate((0, _DW // 4, _DW // 2)):
            delta = np.abs(k - (q + off))
            for h in range(N_HEADS):
                tab = np.where(delta <= DIL_RADIUS, -LOG2E * slopes[h] * dil * delta, NEG_INF)
                t12[h // 2, pat, case, (h % 2) * _DQ:(h % 2 + 1) * _DQ] = tab
    t3 = np.zeros((N_PAIRS, 2 * _DQ, _DQ), np.float32)
    delta = np.abs(np.arange(_DQ)[None, :] - q)
    for h in range(N_HEADS):
        t3[h // 2, (h % 2) * _DQ:(h % 2 + 1) * _DQ] = np.where(
            delta <= DIL_RADIUS, -LOG2E * slopes[h] * 16 * delta, NEG_INF)
    return t12, t3


def _dilated_kernel(q_ref, k_ref, v_ref, t12_ref, t3_ref, o_ref,
                    tmp, tmp4, q4, k4, v4, q16, k16, v16,
                    m1, l1, a1, m2, l2, a2, m3, l3, a3, onat):
    S = tmp.shape[0]
    L4, L16 = S // 4, S // 16

    for src, d4, d16 in ((q_ref, q4, q16), (k_ref, k4, k16), (v_ref, v4, v16)):
        tmp[...] = src[0, 0].astype(F32)
        for r in range(4):
            sub = tmp[pl.ds(r, L4, stride=4), :]
            tmp4[r * L4:(r + 1) * L4, :] = sub
            d4[r * L4:(r + 1) * L4, :] = sub.astype(BF16)
        for r16 in range(16):
            r4, c4 = r16 % 4, r16 // 4
            d16[r16 * L16:(r16 + 1) * L16, :] = tmp4[pl.ds(r4 * L4 + c4, L16, stride=4), :].astype(BF16)

    def block(qb, kw, vw, bias, m_ref, l_ref, a_ref, row):
        m, l, pv = _attend(qb, kw, vw, bias, sums_on_mxu=False)
        m_ref[pl.ds(row, _DQ), :] = _merge_heads(m[:_DQ], m[_DQ:], _DQ)
        l_ref[pl.ds(row, _DQ), :] = _merge_heads(l[:_DQ], l[_DQ:], _DQ)
        a_ref[pl.ds(row, _DQ), :] = _merge_heads(pv[:_DQ], pv[_DQ:], _DQ)

    def case_of(blk, n_blk):
        return jnp.where(blk == 0, 0, jnp.where(blk == n_blk - 1, 2, 1))

    n1 = S // _DQ

    def p1_body(blk, carry):
        t0 = pl.multiple_of(blk * _DQ, _DQ)
        ws = pl.multiple_of(jnp.clip(t0 - DIL_RADIUS, 0, S - _DW), DIL_RADIUS)
        block(q_ref[0, 0, pl.ds(t0, _DQ), :], k_ref[0, 0, pl.ds(ws, _DW), :],
              v_ref[0, 0, pl.ds(ws, _DW), :], t12_ref[0, 0, case_of(blk, n1)], m1, l1, a1, t0)
        return carry

    lax.fori_loop(0, n1, p1_body, 0, unroll=_UNROLL)

    n2 = L4 // _DQ

    def p2_body(j, carry):
        r = j // n2
        blk = j % n2
        l0 = blk * _DQ
        ws = jnp.clip(l0 - DIL_RADIUS, 0, L4 - _DW)
        row = pl.multiple_of(r * L4 + l0, _DQ)
        krow = pl.multiple_of(r * L4 + ws, DIL_RADIUS)
        block(q4[pl.ds(row, _DQ), :], k4[pl.ds(krow, _DW), :], v4[pl.ds(krow, _DW), :],
              t12_ref[0, 1, case_of(blk, n2)], m2, l2, a2, row)
        return carry

    lax.fori_loop(0, 4 * n2, p2_body, 0, unroll=_UNROLL)

    def p3_body(r, carry):
        row = pl.multiple_of(r * L16, L16)
        block(q16[pl.ds(row, L16), :], k16[pl.ds(row, L16), :], v16[pl.ds(row, L16), :],
              t3_ref[0], m3, l3, a3, row)
        return carry

    lax.fori_loop(0, 16, p3_body, 0, unroll=_UNROLL)

    for r16 in range(16):
        r4, c4 = r16 % 4, r16 // 4
        via4 = pl.ds(r4 * L4 + c4, L16, stride=4)
        via16 = pl.ds(r16 * L16, L16)
        mb, mc = m2[via4, :], m3[via16, :]
        mx = jnp.maximum(mb, mc)
        wb, wc = jnp.exp2(mb - mx), jnp.exp2(mc - mx)
        l2[via4, :] = wb * l2[via4, :] + wc * l3[via16, :]
        a2[via4, :] = wb * a2[via4, :] + wc * a3[via16, :]
        m2[via4, :] = mx
    for r4 in range(4):
        for part in range(L4 // _DQ):
            nat = pl.ds(r4 + 4 * _DQ * part, _DQ, stride=4)
            via4 = pl.ds(r4 * L4 + _DQ * part, _DQ)
            ma, mb = m1[nat, :], m2[via4, :]
            mx = jnp.maximum(ma, mb)
            wa, wb = jnp.exp2(ma - mx), jnp.exp2(mb - mx)
            den = wa * l1[nat, :] + wb * l2[via4, :]
            num = wa * a1[nat, :] + wb * a2[via4, :]
            onat[nat, :] = num / den
    o_ref[0] = onat[...].astype(BF16)


_NQ_ROWS = 4
_NK_ROWS = 12


def _na_row_select(rows):
    n_blk = rows // _NQ_ROWS
    sel = np.full((n_blk, _NQ_ROWS, _NK_ROWS), -1, np.int64)
    for i in range(n_blk):
        kr0 = min(max(_NQ_ROWS * i - NA_WIN_ROWS // 2, 0), rows - _NK_ROWS)
        for a in range(_NQ_ROWS):
            qr = _NQ_ROWS * i + a
            rs = min(max(qr - NA_WIN_ROWS // 2, 0), rows - NA_WIN_ROWS)
            for b in range(_NK_ROWS):
                kr = kr0 + b
                if rs <= kr < rs + NA_WIN_ROWS:
                    sel[i, a, b] = kr - qr + NA_WIN_ROWS - 1
    for i in range(2, n_blk - 1):
        assert np.array_equal(sel[1], sel[i])
    return sel[[0, 1, n_blk - 1]]


_NA_QCOLS = 16
_NA_KCOLS = 32
_NA_COLUMN_SETS = tuple(
    (((8 + 16 * j, 16),), ((16 * j, 32),), 0) for j in range(3)
) + ((((0, 8), (GRID_W - 8, 8)), ((0, 16), (GRID_W - 16, 16)), 1),)


def _na_column_bias(rpb):
    n_dr, n_dc = 2 * NA_WIN_ROWS - 1, 2 * NA_WIN_COLS - 1
    kinds = []
    for kind in (0, 1):
        sets = [cs for cs in _NA_COLUMN_SETS if cs[2] == kind]
        layouts = []
        for q_runs, k_runs, _ in sets:
            qc = np.concatenate([np.arange(c0, c0 + n) for c0, n in q_runs])[:, None]
            kc = np.concatenate([np.arange(c0, c0 + n) for c0, n in k_runs])[None, :]
            start = np.clip(qc - NA_WIN_COLS // 2, 0, GRID_W - NA_WIN_COLS)
            col_ok = (kc >= start) & (kc < start + NA_WIN_COLS)
            assert (col_ok.sum(axis=1) == NA_WIN_COLS).all()
            layouts.append((col_ok, np.clip(kc - qc + NA_WIN_COLS - 1, 0, n_dc - 1)))
        col_ok, dc = layouts[0]
        assert all(np.array_equal(col_ok, o) and np.array_equal(dc, d) for o, d in layouts)
        onehot = (dc.reshape(1, -1) == np.arange(n_dc)[:, None]).astype(np.float32)
        t = jnp.dot(rpb.astype(F32).reshape(N_HEADS * n_dr, n_dc), onehot, precision=lax.Precision.HIGHEST)
        t = jnp.where(col_ok[None, None], LOG2E * t.reshape(N_HEADS, n_dr, _NA_QCOLS, _NA_KCOLS), NEG_INF)
        kinds.append(jnp.tile(t, (1, 1, 1, LANES // _NA_KCOLS)).reshape(
            N_PAIRS, 2, n_dr, _NA_QCOLS, LANES))
    return jnp.stack(kinds, axis=1)


def _na_expand_table(cb_ref, tab_ref, rows):
    n_blk = rows // _NQ_ROWS
    nq, nk = _NQ_ROWS * _NA_QCOLS, _NK_ROWS * _NA_KCOLS
    sel = _na_row_select(rows)

    @pl.when(pl.program_id(1) == 0)
    def _():
        a_idx = lax.broadcasted_iota(jnp.int32, (nq, nk), 0) // _NA_QCOLS
        b_idx = lax.broadcasted_iota(jnp.int32, (nq, nk), 1) // _NA_KCOLS
        for c, i in enumerate((0, 1, n_blk - 1)):
            kr0 = min(max(_NQ_ROWS * i - NA_WIN_ROWS // 2, 0), rows - _NK_ROWS)
            qr = _NQ_ROWS * i + a_idx
            kr = kr0 + b_idx
            first = jnp.clip(qr - NA_WIN_ROWS // 2, 0, rows - NA_WIN_ROWS)
            dr = jnp.where((kr >= first) & (kr < first + NA_WIN_ROWS), kr - qr + NA_WIN_ROWS - 1, -1)
            for kind in range(cb_ref.shape[1]):
                for h in range(2):
                    out = jnp.full((nq, nk), NEG_INF, F32)
                    for r in sorted(set(int(s) for s in sel[c].reshape(-1)) - {-1}):
                        tile = jnp.tile(cb_ref[0, kind, h, r], (_NQ_ROWS, nk // LANES))
                        out = jnp.where(dr == r, tile, out)
                    tab_ref[kind, c, h * nq:(h + 1) * nq, :] = out


def _na_blocks(q_ref, k_ref, v_ref, tab_ref, o_ref, qf, of):
    S = q_ref.shape[2]
    rows = S // GRID_W
    n_blk = rows // _NQ_ROWS
    nq = _NQ_ROWS * _NA_QCOLS
    lane = lax.broadcasted_iota(jnp.int32, (nq, LANES), 1)
    qf[...] = q_ref[0, 0].astype(F32)
    for i in range(n_blk):
        kr0 = min(max(_NQ_ROWS * i - NA_WIN_ROWS // 2, 0), rows - _NK_ROWS)
        case = 0 if i == 0 else (2 if i == n_blk - 1 else 1)
        for q_runs, k_runs, kind in _NA_COLUMN_SETS:
            q_rows = [((_NQ_ROWS * i + a) * GRID_W + c0, n) for a in range(_NQ_ROWS) for c0, n in q_runs]
            k_rows = [((kr0 + b) * GRID_W + c0, n) for b in range(_NK_ROWS) for c0, n in k_runs]
            qb = jnp.concatenate([qf[r0:r0 + n, :] for r0, n in q_rows], axis=0).astype(BF16)
            kw = jnp.concatenate([k_ref[0, 0, r0:r0 + n, :] for r0, n in k_rows], axis=0)
            vw = jnp.concatenate([v_ref[0, 0, r0:r0 + n, :] for r0, n in k_rows], axis=0)
            m, l, pv = _attend(qb, kw, vw, tab_ref[kind, case], sums_on_mxu=True)
            o = pv / l
            o = jnp.where(lane < HEAD_DIM, o[:nq], o[nq:])
            at = 0
            for r0, n in q_rows:
                of[r0:r0 + n, :] = o[at:at + n]
                at += n
    o_ref[0] = of[...].astype(BF16)


def _attention_kernel(qd_ref, kd_ref, vd_ref, t12_ref, t3_ref, qn_ref, kn_ref, vn_ref, cb_ref,
                      od_ref, on_ref, tab_ref, qf, of, *dilated_scratch):
    _na_expand_table(cb_ref, tab_ref, qn_ref.shape[2] // GRID_W)
    _dilated_kernel(qd_ref, kd_ref, vd_ref, t12_ref, t3_ref, od_ref, *dilated_scratch)
    _na_blocks(qn_ref, kn_ref, vn_ref, tab_ref, on_ref, qf, of)


def _attention(qkv, t12, t3, col_bias):
    B, _, S, _ = qkv.shape
    f32_buf = pltpu.VMEM((S, LANES), F32)
    bf16_buf = pltpu.VMEM((S, LANES), BF16)
    slab = lambda off: pl.BlockSpec((1, 1, S, LANES), lambda p, b: (b, off + p, 0, 0))
    per_pair = lambda a: pl.BlockSpec((1,) + a.shape[1:], lambda p, b: (p,) + (0,) * (a.ndim - 1))
    out = pl.BlockSpec((1, S, LANES), lambda p, b: (b, 0, p))
    return pl.pallas_call(
        _attention_kernel,
        grid=(N_PAIRS, B),
        in_specs=[slab(0), slab(N_PAIRS), slab(2 * N_PAIRS), per_pair(t12), per_pair(t3),
                  slab(3 * N_PAIRS), slab(4 * N_PAIRS), slab(5 * N_PAIRS), per_pair(col_bias)],
        out_specs=[out, out],
        out_shape=[jax.ShapeDtypeStruct((B, S, WIDTH), BF16)] * 2,
        scratch_shapes=([pltpu.VMEM((col_bias.shape[1], 3, 2 * _NQ_ROWS * _NA_QCOLS,
                                     _NK_ROWS * _NA_KCOLS), F32)]
                        + [f32_buf] * 2 + [f32_buf] * 2 + [bf16_buf] * 6 + [f32_buf] * 10),
        compiler_params=pltpu.CompilerParams(
            dimension_semantics=("arbitrary", "arbitrary"), vmem_limit_bytes=VMEM_LIMIT),
        name="attention",
    )(qkv, qkv, qkv, t12, t3, qkv, qkv, qkv, col_bias)


_GROUP_LANE0 = N_EXPERTS
_ROUTE_GROUP_LANE = EXPERTS_PER_GROUP


def _route(logits):
    tm = logits.shape[0]
    n_rows = _GROUP_LANE0 + 2 * N_GROUPS
    lt = jnp.transpose(logits)[:n_rows, :]
    row_i = lax.broadcasted_iota(jnp.int32, (n_rows, tm), 0)
    row = row_i.astype(F32)
    big = float(LANES)
    is_group = (row_i >= _GROUP_LANE0) & (row_i < _GROUP_LANE0 + N_GROUPS)
    gl = jnp.where(is_group, lt, NEG_INF)
    gmax = jnp.max(gl, axis=0, keepdims=True)
    g_idx = jnp.min(jnp.where(is_group & (gl == gmax), row, big), axis=0, keepdims=True) - _GROUP_LANE0
    g_weight = 1.0 / jnp.sum(jnp.where(is_group, jnp.exp(gl - gmax), 0.0), axis=0, keepdims=True)
    in_group = (row_i < N_EXPERTS) & ((row_i // EXPERTS_PER_GROUP).astype(F32) == g_idx)
    el = jnp.where(in_group, lt, NEG_INF)
    v1 = jnp.max(el, axis=0, keepdims=True)
    i1 = jnp.min(jnp.where(in_group & (el == v1), row, big), axis=0, keepdims=True)
    rest = in_group & (row != i1)
    el2 = jnp.where(rest, lt, NEG_INF)
    v2 = jnp.max(el2, axis=0, keepdims=True)
    i2 = jnp.min(jnp.where(rest & (el2 == v2), row, big), axis=0, keepdims=True)
    e2 = jnp.exp(v2 - v1)
    w1 = g_weight / (1.0 + e2)
    w2 = g_weight * e2 / (1.0 + e2)
    base = g_idx * EXPERTS_PER_GROUP
    rec_i = lax.broadcasted_iota(jnp.int32, (LANES, tm), 0)
    rec = rec_i.astype(F32)
    record = jnp.where(rec == i1 - base, w1,
                       jnp.where(rec == i2 - base, w2,
                                 jnp.where(rec_i == _ROUTE_GROUP_LANE, g_idx, 0.0)))
    return jnp.transpose(record)


def _outproj_kernel(x_ref, yd_ref, yn_ref, gd_ref, gn_ref, wo_ref, gf_ref, wr_ref,
                    br_ref, h_ref, hn_ref, comb_ref):
    yd = _rms(yd_ref[...].astype(F32), gd_ref[...]).astype(BF16)
    yn = _rms(yn_ref[...].astype(F32), gn_ref[...]).astype(BF16)
    h = x_ref[...] + jnp.dot(jnp.concatenate([yd, yn], axis=-1), wo_ref[...], preferred_element_type=F32)
    h_ref[...] = h
    hn = _rms(h, gf_ref[...]).astype(BF16)
    hn_ref[...] = hn
    logits = jnp.dot(hn, wr_ref[...], preferred_element_type=F32) + br_ref[...]
    comb_ref[...] = _route(logits)


def _outproj(x2, yd2, yn2, gd, gn, wo, gf, wr, br, tm):
    N, D = x2.shape
    row = lambda w: pl.BlockSpec((tm, w), lambda i: (i, 0))
    full = lambda a, b: pl.BlockSpec((a, b), lambda i: (0, 0))
    return pl.pallas_call(
        _outproj_kernel,
        grid=(N // tm,),
        in_specs=[row(D), row(WIDTH), row(WIDTH), full(1, WIDTH), full(1, WIDTH), full(2 * WIDTH, D),
                  full(1, D), full(D, LANES), full(1, LANES)],
        out_specs=[row(D), row(D), row(LANES)],
        out_shape=[jax.ShapeDtypeStruct((N, D), F32), jax.ShapeDtypeStruct((N, D), BF16),
                   jax.ShapeDtypeStruct((N, LANES), F32)],
        compiler_params=pltpu.CompilerParams(
            dimension_semantics=("arbitrary",), vmem_limit_bytes=VMEM_LIMIT),
        name="outproj_route",
    )(x2, yd2, yn2, gd, gn, wo, gf, wr, br)


_MOE_TILE = 1024
_MOE_CHUNK = 144
_MOE_NCHUNK = _MOE_TILE // _MOE_CHUNK + N_GROUPS
_GROUP_WIDTH = EXPERTS_PER_GROUP * D_EXPERT
_ROUTE_PIECE = 8
_MOE_VMEM_LIMIT = 58 * 1024 * 1024


def _moe_kernel(hn_ref, route_ref, h_ref, wg_ref, wu_ref, wd_ref, gfin_ref, y_ref, ys_ref, xcat_ref):
    T, C = _MOE_TILE, _MOE_CHUNK
    route = route_ref[...]
    lane = lax.broadcasted_iota(jnp.int32, (T, LANES), 1)
    gid = jnp.sum(jnp.where(lane == _ROUTE_GROUP_LANE, route, 0.0), axis=-1, keepdims=True)
    onehot = jnp.where((lane < N_GROUPS) & (lane.astype(F32) == gid), 1.0, 0.0)

    before = (lax.broadcasted_iota(jnp.int32, (LANES, LANES), 1)
              < lax.broadcasted_iota(jnp.int32, (LANES, LANES), 0)).astype(BF16)
    count = jnp.zeros((1, LANES), F32)
    ranks = []
    for blk in range(T // LANES):
        oh = onehot[blk * LANES:(blk + 1) * LANES]
        ranks.append(jnp.dot(before, oh.astype(BF16), preferred_element_type=F32) + count)
        count = count + jnp.sum(oh, axis=0, keepdims=True)
    rank = jnp.concatenate(ranks, axis=0)
    nchunk = jnp.floor((count + (C - 1)) * (1.0 / C)).astype(jnp.int32)
    off1 = nchunk[0, 0]
    off2 = off1 + nchunk[0, 1]
    off3 = off2 + nchunk[0, 2]
    n_used = off3 + nchunk[0, 3]
    start = jnp.where(lane == 1, off1, jnp.where(lane == 2, off2, jnp.where(lane == 3, off3, 0)))
    pos = jnp.sum(onehot * (rank + (start * C).astype(F32)), axis=-1, keepdims=True)
    pos_i = pos.astype(jnp.int32)
    pos_row = jnp.transpose(jnp.broadcast_to(pos, (T, LANES)))[0:1, :].astype(jnp.int32)

    r_hi = route.astype(BF16).astype(F32)
    r_mid = (route - r_hi).astype(BF16).astype(F32)
    r_lo = (route - r_hi - r_mid).astype(BF16).astype(F32)
    packed = r_hi + pltpu.roll(r_mid, _ROUTE_PIECE, axis=1) + pltpu.roll(r_lo, 2 * _ROUTE_PIECE, axis=1)
    xcat_ref[:, :hn_ref.shape[1]] = hn_ref[...]
    xcat_ref[:, hn_ref.shape[1]:] = packed.astype(BF16)

    def chunk_body(c, carry):
        g = ((c >= off1).astype(jnp.int32) + (c >= off2).astype(jnp.int32)
             + (c >= off3).astype(jnp.int32))
        row0 = pl.multiple_of(c * C, 16)
        sel = (pos_row == row0 + lax.broadcasted_iota(jnp.int32, (C, T), 0)).astype(BF16)
        xr = jnp.dot(sel, xcat_ref[...], preferred_element_type=F32)
        xs = xr[:, :hn_ref.shape[1]].astype(BF16)
        r3 = xr[:, hn_ref.shape[1]:]
        r = (r3 + pltpu.roll(r3, LANES - _ROUTE_PIECE, axis=1)
             + pltpu.roll(r3, LANES - 2 * _ROUTE_PIECE, axis=1))
        clane = lax.broadcasted_iota(jnp.int32, (C, LANES), 1)
        parts = []
        for j in range(EXPERTS_PER_GROUP):
            e = g * EXPERTS_PER_GROUP + j
            gate = jnp.dot(xs, wg_ref[e], preferred_element_type=F32)
            up = jnp.dot(xs, wu_ref[e], preferred_element_type=F32)
            wj = jnp.sum(jnp.where(clane == j, r, 0.0), axis=-1, keepdims=True)
            parts.append((gate / (1.0 + jnp.exp(-gate))) * up * wj)
        act = jnp.concatenate(parts, axis=-1).astype(BF16)
        ys_ref[pl.ds(row0, C), :] = jnp.dot(act, wd_ref[g], preferred_element_type=F32).astype(BF16)
        return carry

    lax.fori_loop(0, n_used, chunk_body, 0)

    def zero_body(c, carry):
        ys_ref[pl.ds(pl.multiple_of(c * C, 16), C), :] = jnp.zeros((C, ys_ref.shape[1]), BF16)
        return carry

    lax.fori_loop(n_used, _MOE_NCHUNK, zero_body, 0)

    n_main = N_GROUPS * (-(-(T // N_GROUPS) // C))
    n_tail = _MOE_NCHUNK - n_main
    back = (lax.broadcasted_iota(jnp.int32, (T, n_main * C), 1) == pos_i).astype(BF16)
    y_ref[...] = h_ref[...] + jnp.dot(back, ys_ref[:n_main * C, :], preferred_element_type=F32)

    @pl.when(n_used > n_main)
    def _():
        tail = (lax.broadcasted_iota(jnp.int32, (T, n_tail * C), 1) == pos_i - n_main * C).astype(BF16)
        y_ref[...] += jnp.dot(tail, ys_ref[n_main * C:, :], preferred_element_type=F32)

    y_ref[...] = _rms(y_ref[...], gfin_ref[...])


def _moe(hn, route, h, wg, wu, wd, gfin):
    N, D = hn.shape
    T = _MOE_TILE
    row = lambda w: pl.BlockSpec((T, w), lambda i: (i, 0))
    whole = lambda a: pl.BlockSpec(a.shape, lambda i: (0,) * a.ndim)
    return pl.pallas_call(
        _moe_kernel,
        grid=(N // T,),
        in_specs=[row(D), row(LANES), row(D),
                  whole(wg), whole(wu), whole(wd), whole(gfin)],
        out_specs=row(D),
        out_shape=jax.ShapeDtypeStruct((N, D), F32),
        scratch_shapes=[pltpu.VMEM((_MOE_NCHUNK * _MOE_CHUNK, D), BF16), pltpu.VMEM((T, D + LANES), BF16)],
        compiler_params=pltpu.CompilerParams(
            dimension_semantics=("arbitrary",), vmem_limit_bytes=_MOE_VMEM_LIMIT),
        name="moe_grouped",
    )(hn, route, h, wg, wu, wd, gfin)


def kernel(x, norm_mix_g, w_in, rpb, g_out_dil, g_out_na, w_out, norm_ffn_g, w_group, b_group,
           w_router, b_router, w_gate, w_up, w_down, norm_final_g):
    B, S, D = x.shape
    N = B * S
    depth = w_in.shape[0]
    assert depth == 1 and D == D_MODEL and S % (16 * _DQ) == 0

    t12, t3 = _dilated_tables()

    layer = 0
    col_scale = np.ones((6, WIDTH), np.float32)
    col_scale[0] = col_scale[3] = LOG2E * HEAD_DIM ** -0.5
    qkv = _inproj(x, norm_mix_g[layer].reshape(1, D), w_in[layer], jnp.asarray(col_scale.reshape(1, -1)),
                  tm=512)

    y_dil, y_na = _attention(qkv, jnp.asarray(t12), jnp.asarray(t3), _na_column_bias(rpb[layer]))

    n_route = N_EXPERTS + N_GROUPS
    w_r = jnp.concatenate([w_router[layer], w_group[layer], jnp.zeros((D, LANES - n_route), F32)], axis=1)
    b_r = jnp.concatenate([b_router[layer], b_group[layer], jnp.zeros((LANES - n_route,), F32)]).reshape(1, LANES)

    h, hn, route = _outproj(
        x.reshape(N, D), y_dil.reshape(N, WIDTH), y_na.reshape(N, WIDTH),
        g_out_dil[layer].reshape(1, WIDTH), g_out_na[layer].reshape(1, WIDTH),
        w_out[layer].astype(BF16), norm_ffn_g[layer].reshape(1, D), w_r.astype(BF16), b_r, tm=512)

    wd = w_down[layer].astype(BF16).reshape(N_GROUPS, _GROUP_WIDTH, D)
    y = _moe(hn, route, h, w_gate[layer].astype(BF16), w_up[layer].astype(BF16), wd,
             norm_final_g.reshape(1, D))
    return y.reshape(B, S, D)
```

```python
import numpy as np
import jax
import jax.numpy as jnp
from jax import lax
from jax.experimental import pallas as pl
from jax.experimental.pallas import tpu as pltpu

D_MODEL = 1024
HEAD_DIM = 64
N_HEADS = 8
N_PAIRS = N_HEADS // 2
WIDTH = N_HEADS * HEAD_DIM
N_SLABS = 6 * N_PAIRS
DIL_PATTERNS = ((128, 1), (512, 4), (2048, 16))
DIL_RADIUS = 64
GRID_W = 64
NA_WIN_ROWS = 8
NA_WIN_COLS = 16
N_GROUPS = 4
EXPERTS_PER_GROUP = 4
N_EXPERTS = 16
D_EXPERT = 256
RMS_EPS = 1e-6
NEG_INF = -1e30
LOG2E = 1.4426950408889634

LANES = 128
VMEM_LIMIT = 48 * 1024 * 1024

F32 = jnp.float32
BF16 = jnp.bfloat16


def _rms(x, gain):
    return x * lax.rsqrt(jnp.mean(x * x, axis=-1, keepdims=True) + RMS_EPS) * gain


def _inproj_kernel(x_ref, g_ref, w_ref, s_ref, o_ref, wb_ref):
    @pl.when((pl.program_id(0) == 0) & (pl.program_id(1) == 0))
    def _():
        for j in range(N_SLABS):
            cols = slice(j * LANES, (j + 1) * LANES)
            wb_ref[:, cols] = (w_ref[:, cols] * s_ref[:, cols]).astype(BF16)

    xn = _rms(x_ref[0], g_ref[...]).astype(BF16)
    chunk = 4 * LANES
    for c in range(N_SLABS * LANES // chunk):
        acc = jnp.dot(xn, wb_ref[:, c * chunk:(c + 1) * chunk], preferred_element_type=F32)
        for j in range(chunk // LANES):
            o_ref[0, c * (chunk // LANES) + j] = acc[:, j * LANES:(j + 1) * LANES].astype(BF16)


def _inproj(x, gain, w, col_scale, tm):
    B, S, D = x.shape
    return pl.pallas_call(
        _inproj_kernel,
        grid=(B, S // tm),
        in_specs=[
            pl.BlockSpec((1, tm, D), lambda b, i: (b, i, 0)),
            pl.BlockSpec((1, D), lambda b, i: (0, 0)),
            pl.BlockSpec((D, N_SLABS * LANES), lambda b, i: (0, 0)),
            pl.BlockSpec((1, N_SLABS * LANES), lambda b, i: (0, 0)),
        ],
        out_specs=pl.BlockSpec((1, N_SLABS, tm, LANES), lambda b, i: (b, 0, i, 0)),
        out_shape=jax.ShapeDtypeStruct((B, N_SLABS, S, LANES), BF16),
        scratch_shapes=[pltpu.VMEM((D, N_SLABS * LANES), BF16)],
        compiler_params=pltpu.CompilerParams(
            dimension_semantics=("arbitrary", "arbitrary"), vmem_limit_bytes=VMEM_LIMIT),
        name="inproj",
    )(x, gain, w, col_scale)


def _attend(qb, kw, vw, bias, sums_on_mxu):
    lane = lax.broadcasted_iota(jnp.int32, qb.shape, 1)
    zero = jnp.zeros_like(qb)
    qq = jnp.concatenate([jnp.where(lane < HEAD_DIM, qb, zero),
                          jnp.where(lane >= HEAD_DIM, qb, zero)], axis=0)
    s = lax.dot_general(qq, kw, (((1,), (1,)), ((), ())), preferred_element_type=F32) + bias
    m = jnp.max(s, axis=-1, keepdims=True)
    p = jnp.exp2(s - m)
    if not sums_on_mxu:
        l = jnp.sum(p, axis=-1, keepdims=True)
        return m, l, jnp.dot(p.astype(BF16), vw, preferred_element_type=F32)
    pv = jnp.dot(p.astype(BF16), jnp.concatenate([vw, jnp.ones_like(vw)], axis=1),
                 preferred_element_type=F32)
    return m, pv[:, LANES:], pv[:, :LANES]


def _merge_heads(top, bottom, q):
    lane = lax.broadcasted_iota(jnp.int32, (q, LANES), 1)
    return jnp.where(lane < HEAD_DIM, jnp.broadcast_to(top, (q, LANES)),
                     jnp.broadcast_to(bottom, (q, LANES)))


_DQ = 128
_DW = 256
_UNROLL = 16


def _dilated_tables():
    slopes = 2.0 ** (-(np.arange(N_HEADS) + 1.0))
    q = np.arange(_DQ)[:, None]
    t12 = np.zeros((N_PAIRS, 2, 3, 2 * _DQ, _DW), np.float32)
    k = np.arange(_DW)[None, :]
    for pat, dil in enumerate((1, 4)):
        for case, off in enumerate((0, _DW // 4, _DW // 2)):
            delta = np.abs(k - (q + off))
            for h in range(N_HEADS):
                tab = np.where(delta <= DIL_RADIUS, -LOG2E * slopes[h] * dil * delta, NEG_INF)
                t12[h // 2, pat, case, (h % 2) * _DQ:(h % 2 + 1) * _DQ] = tab
    t3 = np.zeros((N_PAIRS, 2 * _DQ, _DQ), np.float32)
    delta = np.abs(np.arange(_DQ)[None, :] - q)
    for h in range(N_HEADS):
        t3[h // 2, (h % 2) * _DQ:(h % 2 + 1) * _DQ] = np.where(
            delta <= DIL_RADIUS, -LOG2E * slopes[h] * 16 * delta, NEG_INF)
    return t12, t3


def _dilated_kernel(q_ref, k_ref, v_ref, t12_ref, t3_ref, o_ref,
                    tmp, tmp4, q4, k4, v4, q16, k16, v16,
                    m1, l1, a1, m2, l2, a2, m3, l3, a3, onat):
    S = tmp.shape[0]
    L4, L16 = S // 4, S // 16

    for src, d4, d16 in ((q_ref, q4, q16), (k_ref, k4, k16), (v_ref, v4, v16)):
        tmp[...] = src[0, 0].astype(F32)
        for r in range(4):
            sub = tmp[pl.ds(r, L4, stride=4), :]
            tmp4[r * L4:(r + 1) * L4, :] = sub
            d4[r * L4:(r + 1) * L4, :] = sub.astype(BF16)
        for r16 in range(16):
            r4, c4 = r16 % 4, r16 // 4
            d16[r16 * L16:(r16 + 1) * L16, :] = tmp4[pl.ds(r4 * L4 + c4, L16, stride=4), :].astype(BF16)

    def block(qb, kw, vw, bias, m_ref, l_ref, a_ref, row):
        m, l, pv = _attend(qb, kw, vw, bias, sums_on_mxu=False)
        m_ref[pl.ds(row, _DQ), :] = _merge_heads(m[:_DQ], m[_DQ:], _DQ)
        l_ref[pl.ds(row, _DQ), :] = _merge_heads(l[:_DQ], l[_DQ:], _DQ)
        a_ref[pl.ds(row, _DQ), :] = _merge_heads(pv[:_DQ], pv[_DQ:], _DQ)

    def case_of(blk, n_blk):
        return jnp.where(blk == 0, 0, jnp.where(blk == n_blk - 1, 2, 1))

    n1 = S // _DQ

    def p1_body(blk, carry):
        t0 = pl.multiple_of(blk * _DQ, _DQ)
        ws = pl.multiple_of(jnp.clip(t0 - DIL_RADIUS, 0, S - _DW), DIL_RADIUS)
        block(q_ref[0, 0, pl.ds(t0, _DQ), :], k_ref[0, 0, pl.ds(ws, _DW), :],
              v_ref[0, 0, pl.ds(ws, _DW), :], t12_ref[0, 0, case_of(blk, n1)], m1, l1, a1, t0)
        return carry

    lax.fori_loop(0, n1, p1_body, 0, unroll=_UNROLL)

    n2 = L4 // _DQ

    def p2_body(j, carry):
        r = j // n2
        blk = j % n2
        l0 = blk * _DQ
        ws = jnp.clip(l0 - DIL_RADIUS, 0, L4 - _DW)
        row = pl.multiple_of(r * L4 + l0, _DQ)
        krow = pl.multiple_of(r * L4 + ws, DIL_RADIUS)
        block(q4[pl.ds(row, _DQ), :], k4[pl.ds(krow, _DW), :], v4[pl.ds(krow, _DW), :],
              t12_ref[0, 1, case_of(blk, n2)], m2, l2, a2, row)
        return carry

    lax.fori_loop(0, 4 * n2, p2_body, 0, unroll=_UNROLL)

    def p3_body(r, carry):
        row = pl.multiple_of(r * L16, L16)
        block(q16[pl.ds(row, L16), :], k16[pl.ds(row, L16), :], v16[pl.ds(row, L16), :],
              t3_ref[0], m3, l3, a3, row)
        return carry

    lax.fori_loop(0, 16, p3_body, 0, unroll=_UNROLL)

    for r16 in range(16):
        r4, c4 = r16 % 4, r16 // 4
        via4 = pl.ds(r4 * L4 + c4, L16, stride=4)
        via16 = pl.ds(r16 * L16, L16)
        mb, mc = m2[via4, :], m3[via16, :]
        mx = jnp.maximum(mb, mc)
        wb, wc = jnp.exp2(mb - mx), jnp.exp2(mc - mx)
        l2[via4, :] = wb * l2[via4, :] + wc * l3[via16, :]
        a2[via4, :] = wb * a2[via4, :] + wc * a3[via16, :]
        m2[via4, :] = mx
    for r4 in range(4):
        for part in range(L4 // _DQ):
            nat = pl.ds(r4 + 4 * _DQ * part, _DQ, stride=4)
            via4 = pl.ds(r4 * L4 + _DQ * part, _DQ)
            ma, mb = m1[nat, :], m2[via4, :]
            mx = jnp.maximum(ma, mb)
            wa, wb = jnp.exp2(ma - mx), jnp.exp2(mb - mx)
            den = wa * l1[nat, :] + wb * l2[via4, :]
            num = wa * a1[nat, :] + wb * a2[via4, :]
            onat[nat, :] = num / den
    o_ref[0] = onat[...].astype(BF16)


_NQ_ROWS = 4
_NK_ROWS = 12


def _na_row_select(rows):
    n_blk = rows // _NQ_ROWS
    sel = np.full((n_blk, _NQ_ROWS, _NK_ROWS), -1, np.int64)
    for i in range(n_blk):
        kr0 = min(max(_NQ_ROWS * i - NA_WIN_ROWS // 2, 0), rows - _NK_ROWS)
        for a in range(_NQ_ROWS):
            qr = _NQ_ROWS * i + a
            rs = min(max(qr - NA_WIN_ROWS // 2, 0), rows - NA_WIN_ROWS)
            for b in range(_NK_ROWS):
                kr = kr0 + b
                if rs <= kr < rs + NA_WIN_ROWS:
                    sel[i, a, b] = kr - qr + NA_WIN_ROWS - 1
    for i in range(2, n_blk - 1):
        assert np.array_equal(sel[1], sel[i])
    return sel[[0, 1, n_blk - 1]]


_NA_QCOLS = 16
_NA_KCOLS = 32
_NA_COLUMN_SETS = tuple(
    (((8 + 16 * j, 16),), ((16 * j, 32),), 0) for j in range(3)
) + ((((0, 8), (GRID_W - 8, 8)), ((0, 16), (GRID_W - 16, 16)), 1),)


def _na_column_bias(rpb):
    n_dr, n_dc = 2 * NA_WIN_ROWS - 1, 2 * NA_WIN_COLS - 1
    kinds = []
    for kind in (0, 1):
        sets = [cs for cs in _NA_COLUMN_SETS if cs[2] == kind]
        layouts = []
        for q_runs, k_runs, _ in sets:
            qc = np.concatenate([np.arange(c0, c0 + n) for c0, n in q_runs])[:, None]
            kc = np.concatenate([np.arange(c0, c0 + n) for c0, n in k_runs])[None, :]
            start = np.clip(qc - NA_WIN_COLS // 2, 0, GRID_W - NA_WIN_COLS)
            col_ok = (kc >= start) & (kc < start + NA_WIN_COLS)
            assert (col_ok.sum(axis=1) == NA_WIN_COLS).all()
            layouts.append((col_ok, np.clip(kc - qc + NA_WIN_COLS - 1, 0, n_dc - 1)))
        col_ok, dc = layouts[0]
        assert all(np.array_equal(col_ok, o) and np.array_equal(dc, d) for o, d in layouts)
        onehot = (dc.reshape(1, -1) == np.arange(n_dc)[:, None]).astype(np.float32)
        t = jnp.dot(rpb.astype(F32).reshape(N_HEADS * n_dr, n_dc), onehot, precision=lax.Precision.HIGHEST)
        t = jnp.where(col_ok[None, None], LOG2E * t.reshape(N_HEADS, n_dr, _NA_QCOLS, _NA_KCOLS), NEG_INF)
        kinds.append(jnp.tile(t, (1, 1, 1, LANES // _NA_KCOLS)).reshape(
            N_PAIRS, 2, n_dr, _NA_QCOLS, LANES))
    return jnp.stack(kinds, axis=1)


def _na_expand_table(cb_ref, tab_ref, rows):
    n_blk = rows // _NQ_ROWS
    nq, nk = _NQ_ROWS * _NA_QCOLS, _NK_ROWS * _NA_KCOLS
    sel = _na_row_select(rows)

    @pl.when(pl.program_id(1) == 0)
    def _():
        a_idx = lax.broadcasted_iota(jnp.int32, (nq, nk), 0) // _NA_QCOLS
        b_idx = lax.broadcasted_iota(jnp.int32, (nq, nk), 1) // _NA_KCOLS
        for c, i in enumerate((0, 1, n_blk - 1)):
            kr0 = min(max(_NQ_ROWS * i - NA_WIN_ROWS // 2, 0), rows - _NK_ROWS)
            qr = _NQ_ROWS * i + a_idx
            kr = kr0 + b_idx
            first = jnp.clip(qr - NA_WIN_ROWS // 2, 0, rows - NA_WIN_ROWS)
            dr = jnp.where((kr >= first) & (kr < first + NA_WIN_ROWS), kr - qr + NA_WIN_ROWS - 1, -1)
            for kind in range(cb_ref.shape[1]):
                for h in range(2):
                    out = jnp.full((nq, nk), NEG_INF, F32)
                    for r in sorted(set(int(s) for s in sel[c].reshape(-1)) - {-1}):
                        tile = jnp.tile(cb_ref[0, kind, h, r], (_NQ_ROWS, nk // LANES))
                        out = jnp.where(dr == r, tile, out)
                    tab_ref[kind, c, h * nq:(h + 1) * nq, :] = out


def _na_blocks(q_ref, k_ref, v_ref, tab_ref, o_ref, qf, of):
    S = q_ref.shape[2]
    rows = S // GRID_W
    n_blk = rows // _NQ_ROWS
    nq = _NQ_ROWS * _NA_QCOLS
    lane = lax.broadcasted_iota(jnp.int32, (nq, LANES), 1)
    qf[...] = q_ref[0, 0].astype(F32)
    for i in range(n_blk):
        kr0 = min(max(_NQ_ROWS * i - NA_WIN_ROWS // 2, 0), rows - _NK_ROWS)
        case = 0 if i == 0 else (2 if i == n_blk - 1 else 1)
        for q_runs, k_runs, kind in _NA_COLUMN_SETS:
            q_rows = [((_NQ_ROWS * i + a) * GRID_W + c0, n) for a in range(_NQ_ROWS) for c0, n in q_runs]
            k_rows = [((kr0 + b) * GRID_W + c0, n) for b in range(_NK_ROWS) for c0, n in k_runs]
            qb = jnp.concatenate([qf[r0:r0 + n, :] for r0, n in q_rows], axis=0).astype(BF16)
            kw = jnp.concatenate([k_ref[0, 0, r0:r0 + n, :] for r0, n in k_rows], axis=0)
            vw = jnp.concatenate([v_ref[0, 0, r0:r0 + n, :] for r0, n in k_rows], axis=0)
            m, l, pv = _attend(qb, kw, vw, tab_ref[kind, case], sums_on_mxu=True)
            o = pv / l
            o = jnp.where(lane < HEAD_DIM, o[:nq], o[nq:])
            at = 0
            for r0, n in q_rows:
                of[r0:r0 + n, :] = o[at:at + n]
                at += n
    o_ref[0] = of[...].astype(BF16)


def _attention_kernel(qd_ref, kd_ref, vd_ref, t12_ref, t3_ref, qn_ref, kn_ref, vn_ref, cb_ref,
                      od_ref, on_ref, tab_ref, qf, of, *dilated_scratch):
    _na_expand_table(cb_ref, tab_ref, qn_ref.shape[2] // GRID_W)
    _dilated_kernel(qd_ref, kd_ref, vd_ref, t12_ref, t3_ref, od_ref, *dilated_scratch)
    _na_blocks(qn_ref, kn_ref, vn_ref, tab_ref, on_ref, qf, of)


def _attention(qkv, t12, t3, col_bias):
    B, _, S, _ = qkv.shape
    f32_buf = pltpu.VMEM((S, LANES), F32)
    bf16_buf = pltpu.VMEM((S, LANES), BF16)
    slab = lambda off: pl.BlockSpec((1, 1, S, LANES), lambda p, b: (b, off + p, 0, 0))
    per_pair = lambda a: pl.BlockSpec((1,) + a.shape[1:], lambda p, b: (p,) + (0,) * (a.ndim - 1))
    out = pl.BlockSpec((1, S, LANES), lambda p, b: (b, 0, p))
    return pl.pallas_call(
        _attention_kernel,
        grid=(N_PAIRS, B),
        in_specs=[slab(0), slab(N_PAIRS), slab(2 * N_PAIRS), per_pair(t12), per_pair(t3),
                  slab(3 * N_PAIRS), slab(4 * N_PAIRS), slab(5 * N_PAIRS), per_pair(col_bias)],
        out_specs=[out, out],
        out_shape=[jax.ShapeDtypeStruct((B, S, WIDTH), BF16)] * 2,
        scratch_shapes=([pltpu.VMEM((col_bias.shape[1], 3, 2 * _NQ_ROWS * _NA_QCOLS,
                                     _NK_ROWS * _NA_KCOLS), F32)]
                        + [f32_buf] * 2 + [f32_buf] * 2 + [bf16_buf] * 6 + [f32_buf] * 10),
        compiler_params=pltpu.CompilerParams(
            dimension_semantics=("arbitrary", "arbitrary"), vmem_limit_bytes=VMEM_LIMIT),
        name="attention",
    )(qkv, qkv, qkv, t12, t3, qkv, qkv, qkv, col_bias)


_GROUP_LANE0 = N_EXPERTS
_ROUTE_GROUP_LANE = EXPERTS_PER_GROUP


def _route(logits):
    tm = logits.shape[0]
    n_rows = _GROUP_LANE0 + 2 * N_GROUPS
    lt = jnp.transpose(logits)[:n_rows, :]
    row_i = lax.broadcasted_iota(jnp.int32, (n_rows, tm), 0)
    row = row_i.astype(F32)
    big = float(LANES)
    is_group = (row_i >= _GROUP_LANE0) & (row_i < _GROUP_LANE0 + N_GROUPS)
    gl = jnp.where(is_group, lt, NEG_INF)
    gmax = jnp.max(gl, axis=0, keepdims=True)
    g_idx = jnp.min(jnp.where(is_group & (gl == gmax), row, big), axis=0, keepdims=True) - _GROUP_LANE0
    g_weight = 1.0 / jnp.sum(jnp.where(is_group, jnp.exp(gl - gmax), 0.0), axis=0, keepdims=True)
    in_group = (row_i < N_EXPERTS) & ((row_i // EXPERTS_PER_GROUP).astype(F32) == g_idx)
    el = jnp.where(in_group, lt, NEG_INF)
    v1 = jnp.max(el, axis=0, keepdims=True)
    i1 = jnp.min(jnp.where(in_group & (el == v1), row, big), axis=0, keepdims=True)
    rest = in_group & (row != i1)
    el2 = jnp.where(rest, lt, NEG_INF)
    v2 = jnp.max(el2, axis=0, keepdims=True)
    i2 = jnp.min(jnp.where(rest & (el2 == v2), row, big), axis=0, keepdims=True)
    e2 = jnp.exp(v2 - v1)
    w1 = g_weight / (1.0 + e2)
    w2 = g_weight * e2 / (1.0 + e2)
    base = g_idx * EXPERTS_PER_GROUP
    rec_i = lax.broadcasted_iota(jnp.int32, (LANES, tm), 0)
    rec = rec_i.astype(F32)
    record = jnp.where(rec == i1 - base, w1,
                       jnp.where(rec == i2 - base, w2,
                                 jnp.where(rec_i == _ROUTE_GROUP_LANE, g_idx, 0.0)))
    return jnp.transpose(record)


def _outproj_kernel(x_ref, yd_ref, yn_ref, gd_ref, gn_ref, wo_ref, gf_ref, wr_ref,
                    br_ref, h_ref, hn_ref, comb_ref):
    yd = _rms(yd_ref[...].astype(F32), gd_ref[...]).astype(BF16)
    yn = _rms(yn_ref[...].astype(F32), gn_ref[...]).astype(BF16)
    h = x_ref[...] + jnp.dot(jnp.concatenate([yd, yn], axis=-1), wo_ref[...], preferred_element_type=F32)
    h_ref[...] = h
    hn = _rms(h, gf_ref[...]).astype(BF16)
    hn_ref[...] = hn
    logits = jnp.dot(hn, wr_ref[...], preferred_element_type=F32) + br_ref[...]
    comb_ref[...] = _route(logits)


def _outproj(x2, yd2, yn2, gd, gn, wo, gf, wr, br, tm):
    N, D = x2.shape
    row = lambda w: pl.BlockSpec((tm, w), lambda i: (i, 0))
    full = lambda a, b: pl.BlockSpec((a, b), lambda i: (0, 0))
    return pl.pallas_call(
        _outproj_kernel,
        grid=(N // tm,),
        in_specs=[row(D), row(WIDTH), row(WIDTH), full(1, WIDTH), full(1, WIDTH), full(2 * WIDTH, D),
                  full(1, D), full(D, LANES), full(1, LANES)],
        out_specs=[row(D), row(D), row(LANES)],
        out_shape=[jax.ShapeDtypeStruct((N, D), F32), jax.ShapeDtypeStruct((N, D), BF16),
                   jax.ShapeDtypeStruct((N, LANES), F32)],
        compiler_params=pltpu.CompilerParams(
            dimension_semantics=("arbitrary",), vmem_limit_bytes=VMEM_LIMIT),
        name="outproj_route",
    )(x2, yd2, yn2, gd, gn, wo, gf, wr, br)


_MOE_TILE = 512
_MOE_CHUNK = 144
_MOE_NCHUNK = _MOE_TILE // _MOE_CHUNK + N_GROUPS
_GROUP_WIDTH = EXPERTS_PER_GROUP * D_EXPERT
_ROUTE_PIECE = 8
_MOE_VMEM_LIMIT = 58 * 1024 * 1024


def _moe_kernel(hn_ref, route_ref, h_ref, wg_ref, wu_ref, wd_ref, gfin_ref, y_ref, ys_ref, xcat_ref):
    T, C = _MOE_TILE, _MOE_CHUNK
    route = route_ref[...]
    lane = lax.broadcasted_iota(jnp.int32, (T, LANES), 1)
    gid = jnp.sum(jnp.where(lane == _ROUTE_GROUP_LANE, route, 0.0), axis=-1, keepdims=True)
    onehot = jnp.where((lane < N_GROUPS) & (lane.astype(F32) == gid), 1.0, 0.0)

    before = (lax.broadcasted_iota(jnp.int32, (LANES, LANES), 1)
              < lax.broadcasted_iota(jnp.int32, (LANES, LANES), 0)).astype(BF16)
    count = jnp.zeros((1, LANES), F32)
    ranks = []
    for blk in range(T // LANES):
        oh = onehot[blk * LANES:(blk + 1) * LANES]
        ranks.append(jnp.dot(before, oh.astype(BF16), preferred_element_type=F32) + count)
        count = count + jnp.sum(oh, axis=0, keepdims=True)
    rank = jnp.concatenate(ranks, axis=0)
    nchunk = jnp.floor((count + (C - 1)) * (1.0 / C)).astype(jnp.int32)
    off1 = nchunk[0, 0]
    off2 = off1 + nchunk[0, 1]
    off3 = off2 + nchunk[0, 2]
    n_used = off3 + nchunk[0, 3]
    start = jnp.where(lane == 1, off1, jnp.where(lane == 2, off2, jnp.where(lane == 3, off3, 0)))
    pos = jnp.sum(onehot * (rank + (start * C).astype(F32)), axis=-1, keepdims=True)
    pos_i = pos.astype(jnp.int32)
    pos_row = jnp.transpose(jnp.broadcast_to(pos, (T, LANES)))[0:1, :].astype(jnp.int32)

    r_hi = route.astype(BF16).astype(F32)
    r_mid = (route - r_hi).astype(BF16).astype(F32)
    r_lo = (route - r_hi - r_mid).astype(BF16).astype(F32)
    packed = r_hi + pltpu.roll(r_mid, _ROUTE_PIECE, axis=1) + pltpu.roll(r_lo, 2 * _ROUTE_PIECE, axis=1)
    xcat_ref[:, :hn_ref.shape[1]] = hn_ref[...]
    xcat_ref[:, hn_ref.shape[1]:] = packed.astype(BF16)

    def chunk_body(c, carry):
        g = ((c >= off1).astype(jnp.int32) + (c >= off2).astype(jnp.int32)
             + (c >= off3).astype(jnp.int32))
        row0 = pl.multiple_of(c * C, 16)
        sel = (pos_row == row0 + lax.broadcasted_iota(jnp.int32, (C, T), 0)).astype(BF16)
        xr = jnp.dot(sel, xcat_ref[...], preferred_element_type=F32)
        xs = xr[:, :hn_ref.shape[1]].astype(BF16)
        r3 = xr[:, hn_ref.shape[1]:]
        r = (r3 + pltpu.roll(r3, LANES - _ROUTE_PIECE, axis=1)
             + pltpu.roll(r3, LANES - 2 * _ROUTE_PIECE, axis=1))
        clane = lax.broadcasted_iota(jnp.int32, (C, LANES), 1)
        parts = []
        for j in range(EXPERTS_PER_GROUP):
            e = g * EXPERTS_PER_GROUP + j
            gate = jnp.dot(xs, wg_ref[e], preferred_element_type=F32)
            up = jnp.dot(xs, wu_ref[e], preferred_element_type=F32)
            wj = jnp.sum(jnp.where(clane == j, r, 0.0), axis=-1, keepdims=True)
            parts.append((gate / (1.0 + jnp.exp(-gate))) * up * wj)
        act = jnp.concatenate(parts, axis=-1).astype(BF16)
        ys_ref[pl.ds(row0, C), :] = jnp.dot(act, wd_ref[g], preferred_element_type=F32).astype(BF16)
        return carry

    lax.fori_loop(0, n_used, chunk_body, 0)

    def zero_body(c, carry):
        ys_ref[pl.ds(pl.multiple_of(c * C, 16), C), :] = jnp.zeros((C, ys_ref.shape[1]), BF16)
        return carry

    lax.fori_loop(n_used, _MOE_NCHUNK, zero_body, 0)

    n_main = N_GROUPS * (-(-(T // N_GROUPS) // C))
    n_tail = _MOE_NCHUNK - n_main
    back = (lax.broadcasted_iota(jnp.int32, (T, n_main * C), 1) == pos_i).astype(BF16)
    y_ref[...] = h_ref[...] + jnp.dot(back, ys_ref[:n_main * C, :], preferred_element_type=F32)

    @pl.when(n_used > n_main)
    def _():
        tail = (lax.broadcasted_iota(jnp.int32, (T, n_tail * C), 1) == pos_i - n_main * C).astype(BF16)
        y_ref[...] += jnp.dot(tail, ys_ref[n_main * C:, :], preferred_element_type=F32)

    y_ref[...] = _rms(y_ref[...], gfin_ref[...])


def _moe(hn, route, h, wg, wu, wd, gfin):
    N, D = hn.shape
    T = _MOE_TILE
    row = lambda w: pl.BlockSpec((T, w), lambda i: (i, 0))
    whole = lambda a: pl.BlockSpec(a.shape, lambda i: (0,) * a.ndim)
    return pl.pallas_call(
        _moe_kernel,
        grid=(N // T,),
        in_specs=[row(D), row(LANES), row(D),
                  whole(wg), whole(wu), whole(wd), whole(gfin)],
        out_specs=row(D),
        out_shape=jax.ShapeDtypeStruct((N, D), F32),
        scratch_shapes=[pltpu.VMEM((_MOE_NCHUNK * _MOE_CHUNK, D), BF16), pltpu.VMEM((T, D + LANES), BF16)],
        compiler_params=pltpu.CompilerParams(
            dimension_semantics=("arbitrary",), vmem_limit_bytes=_MOE_VMEM_LIMIT),
        name="moe_grouped",
    )(hn, route, h, wg, wu, wd, gfin)


def kernel(x, norm_mix_g, w_in, rpb, g_out_dil, g_out_na, w_out, norm_ffn_g, w_group, b_group,
           w_router, b_router, w_gate, w_up, w_down, norm_final_g):
    B, S, D = x.shape
    N = B * S
    depth = w_in.shape[0]
    assert depth == 1 and D == D_MODEL and S % (16 * _DQ) == 0

    t12, t3 = _dilated_tables()

    layer = 0
    col_scale = np.ones((6, WIDTH), np.float32)
    col_scale[0] = col_scale[3] = LOG2E * HEAD_DIM ** -0.5
    qkv = _inproj(x, norm_mix_g[layer].reshape(1, D), w_in[layer], jnp.asarray(col_scale.reshape(1, -1)),
                  tm=512)

    y_dil, y_na = _attention(qkv, jnp.asarray(t12), jnp.asarray(t3), _na_column_bias(rpb[layer]))

    n_route = N_EXPERTS + N_GROUPS
    w_r = jnp.concatenate([w_router[layer], w_group[layer], jnp.zeros((D, LANES - n_route), F32)], axis=1)
    b_r = jnp.concatenate([b_router[layer], b_group[layer], jnp.zeros((LANES - n_route,), F32)]).reshape(1, LANES)

    h, hn, route = _outproj(
        x.reshape(N, D), y_dil.reshape(N, WIDTH), y_na.reshape(N, WIDTH),
        g_out_dil[layer].reshape(1, WIDTH), g_out_na[layer].reshape(1, WIDTH),
        w_out[layer].astype(BF16), norm_ffn_g[layer].reshape(1, D), w_r.astype(BF16), b_r, tm=512)

    wd = w_down[layer].astype(BF16).reshape(N_GROUPS, _GROUP_WIDTH, D)
    y = _moe(hn, route, h, w_gate[layer].astype(BF16), w_up[layer].astype(BF16), wd,
             norm_final_g.reshape(1, D))
    return y.reshape(B, S, D)
```

```python
import numpy as np
import jax
import jax.numpy as jnp
from jax import lax
from jax.experimental import pallas as pl
from jax.experimental.pallas import tpu as pltpu

D_MODEL = 1024
HEAD_DIM = 64
N_HEADS = 8
N_PAIRS = N_HEADS // 2
WIDTH = N_HEADS * HEAD_DIM
N_SLABS = 6 * N_PAIRS
DIL_PATTERNS = ((128, 1), (512, 4), (2048, 16))
DIL_RADIUS = 64
GRID_W = 64
NA_WIN_ROWS = 8
NA_WIN_COLS = 16
N_GROUPS = 4
EXPERTS_PER_GROUP = 4
N_EXPERTS = 16
D_EXPERT = 256
RMS_EPS = 1e-6
NEG_INF = -1e30
LOG2E = 1.4426950408889634

LANES = 128
BF16_SUBLANES = 16
VMEM_LIMIT = 48 * 1024 * 1024

F32 = jnp.float32
BF16 = jnp.bfloat16


def _rms(x, gain):
    return x * lax.rsqrt(jnp.mean(x * x, axis=-1, keepdims=True) + RMS_EPS) * gain


def _inproj_kernel(x_ref, g_ref, w_ref, s_ref, o_ref, wb_ref):
    @pl.when((pl.program_id(0) == 0) & (pl.program_id(1) == 0))
    def _():
        for j in range(N_SLABS):
            cols = slice(j * LANES, (j + 1) * LANES)
            wb_ref[:, cols] = (w_ref[:, cols] * s_ref[:, cols]).astype(BF16)

    xn = _rms(x_ref[0], g_ref[...]).astype(BF16)
    chunk = 4 * LANES
    for c in range(N_SLABS * LANES // chunk):
        acc = jnp.dot(xn, wb_ref[:, c * chunk:(c + 1) * chunk], preferred_element_type=F32)
        for j in range(chunk // LANES):
            o_ref[0, c * (chunk // LANES) + j] = acc[:, j * LANES:(j + 1) * LANES].astype(BF16)


def _inproj(x, gain, w, col_scale, tm):
    B, S, D = x.shape
    return pl.pallas_call(
        _inproj_kernel,
        grid=(B, S // tm),
        in_specs=[
            pl.BlockSpec((1, tm, D), lambda b, i: (b, i, 0)),
            pl.BlockSpec((1, D), lambda b, i: (0, 0)),
            pl.BlockSpec((D, N_SLABS * LANES), lambda b, i: (0, 0)),
            pl.BlockSpec((1, N_SLABS * LANES), lambda b, i: (0, 0)),
        ],
        out_specs=pl.BlockSpec((1, N_SLABS, tm, LANES), lambda b, i: (b, 0, i, 0)),
        out_shape=jax.ShapeDtypeStruct((B, N_SLABS, S, LANES), BF16),
        scratch_shapes=[pltpu.VMEM((D, N_SLABS * LANES), BF16)],
        compiler_params=pltpu.CompilerParams(
            dimension_semantics=("arbitrary", "arbitrary"), vmem_limit_bytes=VMEM_LIMIT),
        name="inproj",
    )(x, gain, w, col_scale)


def _attend(qb, kw, vw, bias, sums_on_mxu):
    lane = lax.broadcasted_iota(jnp.int32, qb.shape, 1)
    zero = jnp.zeros_like(qb)
    qq = jnp.concatenate([jnp.where(lane < HEAD_DIM, qb, zero),
                          jnp.where(lane >= HEAD_DIM, qb, zero)], axis=0)
    s = lax.dot_general(qq, kw, (((1,), (1,)), ((), ())), preferred_element_type=F32) + bias
    m = jnp.max(s, axis=-1, keepdims=True)
    p = jnp.exp2(s - m)
    if not sums_on_mxu:
        l = jnp.sum(p, axis=-1, keepdims=True)
        return m, l, jnp.dot(p.astype(BF16), vw, preferred_element_type=F32)
    pv = jnp.dot(p.astype(BF16), jnp.concatenate([vw, jnp.ones_like(vw)], axis=1),
                 preferred_element_type=F32)
    return m, pv[:, LANES:], pv[:, :LANES]


def _merge_heads(top, bottom, q):
    lane = lax.broadcasted_iota(jnp.int32, (q, LANES), 1)
    return jnp.where(lane < HEAD_DIM, jnp.broadcast_to(top, (q, LANES)),
                     jnp.broadcast_to(bottom, (q, LANES)))


_DQ = 128
_DW = 256
_UNROLL = 16


def _dilated_tables():
    slopes = 2.0 ** (-(np.arange(N_HEADS) + 1.0))
    q = np.arange(_DQ)[:, None]
    t12 = np.zeros((N_PAIRS, 2, 3, 2 * _DQ, _DW), np.float32)
    k = np.arange(_DW)[None, :]
    for pat, dil in enumerate((1, 4)):
        for case, off in enumerate((0, _DW // 4, _DW // 2)):
            delta = np.abs(k - (q + off))
            for h in range(N_HEADS):
                tab = np.where(delta <= DIL_RADIUS, -LOG2E * slopes[h] * dil * delta, NEG_INF)
                t12[h // 2, pat, case, (h % 2) * _DQ:(h % 2 + 1) * _DQ] = tab
    t3 = np.zeros((N_PAIRS, 2 * _DQ, _DQ), np.float32)
    delta = np.abs(np.arange(_DQ)[None, :] - q)
    for h in range(N_HEADS):
        t3[h // 2, (h % 2) * _DQ:(h % 2 + 1) * _DQ] = np.where(
            delta <= DIL_RADIUS, -LOG2E * slopes[h] * 16 * delta, NEG_INF)
    return t12, t3


def _dilated_kernel(q_ref, k_ref, v_ref, t12_ref, t3_ref, o_ref,
                    tmp, tmp4, q4, k4, v4, q16, k16, v16,
                    m1, l1, a1, m2, l2, a2, m3, l3, a3, onat):
    S = tmp.shape[0]
    L4, L16 = S // 4, S // 16

    for src, d4, d16 in ((q_ref, q4, q16), (k_ref, k4, k16), (v_ref, v4, v16)):
        tmp[...] = src[0, 0].astype(F32)
        for r in range(4):
            sub = tmp[pl.ds(r, L4, stride=4), :]
            tmp4[r * L4:(r + 1) * L4, :] = sub
            d4[r * L4:(r + 1) * L4, :] = sub.astype(BF16)
        for r16 in range(16):
            r4, c4 = r16 % 4, r16 // 4
            d16[r16 * L16:(r16 + 1) * L16, :] = tmp4[pl.ds(r4 * L4 + c4, L16, stride=4), :].astype(BF16)

    def block(qb, kw, vw, bias, m_ref, l_ref, a_ref, row):
        m, l, pv = _attend(qb, kw, vw, bias, sums_on_mxu=False)
        m_ref[pl.ds(row, _DQ), :] = _merge_heads(m[:_DQ], m[_DQ:], _DQ)
        l_ref[pl.ds(row, _DQ), :] = _merge_heads(l[:_DQ], l[_DQ:], _DQ)
        a_ref[pl.ds(row, _DQ), :] = _merge_heads(pv[:_DQ], pv[_DQ:], _DQ)

    def case_of(blk, n_blk):
        return jnp.where(blk == 0, 0, jnp.where(blk == n_blk - 1, 2, 1))

    n1 = S // _DQ

    def p1_body(blk, carry):
        t0 = pl.multiple_of(blk * _DQ, _DQ)
        ws = pl.multiple_of(jnp.clip(t0 - DIL_RADIUS, 0, S - _DW), DIL_RADIUS)
        block(q_ref[0, 0, pl.ds(t0, _DQ), :], k_ref[0, 0, pl.ds(ws, _DW), :],
              v_ref[0, 0, pl.ds(ws, _DW), :], t12_ref[0, 0, case_of(blk, n1)], m1, l1, a1, t0)
        return carry

    lax.fori_loop(0, n1, p1_body, 0, unroll=_UNROLL)

    n2 = L4 // _DQ

    def p2_body(j, carry):
        r = j // n2
        blk = j % n2
        l0 = blk * _DQ
        ws = jnp.clip(l0 - DIL_RADIUS, 0, L4 - _DW)
        row = pl.multiple_of(r * L4 + l0, _DQ)
        krow = pl.multiple_of(r * L4 + ws, DIL_RADIUS)
        block(q4[pl.ds(row, _DQ), :], k4[pl.ds(krow, _DW), :], v4[pl.ds(krow, _DW), :],
              t12_ref[0, 1, case_of(blk, n2)], m2, l2, a2, row)
        return carry

    lax.fori_loop(0, 4 * n2, p2_body, 0, unroll=_UNROLL)

    def p3_body(r, carry):
        row = pl.multiple_of(r * L16, L16)
        block(q16[pl.ds(row, L16), :], k16[pl.ds(row, L16), :], v16[pl.ds(row, L16), :],
              t3_ref[0], m3, l3, a3, row)
        return carry

    lax.fori_loop(0, 16, p3_body, 0, unroll=_UNROLL)

    for r16 in range(16):
        r4, c4 = r16 % 4, r16 // 4
        via4 = pl.ds(r4 * L4 + c4, L16, stride=4)
        via16 = pl.ds(r16 * L16, L16)
        mb, mc = m2[via4, :], m3[via16, :]
        mx = jnp.maximum(mb, mc)
        wb, wc = jnp.exp2(mb - mx), jnp.exp2(mc - mx)
        l2[via4, :] = wb * l2[via4, :] + wc * l3[via16, :]
        a2[via4, :] = wb * a2[via4, :] + wc * a3[via16, :]
        m2[via4, :] = mx
    for r4 in range(4):
        for part in range(L4 // _DQ):
            nat = pl.ds(r4 + 4 * _DQ * part, _DQ, stride=4)
            via4 = pl.ds(r4 * L4 + _DQ * part, _DQ)
            ma, mb = m1[nat, :], m2[via4, :]
            mx = jnp.maximum(ma, mb)
            wa, wb = jnp.exp2(ma - mx), jnp.exp2(mb - mx)
            den = wa * l1[nat, :] + wb * l2[via4, :]
            num = wa * a1[nat, :] + wb * a2[via4, :]
            onat[nat, :] = num / den
    o_ref[0] = onat[...].astype(BF16)


_NQ_ROWS = 4
_NK_ROWS = 12


def _na_row_select(rows):
    n_blk = rows // _NQ_ROWS
    sel = np.full((n_blk, _NQ_ROWS, _NK_ROWS), -1, np.int64)
    for i in range(n_blk):
        kr0 = min(max(_NQ_ROWS * i - NA_WIN_ROWS // 2, 0), rows - _NK_ROWS)
        for a in range(_NQ_ROWS):
            qr = _NQ_ROWS * i + a
            rs = min(max(qr - NA_WIN_ROWS // 2, 0), rows - NA_WIN_ROWS)
            for b in range(_NK_ROWS):
                kr = kr0 + b
                if rs <= kr < rs + NA_WIN_ROWS:
                    sel[i, a, b] = kr - qr + NA_WIN_ROWS - 1
    for i in range(2, n_blk - 1):
        assert np.array_equal(sel[1], sel[i])
    return sel[[0, 1, n_blk - 1]]


_NA_QCOLS = 16
_NA_KCOLS = 32
_NA_BORDER = NA_WIN_COLS // 2
_NA_COLUMN_SETS = tuple(
    (((_NA_BORDER + _NA_QCOLS * j, _NA_QCOLS),), ((_NA_QCOLS * j, _NA_KCOLS),), 0)
    for j in range((GRID_W - 2 * _NA_BORDER) // _NA_QCOLS)
) + ((((0, _NA_BORDER), (GRID_W - _NA_BORDER, _NA_BORDER)),
      ((0, NA_WIN_COLS), (GRID_W - NA_WIN_COLS, NA_WIN_COLS)), 1),)


def _na_column_bias(rpb):
    n_dr, n_dc = 2 * NA_WIN_ROWS - 1, 2 * NA_WIN_COLS - 1
    kinds = []
    for kind in (0, 1):
        sets = [cs for cs in _NA_COLUMN_SETS if cs[2] == kind]
        layouts = []
        for q_runs, k_runs, _ in sets:
            qc = np.concatenate([np.arange(c0, c0 + n) for c0, n in q_runs])[:, None]
            kc = np.concatenate([np.arange(c0, c0 + n) for c0, n in k_runs])[None, :]
            start = np.clip(qc - NA_WIN_COLS // 2, 0, GRID_W - NA_WIN_COLS)
            col_ok = (kc >= start) & (kc < start + NA_WIN_COLS)
            assert (col_ok.sum(axis=1) == NA_WIN_COLS).all()
            layouts.append((col_ok, np.clip(kc - qc + NA_WIN_COLS - 1, 0, n_dc - 1)))
        col_ok, dc = layouts[0]
        assert all(np.array_equal(col_ok, o) and np.array_equal(dc, d) for o, d in layouts)
        onehot = (dc.reshape(1, -1) == np.arange(n_dc)[:, None]).astype(np.float32)
        t = jnp.dot(rpb.astype(F32).reshape(N_HEADS * n_dr, n_dc), onehot, precision=lax.Precision.HIGHEST)
        t = jnp.where(col_ok[None, None], LOG2E * t.reshape(N_HEADS, n_dr, _NA_QCOLS, _NA_KCOLS), NEG_INF)
        kinds.append(jnp.tile(t, (1, 1, 1, LANES // _NA_KCOLS)).reshape(
            N_PAIRS, 2, n_dr, _NA_QCOLS, LANES))
    return jnp.stack(kinds, axis=1)


def _na_expand_table(cb_ref, tab_ref, rows):
    n_blk = rows // _NQ_ROWS
    nq, nk = _NQ_ROWS * _NA_QCOLS, _NK_ROWS * _NA_KCOLS
    sel = _na_row_select(rows)

    @pl.when(pl.program_id(1) == 0)
    def _():
        a_idx = lax.broadcasted_iota(jnp.int32, (nq, nk), 0) // _NA_QCOLS
        b_idx = lax.broadcasted_iota(jnp.int32, (nq, nk), 1) // _NA_KCOLS
        for c, i in enumerate((0, 1, n_blk - 1)):
            kr0 = min(max(_NQ_ROWS * i - NA_WIN_ROWS // 2, 0), rows - _NK_ROWS)
            qr = _NQ_ROWS * i + a_idx
            kr = kr0 + b_idx
            first = jnp.clip(qr - NA_WIN_ROWS // 2, 0, rows - NA_WIN_ROWS)
            dr = jnp.where((kr >= first) & (kr < first + NA_WIN_ROWS), kr - qr + NA_WIN_ROWS - 1, -1)
            for kind in range(cb_ref.shape[1]):
                for h in range(2):
                    out = jnp.full((nq, nk), NEG_INF, F32)
                    for r in sorted(set(int(s) for s in sel[c].reshape(-1)) - {-1}):
                        tile = jnp.tile(cb_ref[0, kind, h, r], (_NQ_ROWS, nk // LANES))
                        out = jnp.where(dr == r, tile, out)
                    tab_ref[kind, c, h * nq:(h + 1) * nq, :] = out


def _na_blocks(q_ref, k_ref, v_ref, tab_ref, o_ref, qf, of):
    S = q_ref.shape[2]
    rows = S // GRID_W
    n_blk = rows // _NQ_ROWS
    nq = _NQ_ROWS * _NA_QCOLS
    lane = lax.broadcasted_iota(jnp.int32, (nq, LANES), 1)
    qf[...] = q_ref[0, 0].astype(F32)
    for i in range(n_blk):
        kr0 = min(max(_NQ_ROWS * i - NA_WIN_ROWS // 2, 0), rows - _NK_ROWS)
        case = 0 if i == 0 else (2 if i == n_blk - 1 else 1)
        for q_runs, k_runs, kind in _NA_COLUMN_SETS:
            q_rows = [((_NQ_ROWS * i + a) * GRID_W + c0, n) for a in range(_NQ_ROWS) for c0, n in q_runs]
            k_rows = [((kr0 + b) * GRID_W + c0, n) for b in range(_NK_ROWS) for c0, n in k_runs]
            qb = jnp.concatenate([qf[r0:r0 + n, :] for r0, n in q_rows], axis=0).astype(BF16)
            kw = jnp.concatenate([k_ref[0, 0, r0:r0 + n, :] for r0, n in k_rows], axis=0)
            vw = jnp.concatenate([v_ref[0, 0, r0:r0 + n, :] for r0, n in k_rows], axis=0)
            m, l, pv = _attend(qb, kw, vw, tab_ref[kind, case], sums_on_mxu=True)
            o = pv / l
            o = jnp.where(lane < HEAD_DIM, o[:nq], o[nq:])
            at = 0
            for r0, n in q_rows:
                of[r0:r0 + n, :] = o[at:at + n]
                at += n
    o_ref[0] = of[...].astype(BF16)


def _attention_kernel(qd_ref, kd_ref, vd_ref, t12_ref, t3_ref, qn_ref, kn_ref, vn_ref, cb_ref,
                      od_ref, on_ref, tab_ref, qf, of, *dilated_scratch):
    _na_expand_table(cb_ref, tab_ref, qn_ref.shape[2] // GRID_W)
    _dilated_kernel(qd_ref, kd_ref, vd_ref, t12_ref, t3_ref, od_ref, *dilated_scratch)
    _na_blocks(qn_ref, kn_ref, vn_ref, tab_ref, on_ref, qf, of)


def _attention(qkv, t12, t3, col_bias):
    B, _, S, _ = qkv.shape
    f32_buf = pltpu.VMEM((S, LANES), F32)
    bf16_buf = pltpu.VMEM((S, LANES), BF16)
    slab = lambda off: pl.BlockSpec((1, 1, S, LANES), lambda p, b: (b, off + p, 0, 0))
    per_pair = lambda a: pl.BlockSpec((1,) + a.shape[1:], lambda p, b: (p,) + (0,) * (a.ndim - 1))
    out = pl.BlockSpec((1, S, LANES), lambda p, b: (b, 0, p))
    return pl.pallas_call(
        _attention_kernel,
        grid=(N_PAIRS, B),
        in_specs=[slab(0), slab(N_PAIRS), slab(2 * N_PAIRS), per_pair(t12), per_pair(t3),
                  slab(3 * N_PAIRS), slab(4 * N_PAIRS), slab(5 * N_PAIRS), per_pair(col_bias)],
        out_specs=[out, out],
        out_shape=[jax.ShapeDtypeStruct((B, S, WIDTH), BF16)] * 2,
        scratch_shapes=([pltpu.VMEM((col_bias.shape[1], 3, 2 * _NQ_ROWS * _NA_QCOLS,
                                     _NK_ROWS * _NA_KCOLS), F32)]
                        + [f32_buf] * 2 + [f32_buf] * 2 + [bf16_buf] * 6 + [f32_buf] * 10),
        compiler_params=pltpu.CompilerParams(
            dimension_semantics=("arbitrary", "arbitrary"), vmem_limit_bytes=VMEM_LIMIT),
        name="attention",
    )(qkv, qkv, qkv, t12, t3, qkv, qkv, qkv, col_bias)


_GROUP_LANE0 = N_EXPERTS
_ROUTE_GROUP_LANE = EXPERTS_PER_GROUP


def _route(logits):
    tm = logits.shape[0]
    n_rows = _GROUP_LANE0 + 2 * N_GROUPS
    lt = jnp.transpose(logits)[:n_rows, :]
    row_i = lax.broadcasted_iota(jnp.int32, (n_rows, tm), 0)
    row = row_i.astype(F32)
    big = float(LANES)
    is_group = (row_i >= _GROUP_LANE0) & (row_i < _GROUP_LANE0 + N_GROUPS)
    gl = jnp.where(is_group, lt, NEG_INF)
    gmax = jnp.max(gl, axis=0, keepdims=True)
    g_idx = jnp.min(jnp.where(is_group & (gl == gmax), row, big), axis=0, keepdims=True) - _GROUP_LANE0
    g_weight = 1.0 / jnp.sum(jnp.where(is_group, jnp.exp(gl - gmax), 0.0), axis=0, keepdims=True)
    in_group = (row_i < N_EXPERTS) & ((row_i // EXPERTS_PER_GROUP).astype(F32) == g_idx)
    el = jnp.where(in_group, lt, NEG_INF)
    v1 = jnp.max(el, axis=0, keepdims=True)
    i1 = jnp.min(jnp.where(in_group & (el == v1), row, big), axis=0, keepdims=True)
    rest = in_group & (row != i1)
    el2 = jnp.where(rest, lt, NEG_INF)
    v2 = jnp.max(el2, axis=0, keepdims=True)
    i2 = jnp.min(jnp.where(rest & (el2 == v2), row, big), axis=0, keepdims=True)
    e2 = jnp.exp(v2 - v1)
    w1 = g_weight / (1.0 + e2)
    w2 = g_weight * e2 / (1.0 + e2)
    base = g_idx * EXPERTS_PER_GROUP
    rec_i = lax.broadcasted_iota(jnp.int32, (LANES, tm), 0)
    rec = rec_i.astype(F32)
    record = jnp.where(rec == i1 - base, w1,
                       jnp.where(rec == i2 - base, w2,
                                 jnp.where(rec_i == _ROUTE_GROUP_LANE, g_idx, 0.0)))
    return jnp.transpose(record)


def _outproj_kernel(x_ref, yd_ref, yn_ref, gd_ref, gn_ref, wo_ref, gf_ref, wr_ref,
                    br_ref, h_ref, hn_ref, comb_ref):
    yd = _rms(yd_ref[...].astype(F32), gd_ref[...]).astype(BF16)
    yn = _rms(yn_ref[...].astype(F32), gn_ref[...]).astype(BF16)
    h = x_ref[...] + jnp.dot(jnp.concatenate([yd, yn], axis=-1), wo_ref[...], preferred_element_type=F32)
    h_ref[...] = h
    hn = _rms(h, gf_ref[...]).astype(BF16)
    hn_ref[...] = hn
    logits = jnp.dot(hn, wr_ref[...], preferred_element_type=F32) + br_ref[...]
    comb_ref[...] = _route(logits)


def _outproj(x2, yd2, yn2, gd, gn, wo, gf, wr, br, tm):
    N, D = x2.shape
    row = lambda w: pl.BlockSpec((tm, w), lambda i: (i, 0))
    full = lambda a, b: pl.BlockSpec((a, b), lambda i: (0, 0))
    return pl.pallas_call(
        _outproj_kernel,
        grid=(N // tm,),
        in_specs=[row(D), row(WIDTH), row(WIDTH), full(1, WIDTH), full(1, WIDTH), full(2 * WIDTH, D),
                  full(1, D), full(D, LANES), full(1, LANES)],
        out_specs=[row(D), row(D), row(LANES)],
        out_shape=[jax.ShapeDtypeStruct((N, D), F32), jax.ShapeDtypeStruct((N, D), BF16),
                   jax.ShapeDtypeStruct((N, LANES), F32)],
        compiler_params=pltpu.CompilerParams(
            dimension_semantics=("arbitrary",), vmem_limit_bytes=VMEM_LIMIT),
        name="outproj_route",
    )(x2, yd2, yn2, gd, gn, wo, gf, wr, br)


_MOE_TILE = 512
_MOE_CHUNK = 144
_MOE_NCHUNK = _MOE_TILE // _MOE_CHUNK + N_GROUPS
_GROUP_WIDTH = EXPERTS_PER_GROUP * D_EXPERT
_ROUTE_PIECE = 8
_MOE_VMEM_LIMIT = 58 * 1024 * 1024


def _moe_kernel(hn_ref, route_ref, h_ref, wg_ref, wu_ref, wd_ref, gfin_ref, y_ref, ys_ref, xcat_ref):
    T, C = _MOE_TILE, _MOE_CHUNK
    route = route_ref[...]
    lane = lax.broadcasted_iota(jnp.int32, (T, LANES), 1)
    gid = jnp.sum(jnp.where(lane == _ROUTE_GROUP_LANE, route, 0.0), axis=-1, keepdims=True)
    onehot = jnp.where((lane < N_GROUPS) & (lane.astype(F32) == gid), 1.0, 0.0)

    before = (lax.broadcasted_iota(jnp.int32, (LANES, LANES), 1)
              < lax.broadcasted_iota(jnp.int32, (LANES, LANES), 0)).astype(BF16)
    count = jnp.zeros((1, LANES), F32)
    ranks = []
    for blk in range(T // LANES):
        oh = onehot[blk * LANES:(blk + 1) * LANES]
        ranks.append(jnp.dot(before, oh.astype(BF16), preferred_element_type=F32) + count)
        count = count + jnp.sum(oh, axis=0, keepdims=True)
    rank = jnp.concatenate(ranks, axis=0)
    nchunk = jnp.floor((count + (C - 1)) * (1.0 / C)).astype(jnp.int32)
    off1 = nchunk[0, 0]
    off2 = off1 + nchunk[0, 1]
    off3 = off2 + nchunk[0, 2]
    n_used = off3 + nchunk[0, 3]
    start = jnp.where(lane == 1, off1, jnp.where(lane == 2, off2, jnp.where(lane == 3, off3, 0)))
    pos = jnp.sum(onehot * (rank + (start * C).astype(F32)), axis=-1, keepdims=True)
    pos_i = pos.astype(jnp.int32)
    pos_row = jnp.transpose(jnp.broadcast_to(pos, (T, LANES)))[0:1, :].astype(jnp.int32)

    r_hi = route.astype(BF16).astype(F32)
    r_mid = (route - r_hi).astype(BF16).astype(F32)
    r_lo = (route - r_hi - r_mid).astype(BF16).astype(F32)
    packed = r_hi + pltpu.roll(r_mid, _ROUTE_PIECE, axis=1) + pltpu.roll(r_lo, 2 * _ROUTE_PIECE, axis=1)
    xcat_ref[:, :hn_ref.shape[1]] = hn_ref[...]
    xcat_ref[:, hn_ref.shape[1]:] = packed.astype(BF16)

    def chunk_body(c, carry):
        g = ((c >= off1).astype(jnp.int32) + (c >= off2).astype(jnp.int32)
             + (c >= off3).astype(jnp.int32))
        row0 = pl.multiple_of(c * C, BF16_SUBLANES)
        sel = (pos_row == row0 + lax.broadcasted_iota(jnp.int32, (C, T), 0)).astype(BF16)
        xr = jnp.dot(sel, xcat_ref[...], preferred_element_type=F32)
        xs = xr[:, :hn_ref.shape[1]].astype(BF16)
        r3 = xr[:, hn_ref.shape[1]:]
        r = (r3 + pltpu.roll(r3, LANES - _ROUTE_PIECE, axis=1)
             + pltpu.roll(r3, LANES - 2 * _ROUTE_PIECE, axis=1))
        clane = lax.broadcasted_iota(jnp.int32, (C, LANES), 1)
        parts = []
        for j in range(EXPERTS_PER_GROUP):
            e = g * EXPERTS_PER_GROUP + j
            gate = jnp.dot(xs, wg_ref[e], preferred_element_type=F32)
            up = jnp.dot(xs, wu_ref[e], preferred_element_type=F32)
            wj = jnp.sum(jnp.where(clane == j, r, 0.0), axis=-1, keepdims=True)
            parts.append((gate / (1.0 + jnp.exp(-gate))) * up * wj)
        act = jnp.concatenate(parts, axis=-1).astype(BF16)
        ys_ref[pl.ds(row0, C), :] = jnp.dot(act, wd_ref[g], preferred_element_type=F32).astype(BF16)
        return carry

    lax.fori_loop(0, n_used, chunk_body, 0)

    def zero_body(c, carry):
        ys_ref[pl.ds(pl.multiple_of(c * C, BF16_SUBLANES), C), :] = jnp.zeros((C, ys_ref.shape[1]), BF16)
        return carry

    lax.fori_loop(n_used, _MOE_NCHUNK, zero_body, 0)

    n_main = N_GROUPS * (-(-(T // N_GROUPS) // C))
    n_tail = _MOE_NCHUNK - n_main
    back = (lax.broadcasted_iota(jnp.int32, (T, n_main * C), 1) == pos_i).astype(BF16)
    y_ref[...] = h_ref[...] + jnp.dot(back, ys_ref[:n_main * C, :], preferred_element_type=F32)

    @pl.when(n_used > n_main)
    def _():
        tail = (lax.broadcasted_iota(jnp.int32, (T, n_tail * C), 1) == pos_i - n_main * C).astype(BF16)
        y_ref[...] += jnp.dot(tail, ys_ref[n_main * C:, :], preferred_element_type=F32)

    y_ref[...] = _rms(y_ref[...], gfin_ref[...])


def _moe(hn, route, h, wg, wu, wd, gfin):
    N, D = hn.shape
    T = _MOE_TILE
    row = lambda w: pl.BlockSpec((T, w), lambda i: (i, 0))
    whole = lambda a: pl.BlockSpec(a.shape, lambda i: (0,) * a.ndim)
    return pl.pallas_call(
        _moe_kernel,
        grid=(N // T,),
        in_specs=[row(D), row(LANES), row(D),
                  whole(wg), whole(wu), whole(wd), whole(gfin)],
        out_specs=row(D),
        out_shape=jax.ShapeDtypeStruct((N, D), F32),
        scratch_shapes=[pltpu.VMEM((_MOE_NCHUNK * _MOE_CHUNK, D), BF16), pltpu.VMEM((T, D + LANES), BF16)],
        compiler_params=pltpu.CompilerParams(
            dimension_semantics=("arbitrary",), vmem_limit_bytes=_MOE_VMEM_LIMIT),
        name="moe_grouped",
    )(hn, route, h, wg, wu, wd, gfin)


def kernel(x, norm_mix_g, w_in, rpb, g_out_dil, g_out_na, w_out, norm_ffn_g, w_group, b_group,
           w_router, b_router, w_gate, w_up, w_down, norm_final_g):
    B, S, D = x.shape
    N = B * S
    depth = w_in.shape[0]
    assert depth == 1 and D == D_MODEL and S % (16 * _DQ) == 0

    t12, t3 = _dilated_tables()

    layer = 0
    col_scale = np.ones((6, WIDTH), np.float32)
    col_scale[0] = col_scale[3] = LOG2E * HEAD_DIM ** -0.5
    qkv = _inproj(x, norm_mix_g[layer].reshape(1, D), w_in[layer], jnp.asarray(col_scale.reshape(1, -1)),
                  tm=512)

    y_dil, y_na = _attention(qkv, jnp.asarray(t12), jnp.asarray(t3), _na_column_bias(rpb[layer]))

    n_route = N_EXPERTS + N_GROUPS
    w_r = jnp.concatenate([w_router[layer], w_group[layer], jnp.zeros((D, LANES - n_route), F32)], axis=1)
    b_r = jnp.concatenate([b_router[layer], b_group[layer], jnp.zeros((LANES - n_route,), F32)]).reshape(1, LANES)

    h, hn, route = _outproj(
        x.reshape(N, D), y_dil.reshape(N, WIDTH), y_na.reshape(N, WIDTH),
        g_out_dil[layer].reshape(1, WIDTH), g_out_na[layer].reshape(1, WIDTH),
        w_out[layer].astype(BF16), norm_ffn_g[layer].reshape(1, D), w_r.astype(BF16), b_r, tm=512)

    wd = w_down[layer].astype(BF16).reshape(N_GROUPS, _GROUP_WIDTH, D)
    y = _moe(hn, route, h, w_gate[layer].astype(BF16), w_up[layer].astype(BF16), wd,
             norm_final_g.reshape(1, D))
    return y.reshape(B, S, D)
```

```python
import numpy as np
import jax
import jax.numpy as jnp
from jax import lax
from jax.experimental import pallas as pl
from jax.experimental.pallas import tpu as pltpu

D_MODEL = 1024
HEAD_DIM = 64
N_HEADS = 8
N_PAIRS = N_HEADS // 2
WIDTH = N_HEADS * HEAD_DIM
N_SLABS = 6 * N_PAIRS
DIL_PATTERNS = ((128, 1), (512, 4), (2048, 16))
DIL_RADIUS = 64
GRID_W = 64
NA_WIN_ROWS = 8
NA_WIN_COLS = 16
N_GROUPS = 4
EXPERTS_PER_GROUP = 4
N_EXPERTS = 16
D_EXPERT = 256
RMS_EPS = 1e-6
NEG_INF = -1e30
LOG2E = 1.4426950408889634

LANES = 128
BF16_SUBLANES = 16
VMEM_LIMIT = 48 * 1024 * 1024

F32 = jnp.float32
BF16 = jnp.bfloat16


def _rms(x, gain):
    return x * lax.rsqrt(jnp.mean(x * x, axis=-1, keepdims=True) + RMS_EPS) * gain


def _inproj_kernel(x_ref, g_ref, w_ref, s_ref, o_ref, wb_ref):
    @pl.when((pl.program_id(0) == 0) & (pl.program_id(1) == 0))
    def _():
        for j in range(N_SLABS):
            cols = slice(j * LANES, (j + 1) * LANES)
            wb_ref[:, cols] = (w_ref[:, cols] * s_ref[:, cols]).astype(BF16)

    xn = _rms(x_ref[0], g_ref[...]).astype(BF16)
    chunk = 4 * LANES
    for c in range(N_SLABS * LANES // chunk):
        acc = jnp.dot(xn, wb_ref[:, c * chunk:(c + 1) * chunk], preferred_element_type=F32)
        for j in range(chunk // LANES):
            o_ref[0, c * (chunk // LANES) + j] = acc[:, j * LANES:(j + 1) * LANES].astype(BF16)


def _inproj(x, gain, w, col_scale, tm):
    B, S, D = x.shape
    return pl.pallas_call(
        _inproj_kernel,
        grid=(B, S // tm),
        in_specs=[
            pl.BlockSpec((1, tm, D), lambda b, i: (b, i, 0)),
            pl.BlockSpec((1, D), lambda b, i: (0, 0)),
            pl.BlockSpec((D, N_SLABS * LANES), lambda b, i: (0, 0)),
            pl.BlockSpec((1, N_SLABS * LANES), lambda b, i: (0, 0)),
        ],
        out_specs=pl.BlockSpec((1, N_SLABS, tm, LANES), lambda b, i: (b, 0, i, 0)),
        out_shape=jax.ShapeDtypeStruct((B, N_SLABS, S, LANES), BF16),
        scratch_shapes=[pltpu.VMEM((D, N_SLABS * LANES), BF16)],
        compiler_params=pltpu.CompilerParams(
            dimension_semantics=("arbitrary", "arbitrary"), vmem_limit_bytes=VMEM_LIMIT),
        name="inproj",
    )(x, gain, w, col_scale)


def _attend(qb, kw, vw, bias, sums_on_mxu):
    lane = lax.broadcasted_iota(jnp.int32, qb.shape, 1)
    zero = jnp.zeros_like(qb)
    qq = jnp.concatenate([jnp.where(lane < HEAD_DIM, qb, zero),
                          jnp.where(lane >= HEAD_DIM, qb, zero)], axis=0)
    s = lax.dot_general(qq, kw, (((1,), (1,)), ((), ())), preferred_element_type=F32) + bias
    m = jnp.max(s, axis=-1, keepdims=True)
    p = jnp.exp2(s - m)
    if not sums_on_mxu:
        l = jnp.sum(p, axis=-1, keepdims=True)
        return m, l, jnp.dot(p.astype(BF16), vw, preferred_element_type=F32)
    pv = jnp.dot(p.astype(BF16), jnp.concatenate([vw, jnp.ones_like(vw)], axis=1),
                 preferred_element_type=F32)
    return m, pv[:, LANES:], pv[:, :LANES]


def _merge_heads(top, bottom, q):
    lane = lax.broadcasted_iota(jnp.int32, (q, LANES), 1)
    return jnp.where(lane < HEAD_DIM, jnp.broadcast_to(top, (q, LANES)),
                     jnp.broadcast_to(bottom, (q, LANES)))


_DQ = 128
_DW = 256
_UNROLL = 16


def _dilated_tables():
    slopes = 2.0 ** (-(np.arange(N_HEADS) + 1.0))
    q = np.arange(_DQ)[:, None]
    t12 = np.zeros((N_PAIRS, 2, 3, 2 * _DQ, _DW), np.float32)
    k = np.arange(_DW)[None, :]
    for pat, dil in enumerate((1, 4)):
        for case, off in enumerate((0, _DW // 4, _DW // 2)):
            delta = np.abs(k - (q + off))
            for h in range(N_HEADS):
                tab = np.where(delta <= DIL_RADIUS, -LOG2E * slopes[h] * dil * delta, NEG_INF)
                t12[h // 2, pat, case, (h % 2) * _DQ:(h % 2 + 1) * _DQ] = tab
    t3 = np.zeros((N_PAIRS, 2 * _DQ, _DQ), np.float32)
    delta = np.abs(np.arange(_DQ)[None, :] - q)
    for h in range(N_HEADS):
        t3[h // 2, (h % 2) * _DQ:(h % 2 + 1) * _DQ] = np.where(
            delta <= DIL_RADIUS, -LOG2E * slopes[h] * 16 * delta, NEG_INF)
    return t12, t3


def _dilated_kernel(q_ref, k_ref, v_ref, t12_ref, t3_ref, o_ref,
                    tmp, tmp4, q4, k4, v4, q16, k16, v16,
                    m1, l1, a1, m2, l2, a2, m3, l3, a3, onat):
    S = tmp.shape[0]
    L4, L16 = S // 4, S // 16

    for src, d4, d16 in ((q_ref, q4, q16), (k_ref, k4, k16), (v_ref, v4, v16)):
        tmp[...] = src[0, 0].astype(F32)
        for r in range(4):
            sub = tmp[pl.ds(r, L4, stride=4), :]
            tmp4[r * L4:(r + 1) * L4, :] = sub
            d4[r * L4:(r + 1) * L4, :] = sub.astype(BF16)
        for r16 in range(16):
            r4, c4 = r16 % 4, r16 // 4
            d16[r16 * L16:(r16 + 1) * L16, :] = tmp4[pl.ds(r4 * L4 + c4, L16, stride=4), :].astype(BF16)

    def block(qb, kw, vw, bias, m_ref, l_ref, a_ref, row):
        m, l, pv = _attend(qb, kw, vw, bias, sums_on_mxu=False)
        m_ref[pl.ds(row, _DQ), :] = _merge_heads(m[:_DQ], m[_DQ:], _DQ)
        l_ref[pl.ds(row, _DQ), :] = _merge_heads(l[:_DQ], l[_DQ:], _DQ)
        a_ref[pl.ds(row, _DQ), :] = _merge_heads(pv[:_DQ], pv[_DQ:], _DQ)

    def case_of(blk, n_blk):
        return jnp.where(blk == 0, 0, jnp.where(blk == n_blk - 1, 2, 1))

    n1 = S // _DQ

    def p1_body(blk, carry):
        t0 = pl.multiple_of(blk * _DQ, _DQ)
        ws = pl.multiple_of(jnp.clip(t0 - DIL_RADIUS, 0, S - _DW), DIL_RADIUS)
        block(q_ref[0, 0, pl.ds(t0, _DQ), :], k_ref[0, 0, pl.ds(ws, _DW), :],
              v_ref[0, 0, pl.ds(ws, _DW), :], t12_ref[0, 0, case_of(blk, n1)], m1, l1, a1, t0)
        return carry

    lax.fori_loop(0, n1, p1_body, 0, unroll=_UNROLL)

    n2 = L4 // _DQ

    def p2_body(j, carry):
        r = j // n2
        blk = j % n2
        l0 = blk * _DQ
        ws = jnp.clip(l0 - DIL_RADIUS, 0, L4 - _DW)
        row = pl.multiple_of(r * L4 + l0, _DQ)
        krow = pl.multiple_of(r * L4 + ws, DIL_RADIUS)
        block(q4[pl.ds(row, _DQ), :], k4[pl.ds(krow, _DW), :], v4[pl.ds(krow, _DW), :],
              t12_ref[0, 1, case_of(blk, n2)], m2, l2, a2, row)
        return carry

    lax.fori_loop(0, 4 * n2, p2_body, 0, unroll=_UNROLL)

    def p3_body(r, carry):
        row = pl.multiple_of(r * L16, L16)
        block(q16[pl.ds(row, L16), :], k16[pl.ds(row, L16), :], v16[pl.ds(row, L16), :],
              t3_ref[0], m3, l3, a3, row)
        return carry

    lax.fori_loop(0, 16, p3_body, 0, unroll=_UNROLL)

    for r16 in range(16):
        r4, c4 = r16 % 4, r16 // 4
        via4 = pl.ds(r4 * L4 + c4, L16, stride=4)
        via16 = pl.ds(r16 * L16, L16)
        mb, mc = m2[via4, :], m3[via16, :]
        mx = jnp.maximum(mb, mc)
        wb, wc = jnp.exp2(mb - mx), jnp.exp2(mc - mx)
        l2[via4, :] = wb * l2[via4, :] + wc * l3[via16, :]
        a2[via4, :] = wb * a2[via4, :] + wc * a3[via16, :]
        m2[via4, :] = mx
    for r4 in range(4):
        for part in range(L4 // _DQ):
            nat = pl.ds(r4 + 4 * _DQ * part, _DQ, stride=4)
            via4 = pl.ds(r4 * L4 + _DQ * part, _DQ)
            ma, mb = m1[nat, :], m2[via4, :]
            mx = jnp.maximum(ma, mb)
            wa, wb = jnp.exp2(ma - mx), jnp.exp2(mb - mx)
            den = wa * l1[nat, :] + wb * l2[via4, :]
            num = wa * a1[nat, :] + wb * a2[via4, :]
            onat[nat, :] = num / den
    o_ref[0] = onat[...].astype(BF16)


_NQ_ROWS = 4
_NK_ROWS = 12


def _na_row_select(rows):
    n_blk = rows // _NQ_ROWS
    sel = np.full((n_blk, _NQ_ROWS, _NK_ROWS), -1, np.int64)
    for i in range(n_blk):
        kr0 = min(max(_NQ_ROWS * i - NA_WIN_ROWS // 2, 0), rows - _NK_ROWS)
        for a in range(_NQ_ROWS):
            qr = _NQ_ROWS * i + a
            rs = min(max(qr - NA_WIN_ROWS // 2, 0), rows - NA_WIN_ROWS)
            for b in range(_NK_ROWS):
                kr = kr0 + b
                if rs <= kr < rs + NA_WIN_ROWS:
                    sel[i, a, b] = kr - qr + NA_WIN_ROWS - 1
    for i in range(2, n_blk - 1):
        assert np.array_equal(sel[1], sel[i])
    return sel[[0, 1, n_blk - 1]]


_NA_QCOLS = 16
_NA_KCOLS = 32
_NA_BORDER = NA_WIN_COLS // 2
_NA_COLUMN_SETS = tuple(
    (((_NA_BORDER + _NA_QCOLS * j, _NA_QCOLS),), ((_NA_QCOLS * j, _NA_KCOLS),), 0)
    for j in range((GRID_W - 2 * _NA_BORDER) // _NA_QCOLS)
) + ((((0, _NA_BORDER), (GRID_W - _NA_BORDER, _NA_BORDER)),
      ((0, NA_WIN_COLS), (GRID_W - NA_WIN_COLS, NA_WIN_COLS)), 1),)


def _na_column_bias(rpb):
    n_dr, n_dc = 2 * NA_WIN_ROWS - 1, 2 * NA_WIN_COLS - 1
    kinds = []
    for kind in (0, 1):
        sets = [cs for cs in _NA_COLUMN_SETS if cs[2] == kind]
        layouts = []
        for q_runs, k_runs, _ in sets:
            qc = np.concatenate([np.arange(c0, c0 + n) for c0, n in q_runs])[:, None]
            kc = np.concatenate([np.arange(c0, c0 + n) for c0, n in k_runs])[None, :]
            start = np.clip(qc - NA_WIN_COLS // 2, 0, GRID_W - NA_WIN_COLS)
            col_ok = (kc >= start) & (kc < start + NA_WIN_COLS)
            assert (col_ok.sum(axis=1) == NA_WIN_COLS).all()
            layouts.append((col_ok, np.clip(kc - qc + NA_WIN_COLS - 1, 0, n_dc - 1)))
        col_ok, dc = layouts[0]
        assert all(np.array_equal(col_ok, o) and np.array_equal(dc, d) for o, d in layouts)
        onehot = (dc.reshape(1, -1) == np.arange(n_dc)[:, None]).astype(np.float32)
        t = jnp.dot(rpb.astype(F32).reshape(N_HEADS * n_dr, n_dc), onehot, precision=lax.Precision.HIGHEST)
        t = jnp.where(col_ok[None, None], LOG2E * t.reshape(N_HEADS, n_dr, _NA_QCOLS, _NA_KCOLS), NEG_INF)
        kinds.append(jnp.tile(t, (1, 1, 1, LANES // _NA_KCOLS)).reshape(
            N_PAIRS, 2, n_dr, _NA_QCOLS, LANES))
    return jnp.stack(kinds, axis=1)


def _na_expand_table(cb_ref, tab_ref, rows):
    n_blk = rows // _NQ_ROWS
    nq, nk = _NQ_ROWS * _NA_QCOLS, _NK_ROWS * _NA_KCOLS
    sel = _na_row_select(rows)

    @pl.when(pl.program_id(1) == 0)
    def _():
        a_idx = lax.broadcasted_iota(jnp.int32, (nq, nk), 0) // _NA_QCOLS
        b_idx = lax.broadcasted_iota(jnp.int32, (nq, nk), 1) // _NA_KCOLS
        for c, i in enumerate((0, 1, n_blk - 1)):
            kr0 = min(max(_NQ_ROWS * i - NA_WIN_ROWS // 2, 0), rows - _NK_ROWS)
            qr = _NQ_ROWS * i + a_idx
            kr = kr0 + b_idx
            first = jnp.clip(qr - NA_WIN_ROWS // 2, 0, rows - NA_WIN_ROWS)
            dr = jnp.where((kr >= first) & (kr < first + NA_WIN_ROWS), kr - qr + NA_WIN_ROWS - 1, -1)
            for kind in range(cb_ref.shape[1]):
                for h in range(2):
                    out = jnp.full((nq, nk), NEG_INF, F32)
                    for r in sorted(set(int(s) for s in sel[c].reshape(-1)) - {-1}):
                        tile = jnp.tile(cb_ref[0, kind, h, r], (_NQ_ROWS, nk // LANES))
                        out = jnp.where(dr == r, tile, out)
                    tab_ref[kind, c, h * nq:(h + 1) * nq, :] = out


def _na_blocks(q_ref, k_ref, v_ref, tab_ref, o_ref, qf, of):
    S = q_ref.shape[2]
    rows = S // GRID_W
    n_blk = rows // _NQ_ROWS
    nq = _NQ_ROWS * _NA_QCOLS
    lane = lax.broadcasted_iota(jnp.int32, (nq, LANES), 1)
    qf[...] = q_ref[0, 0].astype(F32)
    for i in range(n_blk):
        kr0 = min(max(_NQ_ROWS * i - NA_WIN_ROWS // 2, 0), rows - _NK_ROWS)
        case = 0 if i == 0 else (2 if i == n_blk - 1 else 1)
        for q_runs, k_runs, kind in _NA_COLUMN_SETS:
            q_rows = [((_NQ_ROWS * i + a) * GRID_W + c0, n) for a in range(_NQ_ROWS) for c0, n in q_runs]
            k_rows = [((kr0 + b) * GRID_W + c0, n) for b in range(_NK_ROWS) for c0, n in k_runs]
            qb = jnp.concatenate([qf[r0:r0 + n, :] for r0, n in q_rows], axis=0).astype(BF16)
            kw = jnp.concatenate([k_ref[0, 0, r0:r0 + n, :] for r0, n in k_rows], axis=0)
            vw = jnp.concatenate([v_ref[0, 0, r0:r0 + n, :] for r0, n in k_rows], axis=0)
            m, l, pv = _attend(qb, kw, vw, tab_ref[kind, case], sums_on_mxu=True)
            o = pv / l
            o = jnp.where(lane < HEAD_DIM, o[:nq], o[nq:])
            at = 0
            for r0, n in q_rows:
                of[r0:r0 + n, :] = o[at:at + n]
                at += n
    o_ref[0] = of[...].astype(BF16)


def _attention_kernel(qd_ref, kd_ref, vd_ref, t12_ref, t3_ref, qn_ref, kn_ref, vn_ref, cb_ref,
                      od_ref, on_ref, tab_ref, qf, of, *dilated_scratch):
    _na_expand_table(cb_ref, tab_ref, qn_ref.shape[2] // GRID_W)
    _dilated_kernel(qd_ref, kd_ref, vd_ref, t12_ref, t3_ref, od_ref, *dilated_scratch)
    _na_blocks(qn_ref, kn_ref, vn_ref, tab_ref, on_ref, qf, of)


def _attention(qkv, t12, t3, col_bias):
    B, _, S, _ = qkv.shape
    f32_buf = pltpu.VMEM((S, LANES), F32)
    bf16_buf = pltpu.VMEM((S, LANES), BF16)
    slab = lambda off: pl.BlockSpec((1, 1, S, LANES), lambda p, b: (b, off + p, 0, 0))
    per_pair = lambda a: pl.BlockSpec((1,) + a.shape[1:], lambda p, b: (p,) + (0,) * (a.ndim - 1))
    out = pl.BlockSpec((1, S, LANES), lambda p, b: (b, 0, p))
    return pl.pallas_call(
        _attention_kernel,
        grid=(N_PAIRS, B),
        in_specs=[slab(0), slab(N_PAIRS), slab(2 * N_PAIRS), per_pair(t12), per_pair(t3),
                  slab(3 * N_PAIRS), slab(4 * N_PAIRS), slab(5 * N_PAIRS), per_pair(col_bias)],
        out_specs=[out, out],
        out_shape=[jax.ShapeDtypeStruct((B, S, WIDTH), BF16)] * 2,
        scratch_shapes=([pltpu.VMEM((col_bias.shape[1], 3, 2 * _NQ_ROWS * _NA_QCOLS,
                                     _NK_ROWS * _NA_KCOLS), F32)]
                        + [f32_buf] * 2 + [f32_buf] * 2 + [bf16_buf] * 6 + [f32_buf] * 10),
        compiler_params=pltpu.CompilerParams(
            dimension_semantics=("arbitrary", "arbitrary"), vmem_limit_bytes=VMEM_LIMIT),
        name="attention",
    )(qkv, qkv, qkv, t12, t3, qkv, qkv, qkv, col_bias)


_GROUP_LANE0 = N_EXPERTS
_ROUTE_GROUP_LANE = EXPERTS_PER_GROUP


def _route(logits):
    tm = logits.shape[0]
    n_rows = _GROUP_LANE0 + 2 * N_GROUPS
    lt = jnp.transpose(logits)[:n_rows, :]
    row_i = lax.broadcasted_iota(jnp.int32, (n_rows, tm), 0)
    row = row_i.astype(F32)
    big = float(LANES)
    is_group = (row_i >= _GROUP_LANE0) & (row_i < _GROUP_LANE0 + N_GROUPS)
    gl = jnp.where(is_group, lt, NEG_INF)
    gmax = jnp.max(gl, axis=0, keepdims=True)
    g_idx = jnp.min(jnp.where(is_group & (gl == gmax), row, big), axis=0, keepdims=True) - _GROUP_LANE0
    g_weight = 1.0 / jnp.sum(jnp.where(is_group, jnp.exp(gl - gmax), 0.0), axis=0, keepdims=True)
    in_group = (row_i < N_EXPERTS) & ((row_i // EXPERTS_PER_GROUP).astype(F32) == g_idx)
    el = jnp.where(in_group, lt, NEG_INF)
    v1 = jnp.max(el, axis=0, keepdims=True)
    i1 = jnp.min(jnp.where(in_group & (el == v1), row, big), axis=0, keepdims=True)
    rest = in_group & (row != i1)
    el2 = jnp.where(rest, lt, NEG_INF)
    v2 = jnp.max(el2, axis=0, keepdims=True)
    i2 = jnp.min(jnp.where(rest & (el2 == v2), row, big), axis=0, keepdims=True)
    e2 = jnp.exp(v2 - v1)
    w1 = g_weight / (1.0 + e2)
    w2 = g_weight * e2 / (1.0 + e2)
    base = g_idx * EXPERTS_PER_GROUP
    rec_i = lax.broadcasted_iota(jnp.int32, (LANES, tm), 0)
    rec = rec_i.astype(F32)
    record = jnp.where(rec == i1 - base, w1,
                       jnp.where(rec == i2 - base, w2,
                                 jnp.where(rec_i == _ROUTE_GROUP_LANE, g_idx, 0.0)))
    return jnp.transpose(record)


def _outproj_kernel(x_ref, yd_ref, yn_ref, gd_ref, gn_ref, wo_ref, gf_ref, wr_ref,
                    br_ref, h_ref, hn_ref, comb_ref):
    yd = _rms(yd_ref[...].astype(F32), gd_ref[...]).astype(BF16)
    yn = _rms(yn_ref[...].astype(F32), gn_ref[...]).astype(BF16)
    h = x_ref[...] + jnp.dot(jnp.concatenate([yd, yn], axis=-1), wo_ref[...], preferred_element_type=F32)
    h_ref[...] = h
    hn = _rms(h, gf_ref[...]).astype(BF16)
    hn_ref[...] = hn
    logits = jnp.dot(hn, wr_ref[...], preferred_element_type=F32) + br_ref[...]
    comb_ref[...] = _route(logits)


def _outproj(x2, yd2, yn2, gd, gn, wo, gf, wr, br, tm):
    N, D = x2.shape
    row = lambda w: pl.BlockSpec((tm, w), lambda i: (i, 0))
    full = lambda a, b: pl.BlockSpec((a, b), lambda i: (0, 0))
    return pl.pallas_call(
        _outproj_kernel,
        grid=(N // tm,),
        in_specs=[row(D), row(WIDTH), row(WIDTH), full(1, WIDTH), full(1, WIDTH), full(2 * WIDTH, D),
                  full(1, D), full(D, LANES), full(1, LANES)],
        out_specs=[row(D), row(D), row(LANES)],
        out_shape=[jax.ShapeDtypeStruct((N, D), F32), jax.ShapeDtypeStruct((N, D), BF16),
                   jax.ShapeDtypeStruct((N, LANES), F32)],
        compiler_params=pltpu.CompilerParams(
            dimension_semantics=("arbitrary",), vmem_limit_bytes=VMEM_LIMIT),
        name="outproj_route",
    )(x2, yd2, yn2, gd, gn, wo, gf, wr, br)


_MOE_TILE = 512
_MOE_CHUNK = 144
_MOE_NCHUNK = _MOE_TILE // _MOE_CHUNK + N_GROUPS
_GROUP_WIDTH = EXPERTS_PER_GROUP * D_EXPERT
_ROUTE_PIECE = 8
_MOE_VMEM_LIMIT = 58 * 1024 * 1024


def _moe_kernel(hn_ref, route_ref, h_ref, wg_ref, wu_ref, wd_ref, gfin_ref, y_ref, ys_ref, xcat_ref):
    T, C = _MOE_TILE, _MOE_CHUNK
    route = route_ref[...]
    lane = lax.broadcasted_iota(jnp.int32, (T, LANES), 1)
    gid = jnp.sum(jnp.where(lane == _ROUTE_GROUP_LANE, route, 0.0), axis=-1, keepdims=True)
    onehot = jnp.where((lane < N_GROUPS) & (lane.astype(F32) == gid), 1.0, 0.0)

    before = (lax.broadcasted_iota(jnp.int32, (LANES, LANES), 1)
              < lax.broadcasted_iota(jnp.int32, (LANES, LANES), 0)).astype(BF16)
    count = jnp.zeros((1, LANES), F32)
    ranks = []
    for blk in range(T // LANES):
        oh = onehot[blk * LANES:(blk + 1) * LANES]
        ranks.append(jnp.dot(before, oh.astype(BF16), preferred_element_type=F32) + count)
        count = count + jnp.sum(oh, axis=0, keepdims=True)
    rank = jnp.concatenate(ranks, axis=0)
    nchunk = jnp.floor((count + (C - 1)) * (1.0 / C)).astype(jnp.int32)
    off1 = nchunk[0, 0]
    off2 = off1 + nchunk[0, 1]
    off3 = off2 + nchunk[0, 2]
    n_used = off3 + nchunk[0, 3]
    start = jnp.where(lane == 1, off1, jnp.where(lane == 2, off2, jnp.where(lane == 3, off3, 0)))
    pos = jnp.sum(onehot * (rank + (start * C).astype(F32)), axis=-1, keepdims=True)
    pos_i = pos.astype(jnp.int32)
    pos_row = jnp.transpose(jnp.broadcast_to(pos, (T, LANES)))[0:1, :].astype(jnp.int32)

    r_hi = route.astype(BF16).astype(F32)
    r_mid = (route - r_hi).astype(BF16).astype(F32)
    r_lo = (route - r_hi - r_mid).astype(BF16).astype(F32)
    packed = r_hi + pltpu.roll(r_mid, _ROUTE_PIECE, axis=1) + pltpu.roll(r_lo, 2 * _ROUTE_PIECE, axis=1)
    xcat_ref[:, :hn_ref.shape[1]] = hn_ref[...]
    xcat_ref[:, hn_ref.shape[1]:] = packed.astype(BF16)

    def chunk_body(c, carry):
        g = ((c >= off1).astype(jnp.int32) + (c >= off2).astype(jnp.int32)
             + (c >= off3).astype(jnp.int32))
        row0 = c * C if isinstance(c, int) else pl.multiple_of(c * C, BF16_SUBLANES)
        sel = (pos_row == row0 + lax.broadcasted_iota(jnp.int32, (C, T), 0)).astype(BF16)
        xr = jnp.dot(sel, xcat_ref[...], preferred_element_type=F32)
        xs = xr[:, :hn_ref.shape[1]].astype(BF16)
        r3 = xr[:, hn_ref.shape[1]:]
        r = (r3 + pltpu.roll(r3, LANES - _ROUTE_PIECE, axis=1)
             + pltpu.roll(r3, LANES - 2 * _ROUTE_PIECE, axis=1))
        clane = lax.broadcasted_iota(jnp.int32, (C, LANES), 1)
        parts = []
        for j in range(EXPERTS_PER_GROUP):
            e = g * EXPERTS_PER_GROUP + j
            gate = jnp.dot(xs, wg_ref[e], preferred_element_type=F32)
            up = jnp.dot(xs, wu_ref[e], preferred_element_type=F32)
            wj = jnp.sum(jnp.where(clane == j, r, 0.0), axis=-1, keepdims=True)
            parts.append((gate / (1.0 + jnp.exp(-gate))) * up * wj)
        act = jnp.concatenate(parts, axis=-1).astype(BF16)
        ys_ref[pl.ds(row0, C), :] = jnp.dot(act, wd_ref[g], preferred_element_type=F32).astype(BF16)
        return carry

    n_main = N_GROUPS * (-(-(T // N_GROUPS) // C))
    n_tail = _MOE_NCHUNK - n_main
    for c in range(n_main):
        chunk_body(c, 0)

    back = (lax.broadcasted_iota(jnp.int32, (T, n_main * C), 1) == pos_i).astype(BF16)
    y_ref[...] = h_ref[...] + jnp.dot(back, ys_ref[:n_main * C, :], preferred_element_type=F32)

    @pl.when(n_used > n_main)
    def _():
        lax.fori_loop(n_main, n_used, chunk_body, 0)

        def zero_body(c, carry):
            ys_ref[pl.ds(pl.multiple_of(c * C, BF16_SUBLANES), C), :] = jnp.zeros((C, ys_ref.shape[1]), BF16)
            return carry

        lax.fori_loop(n_used, _MOE_NCHUNK, zero_body, 0)
        tail = (lax.broadcasted_iota(jnp.int32, (T, n_tail * C), 1) == pos_i - n_main * C).astype(BF16)
        y_ref[...] += jnp.dot(tail, ys_ref[n_main * C:, :], preferred_element_type=F32)

    y_ref[...] = _rms(y_ref[...], gfin_ref[...])


def _moe(hn, route, h, wg, wu, wd, gfin):
    N, D = hn.shape
    T = _MOE_TILE
    row = lambda w: pl.BlockSpec((T, w), lambda i: (i, 0))
    whole = lambda a: pl.BlockSpec(a.shape, lambda i: (0,) * a.ndim)
    return pl.pallas_call(
        _moe_kernel,
        grid=(N // T,),
        in_specs=[row(D), row(LANES), row(D),
                  whole(wg), whole(wu), whole(wd), whole(gfin)],
        out_specs=row(D),
        out_shape=jax.ShapeDtypeStruct((N, D), F32),
        scratch_shapes=[pltpu.VMEM((_MOE_NCHUNK * _MOE_CHUNK, D), BF16), pltpu.VMEM((T, D + LANES), BF16)],
        compiler_params=pltpu.CompilerParams(
            dimension_semantics=("arbitrary",), vmem_limit_bytes=_MOE_VMEM_LIMIT),
        name="moe_grouped",
    )(hn, route, h, wg, wu, wd, gfin)


def kernel(x, norm_mix_g, w_in, rpb, g_out_dil, g_out_na, w_out, norm_ffn_g, w_group, b_group,
           w_router, b_router, w_gate, w_up, w_down, norm_final_g):
    B, S, D = x.shape
    N = B * S
    depth = w_in.shape[0]
    assert depth == 1 and D == D_MODEL and S % (16 * _DQ) == 0

    t12, t3 = _dilated_tables()

    layer = 0
    col_scale = np.ones((6, WIDTH), np.float32)
    col_scale[0] = col_scale[3] = LOG2E * HEAD_DIM ** -0.5
    qkv = _inproj(x, norm_mix_g[layer].reshape(1, D), w_in[layer], jnp.asarray(col_scale.reshape(1, -1)),
                  tm=512)

    y_dil, y_na = _attention(qkv, jnp.asarray(t12), jnp.asarray(t3), _na_column_bias(rpb[layer]))

    n_route = N_EXPERTS + N_GROUPS
    w_r = jnp.concatenate([w_router[layer], w_group[layer], jnp.zeros((D, LANES - n_route), F32)], axis=1)
    b_r = jnp.concatenate([b_router[layer], b_group[layer], jnp.zeros((LANES - n_route,), F32)]).reshape(1, LANES)

    h, hn, route = _outproj(
        x.reshape(N, D), y_dil.reshape(N, WIDTH), y_na.reshape(N, WIDTH),
        g_out_dil[layer].reshape(1, WIDTH), g_out_na[layer].reshape(1, WIDTH),
        w_out[layer].astype(BF16), norm_ffn_g[layer].reshape(1, D), w_r.astype(BF16), b_r, tm=512)

    wd = w_down[layer].astype(BF16).reshape(N_GROUPS, _GROUP_WIDTH, D)
    y = _moe(hn, route, h, w_gate[layer].astype(BF16), w_up[layer].astype(BF16), wd,
             norm_final_g.reshape(1, D))
    return y.reshape(B, S, D)
```

```python
import numpy as np
import jax
import jax.numpy as jnp
from jax import lax
from jax.experimental import pallas as pl
from jax.experimental.pallas import tpu as pltpu

D_MODEL = 1024
HEAD_DIM = 64
N_HEADS = 8
N_PAIRS = N_HEADS // 2
WIDTH = N_HEADS * HEAD_DIM
N_SLABS = 6 * N_PAIRS
DIL_PATTERNS = ((128, 1), (512, 4), (2048, 16))
DIL_RADIUS = 64
GRID_W = 64
NA_WIN_ROWS = 8
NA_WIN_COLS = 16
N_GROUPS = 4
EXPERTS_PER_GROUP = 4
N_EXPERTS = 16
D_EXPERT = 256
RMS_EPS = 1e-6
NEG_INF = -1e30
LOG2E = 1.4426950408889634

LANES = 128
BF16_SUBLANES = 16
VMEM_LIMIT = 48 * 1024 * 1024

F32 = jnp.float32
BF16 = jnp.bfloat16


def _rms(x, gain):
    return x * lax.rsqrt(jnp.mean(x * x, axis=-1, keepdims=True) + RMS_EPS) * gain


def _inproj_kernel(x_ref, g_ref, w_ref, s_ref, o_ref, wb_ref):
    @pl.when((pl.program_id(0) == 0) & (pl.program_id(1) == 0))
    def _():
        for j in range(N_SLABS):
            cols = slice(j * LANES, (j + 1) * LANES)
            wb_ref[:, cols] = (w_ref[:, cols] * s_ref[:, cols]).astype(BF16)

    xn = _rms(x_ref[0], g_ref[...]).astype(BF16)
    chunk = 4 * LANES
    for c in range(N_SLABS * LANES // chunk):
        acc = jnp.dot(xn, wb_ref[:, c * chunk:(c + 1) * chunk], preferred_element_type=F32)
        for j in range(chunk // LANES):
            o_ref[0, c * (chunk // LANES) + j] = acc[:, j * LANES:(j + 1) * LANES].astype(BF16)


def _inproj(x, gain, w, col_scale, tm):
    B, S, D = x.shape
    return pl.pallas_call(
        _inproj_kernel,
        grid=(B, S // tm),
        in_specs=[
            pl.BlockSpec((1, tm, D), lambda b, i: (b, i, 0)),
            pl.BlockSpec((1, D), lambda b, i: (0, 0)),
            pl.BlockSpec((D, N_SLABS * LANES), lambda b, i: (0, 0)),
            pl.BlockSpec((1, N_SLABS * LANES), lambda b, i: (0, 0)),
        ],
        out_specs=pl.BlockSpec((1, N_SLABS, tm, LANES), lambda b, i: (b, 0, i, 0)),
        out_shape=jax.ShapeDtypeStruct((B, N_SLABS, S, LANES), BF16),
        scratch_shapes=[pltpu.VMEM((D, N_SLABS * LANES), BF16)],
        compiler_params=pltpu.CompilerParams(
            dimension_semantics=("arbitrary", "arbitrary"), vmem_limit_bytes=VMEM_LIMIT),
        name="inproj",
    )(x, gain, w, col_scale)


def _attend(qb, kw, vw, bias, sums_on_mxu):
    lane = lax.broadcasted_iota(jnp.int32, qb.shape, 1)
    zero = jnp.zeros_like(qb)
    qq = jnp.concatenate([jnp.where(lane < HEAD_DIM, qb, zero),
                          jnp.where(lane >= HEAD_DIM, qb, zero)], axis=0)
    s = lax.dot_general(qq, kw, (((1,), (1,)), ((), ())), preferred_element_type=F32) + bias
    m = jnp.max(s, axis=-1, keepdims=True)
    p = jnp.exp2(s - m)
    if not sums_on_mxu:
        l = jnp.sum(p, axis=-1, keepdims=True)
        return m, l, jnp.dot(p.astype(BF16), vw, preferred_element_type=F32)
    pv = jnp.dot(p.astype(BF16), jnp.concatenate([vw, jnp.ones_like(vw)], axis=1),
                 preferred_element_type=F32)
    return m, pv[:, LANES:], pv[:, :LANES]


def _merge_heads(top, bottom, q):
    lane = lax.broadcasted_iota(jnp.int32, (q, LANES), 1)
    return jnp.where(lane < HEAD_DIM, jnp.broadcast_to(top, (q, LANES)),
                     jnp.broadcast_to(bottom, (q, LANES)))


_DQ = 128
_DW = 256
_UNROLL = 16


def _dilated_tables():
    slopes = 2.0 ** (-(np.arange(N_HEADS) + 1.0))
    q = np.arange(_DQ)[:, None]
    t12 = np.zeros((N_PAIRS, 2, 3, 2 * _DQ, _DW), np.float32)
    k = np.arange(_DW)[None, :]
    for pat, dil in enumerate((1, 4)):
        for case, off in enumerate((0, _DW // 4, _DW // 2)):
            delta = np.abs(k - (q + off))
            for h in range(N_HEADS):
                tab = np.where(delta <= DIL_RADIUS, -LOG2E * slopes[h] * dil * delta, NEG_INF)
                t12[h // 2, pat, case, (h % 2) * _DQ:(h % 2 + 1) * _DQ] = tab
    t3 = np.zeros((N_PAIRS, 2 * _DQ, _DQ), np.float32)
    delta = np.abs(np.arange(_DQ)[None, :] - q)
    for h in range(N_HEADS):
        t3[h // 2, (h % 2) * _DQ:(h % 2 + 1) * _DQ] = np.where(
            delta <= DIL_RADIUS, -LOG2E * slopes[h] * 16 * delta, NEG_INF)
    return t12, t3


def _dilated_kernel(q_ref, k_ref, v_ref, t12_ref, t3_ref, o_ref,
                    tmp, tmp4, q4, k4, v4, q16, k16, v16,
                    m1, l1, a1, m2, l2, a2, m3, l3, a3, onat):
    S = tmp.shape[0]
    L4, L16 = S // 4, S // 16

    for src, d4, d16 in ((q_ref, q4, q16), (k_ref, k4, k16), (v_ref, v4, v16)):
        tmp[...] = src[0, 0].astype(F32)
        for r in range(4):
            sub = tmp[pl.ds(r, L4, stride=4), :]
            tmp4[r * L4:(r + 1) * L4, :] = sub
            d4[r * L4:(r + 1) * L4, :] = sub.astype(BF16)
        for r16 in range(16):
            r4, c4 = r16 % 4, r16 // 4
            d16[r16 * L16:(r16 + 1) * L16, :] = tmp4[pl.ds(r4 * L4 + c4, L16, stride=4), :].astype(BF16)

    def block(qb, kw, vw, bias, m_ref, l_ref, a_ref, row):
        m, l, pv = _attend(qb, kw, vw, bias, sums_on_mxu=False)
        m_ref[pl.ds(row, _DQ), :] = _merge_heads(m[:_DQ], m[_DQ:], _DQ)
        l_ref[pl.ds(row, _DQ), :] = _merge_heads(l[:_DQ], l[_DQ:], _DQ)
        a_ref[pl.ds(row, _DQ), :] = _merge_heads(pv[:_DQ], pv[_DQ:], _DQ)

    def case_of(blk, n_blk):
        return jnp.where(blk == 0, 0, jnp.where(blk == n_blk - 1, 2, 1))

    n1 = S // _DQ

    def p1_body(blk, carry):
        t0 = pl.multiple_of(blk * _DQ, _DQ)
        ws = pl.multiple_of(jnp.clip(t0 - DIL_RADIUS, 0, S - _DW), DIL_RADIUS)
        block(q_ref[0, 0, pl.ds(t0, _DQ), :], k_ref[0, 0, pl.ds(ws, _DW), :],
              v_ref[0, 0, pl.ds(ws, _DW), :], t12_ref[0, 0, case_of(blk, n1)], m1, l1, a1, t0)
        return carry

    lax.fori_loop(0, n1, p1_body, 0, unroll=_UNROLL)

    n2 = L4 // _DQ

    def p2_body(j, carry):
        r = j // n2
        blk = j % n2
        l0 = blk * _DQ
        ws = jnp.clip(l0 - DIL_RADIUS, 0, L4 - _DW)
        row = pl.multiple_of(r * L4 + l0, _DQ)
        krow = pl.multiple_of(r * L4 + ws, DIL_RADIUS)
        block(q4[pl.ds(row, _DQ), :], k4[pl.ds(krow, _DW), :], v4[pl.ds(krow, _DW), :],
              t12_ref[0, 1, case_of(blk, n2)], m2, l2, a2, row)
        return carry

    lax.fori_loop(0, 4 * n2, p2_body, 0, unroll=_UNROLL)

    def p3_body(r, carry):
        row = pl.multiple_of(r * L16, L16)
        block(q16[pl.ds(row, L16), :], k16[pl.ds(row, L16), :], v16[pl.ds(row, L16), :],
              t3_ref[0], m3, l3, a3, row)
        return carry

    lax.fori_loop(0, 16, p3_body, 0, unroll=_UNROLL)

    for r16 in range(16):
        r4, c4 = r16 % 4, r16 // 4
        via4 = pl.ds(r4 * L4 + c4, L16, stride=4)
        via16 = pl.ds(r16 * L16, L16)
        mb, mc = m2[via4, :], m3[via16, :]
        mx = jnp.maximum(mb, mc)
        wb, wc = jnp.exp2(mb - mx), jnp.exp2(mc - mx)
        l2[via4, :] = wb * l2[via4, :] + wc * l3[via16, :]
        a2[via4, :] = wb * a2[via4, :] + wc * a3[via16, :]
        m2[via4, :] = mx
    for r4 in range(4):
        for part in range(L4 // _DQ):
            nat = pl.ds(r4 + 4 * _DQ * part, _DQ, stride=4)
            via4 = pl.ds(r4 * L4 + _DQ * part, _DQ)
            ma, mb = m1[nat, :], m2[via4, :]
            mx = jnp.maximum(ma, mb)
            wa, wb = jnp.exp2(ma - mx), jnp.exp2(mb - mx)
            den = wa * l1[nat, :] + wb * l2[via4, :]
            num = wa * a1[nat, :] + wb * a2[via4, :]
            onat[nat, :] = num / den
    o_ref[0] = onat[...].astype(BF16)


_NQ_ROWS = 4
_NK_ROWS = 12


def _na_row_select(rows):
    n_blk = rows // _NQ_ROWS
    sel = np.full((n_blk, _NQ_ROWS, _NK_ROWS), -1, np.int64)
    for i in range(n_blk):
        kr0 = min(max(_NQ_ROWS * i - NA_WIN_ROWS // 2, 0), rows - _NK_ROWS)
        for a in range(_NQ_ROWS):
            qr = _NQ_ROWS * i + a
            rs = min(max(qr - NA_WIN_ROWS // 2, 0), rows - NA_WIN_ROWS)
            for b in range(_NK_ROWS):
                kr = kr0 + b
                if rs <= kr < rs + NA_WIN_ROWS:
                    sel[i, a, b] = kr - qr + NA_WIN_ROWS - 1
    for i in range(2, n_blk - 1):
        assert np.array_equal(sel[1], sel[i])
    return sel[[0, 1, n_blk - 1]]


_NA_QCOLS = 16
_NA_KCOLS = 32
_NA_BORDER = NA_WIN_COLS // 2
_NA_COLUMN_SETS = tuple(
    (((_NA_BORDER + _NA_QCOLS * j, _NA_QCOLS),), ((_NA_QCOLS * j, _NA_KCOLS),), 0)
    for j in range((GRID_W - 2 * _NA_BORDER) // _NA_QCOLS)
) + ((((0, _NA_BORDER), (GRID_W - _NA_BORDER, _NA_BORDER)),
      ((0, NA_WIN_COLS), (GRID_W - NA_WIN_COLS, NA_WIN_COLS)), 1),)


def _na_column_bias(rpb):
    n_dr, n_dc = 2 * NA_WIN_ROWS - 1, 2 * NA_WIN_COLS - 1
    kinds = []
    for kind in (0, 1):
        sets = [cs for cs in _NA_COLUMN_SETS if cs[2] == kind]
        layouts = []
        for q_runs, k_runs, _ in sets:
            qc = np.concatenate([np.arange(c0, c0 + n) for c0, n in q_runs])[:, None]
            kc = np.concatenate([np.arange(c0, c0 + n) for c0, n in k_runs])[None, :]
            start = np.clip(qc - NA_WIN_COLS // 2, 0, GRID_W - NA_WIN_COLS)
            col_ok = (kc >= start) & (kc < start + NA_WIN_COLS)
            assert (col_ok.sum(axis=1) == NA_WIN_COLS).all()
            layouts.append((col_ok, np.clip(kc - qc + NA_WIN_COLS - 1, 0, n_dc - 1)))
        col_ok, dc = layouts[0]
        assert all(np.array_equal(col_ok, o) and np.array_equal(dc, d) for o, d in layouts)
        onehot = (dc.reshape(1, -1) == np.arange(n_dc)[:, None]).astype(np.float32)
        t = jnp.dot(rpb.astype(F32).reshape(N_HEADS * n_dr, n_dc), onehot, precision=lax.Precision.HIGHEST)
        t = jnp.where(col_ok[None, None], LOG2E * t.reshape(N_HEADS, n_dr, _NA_QCOLS, _NA_KCOLS), NEG_INF)
        kinds.append(jnp.tile(t, (1, 1, 1, LANES // _NA_KCOLS)).reshape(
            N_PAIRS, 2, n_dr, _NA_QCOLS, LANES))
    return jnp.stack(kinds, axis=1)


def _na_expand_table(cb_ref, tab_ref, rows):
    n_blk = rows // _NQ_ROWS
    nq, nk = _NQ_ROWS * _NA_QCOLS, _NK_ROWS * _NA_KCOLS
    sel = _na_row_select(rows)

    @pl.when(pl.program_id(1) == 0)
    def _():
        a_idx = lax.broadcasted_iota(jnp.int32, (nq, nk), 0) // _NA_QCOLS
        b_idx = lax.broadcasted_iota(jnp.int32, (nq, nk), 1) // _NA_KCOLS
        for c, i in enumerate((0, 1, n_blk - 1)):
            kr0 = min(max(_NQ_ROWS * i - NA_WIN_ROWS // 2, 0), rows - _NK_ROWS)
            qr = _NQ_ROWS * i + a_idx
            kr = kr0 + b_idx
            first = jnp.clip(qr - NA_WIN_ROWS // 2, 0, rows - NA_WIN_ROWS)
            dr = jnp.where((kr >= first) & (kr < first + NA_WIN_ROWS), kr - qr + NA_WIN_ROWS - 1, -1)
            for kind in range(cb_ref.shape[1]):
                for h in range(2):
                    out = jnp.full((nq, nk), NEG_INF, F32)
                    for r in sorted(set(int(s) for s in sel[c].reshape(-1)) - {-1}):
                        tile = jnp.tile(cb_ref[0, kind, h, r], (_NQ_ROWS, nk // LANES))
                        out = jnp.where(dr == r, tile, out)
                    tab_ref[kind, c, h * nq:(h + 1) * nq, :] = out


def _na_blocks(q_ref, k_ref, v_ref, tab_ref, o_ref, qf, of):
    S = q_ref.shape[2]
    rows = S // GRID_W
    n_blk = rows // _NQ_ROWS
    nq = _NQ_ROWS * _NA_QCOLS
    lane = lax.broadcasted_iota(jnp.int32, (nq, LANES), 1)
    qf[...] = q_ref[0, 0].astype(F32)
    for i in range(n_blk):
        kr0 = min(max(_NQ_ROWS * i - NA_WIN_ROWS // 2, 0), rows - _NK_ROWS)
        case = 0 if i == 0 else (2 if i == n_blk - 1 else 1)
        for q_runs, k_runs, kind in _NA_COLUMN_SETS:
            q_rows = [((_NQ_ROWS * i + a) * GRID_W + c0, n) for a in range(_NQ_ROWS) for c0, n in q_runs]
            k_rows = [((kr0 + b) * GRID_W + c0, n) for b in range(_NK_ROWS) for c0, n in k_runs]
            qb = jnp.concatenate([qf[r0:r0 + n, :] for r0, n in q_rows], axis=0).astype(BF16)
            kw = jnp.concatenate([k_ref[0, 0, r0:r0 + n, :] for r0, n in k_rows], axis=0)
            vw = jnp.concatenate([v_ref[0, 0, r0:r0 + n, :] for r0, n in k_rows], axis=0)
            m, l, pv = _attend(qb, kw, vw, tab_ref[kind, case], sums_on_mxu=True)
            o = pv / l
            o = jnp.where(lane < HEAD_DIM, o[:nq], o[nq:])
            at = 0
            for r0, n in q_rows:
                of[r0:r0 + n, :] = o[at:at + n]
                at += n
    o_ref[0] = of[...].astype(BF16)


def _attention_kernel(qd_ref, kd_ref, vd_ref, t12_ref, t3_ref, qn_ref, kn_ref, vn_ref, cb_ref,
                      od_ref, on_ref, tab_ref, qf, of, *dilated_scratch):
    _na_expand_table(cb_ref, tab_ref, qn_ref.shape[2] // GRID_W)
    _dilated_kernel(qd_ref, kd_ref, vd_ref, t12_ref, t3_ref, od_ref, *dilated_scratch)
    _na_blocks(qn_ref, kn_ref, vn_ref, tab_ref, on_ref, qf, of)


def _attention(qkv, t12, t3, col_bias):
    B, _, S, _ = qkv.shape
    f32_buf = pltpu.VMEM((S, LANES), F32)
    bf16_buf = pltpu.VMEM((S, LANES), BF16)
    slab = lambda off: pl.BlockSpec((1, 1, S, LANES), lambda p, b: (b, off + p, 0, 0))
    per_pair = lambda a: pl.BlockSpec((1,) + a.shape[1:], lambda p, b: (p,) + (0,) * (a.ndim - 1))
    out = pl.BlockSpec((1, S, LANES), lambda p, b: (b, 0, p))
    return pl.pallas_call(
        _attention_kernel,
        grid=(N_PAIRS, B),
        in_specs=[slab(0), slab(N_PAIRS), slab(2 * N_PAIRS), per_pair(t12), per_pair(t3),
                  slab(3 * N_PAIRS), slab(4 * N_PAIRS), slab(5 * N_PAIRS), per_pair(col_bias)],
        out_specs=[out, out],
        out_shape=[jax.ShapeDtypeStruct((B, S, WIDTH), BF16)] * 2,
        scratch_shapes=([pltpu.VMEM((col_bias.shape[1], 3, 2 * _NQ_ROWS * _NA_QCOLS,
                                     _NK_ROWS * _NA_KCOLS), F32)]
                        + [f32_buf] * 2 + [f32_buf] * 2 + [bf16_buf] * 6 + [f32_buf] * 10),
        compiler_params=pltpu.CompilerParams(
            dimension_semantics=("arbitrary", "arbitrary"), vmem_limit_bytes=VMEM_LIMIT),
        name="attention",
    )(qkv, qkv, qkv, t12, t3, qkv, qkv, qkv, col_bias)


_GROUP_LANE0 = N_EXPERTS
_ROUTE_GROUP_LANE = EXPERTS_PER_GROUP


def _route(logits):
    tm = logits.shape[0]
    n_rows = _GROUP_LANE0 + 2 * N_GROUPS
    lt = jnp.transpose(logits)[:n_rows, :]
    row_i = lax.broadcasted_iota(jnp.int32, (n_rows, tm), 0)
    row = row_i.astype(F32)
    big = float(LANES)
    is_group = (row_i >= _GROUP_LANE0) & (row_i < _GROUP_LANE0 + N_GROUPS)
    gl = jnp.where(is_group, lt, NEG_INF)
    gmax = jnp.max(gl, axis=0, keepdims=True)
    g_idx = jnp.min(jnp.where(is_group & (gl == gmax), row, big), axis=0, keepdims=True) - _GROUP_LANE0
    g_weight = 1.0 / jnp.sum(jnp.where(is_group, jnp.exp(gl - gmax), 0.0), axis=0, keepdims=True)
    in_group = (row_i < N_EXPERTS) & ((row_i // EXPERTS_PER_GROUP).astype(F32) == g_idx)
    el = jnp.where(in_group, lt, NEG_INF)
    v1 = jnp.max(el, axis=0, keepdims=True)
    i1 = jnp.min(jnp.where(in_group & (el == v1), row, big), axis=0, keepdims=True)
    rest = in_group & (row != i1)
    el2 = jnp.where(rest, lt, NEG_INF)
    v2 = jnp.max(el2, axis=0, keepdims=True)
    i2 = jnp.min(jnp.where(rest & (el2 == v2), row, big), axis=0, keepdims=True)
    e2 = jnp.exp(v2 - v1)
    w1 = g_weight / (1.0 + e2)
    w2 = g_weight * e2 / (1.0 + e2)
    base = g_idx * EXPERTS_PER_GROUP
    rec_i = lax.broadcasted_iota(jnp.int32, (LANES, tm), 0)
    rec = rec_i.astype(F32)
    record = jnp.where(rec == i1 - base, w1,
                       jnp.where(rec == i2 - base, w2,
                                 jnp.where(rec_i == _ROUTE_GROUP_LANE, g_idx, 0.0)))
    return jnp.transpose(record)


def _outproj_kernel(x_ref, yd_ref, yn_ref, gd_ref, gn_ref, wo_ref, gf_ref, wr_ref,
                    br_ref, h_ref, hn_ref, comb_ref):
    yd = _rms(yd_ref[...].astype(F32), gd_ref[...]).astype(BF16)
    yn = _rms(yn_ref[...].astype(F32), gn_ref[...]).astype(BF16)
    h = x_ref[...] + jnp.dot(jnp.concatenate([yd, yn], axis=-1), wo_ref[...], preferred_element_type=F32)
    h_ref[...] = h
    hn = _rms(h, gf_ref[...]).astype(BF16)
    hn_ref[...] = hn
    logits = jnp.dot(hn, wr_ref[...], preferred_element_type=F32) + br_ref[...]
    comb_ref[...] = _route(logits)


_MOE_TILE = 512
_MOE_CHUNK = 144
_MOE_NCHUNK = _MOE_TILE // _MOE_CHUNK + N_GROUPS
_GROUP_WIDTH = EXPERTS_PER_GROUP * D_EXPERT
_ROUTE_PIECE = 8
_MOE_VMEM_LIMIT = 58 * 1024 * 1024


def _moe_kernel(hn_ref, route_ref, h_ref, wg_ref, wu_ref, wd_ref, gfin_ref, y_ref, ys_ref, xcat_ref):
    T, C = _MOE_TILE, _MOE_CHUNK
    route = route_ref[...]
    lane = lax.broadcasted_iota(jnp.int32, (T, LANES), 1)
    gid = jnp.sum(jnp.where(lane == _ROUTE_GROUP_LANE, route, 0.0), axis=-1, keepdims=True)
    onehot = jnp.where((lane < N_GROUPS) & (lane.astype(F32) == gid), 1.0, 0.0)

    before = (lax.broadcasted_iota(jnp.int32, (LANES, LANES), 1)
              < lax.broadcasted_iota(jnp.int32, (LANES, LANES), 0)).astype(BF16)
    count = jnp.zeros((1, LANES), F32)
    ranks = []
    for blk in range(T // LANES):
        oh = onehot[blk * LANES:(blk + 1) * LANES]
        ranks.append(jnp.dot(before, oh.astype(BF16), preferred_element_type=F32) + count)
        count = count + jnp.sum(oh, axis=0, keepdims=True)
    rank = jnp.concatenate(ranks, axis=0)
    nchunk = jnp.floor((count + (C - 1)) * (1.0 / C)).astype(jnp.int32)
    off1 = nchunk[0, 0]
    off2 = off1 + nchunk[0, 1]
    off3 = off2 + nchunk[0, 2]
    n_used = off3 + nchunk[0, 3]
    start = jnp.where(lane == 1, off1, jnp.where(lane == 2, off2, jnp.where(lane == 3, off3, 0)))
    pos = jnp.sum(onehot * (rank + (start * C).astype(F32)), axis=-1, keepdims=True)
    pos_i = pos.astype(jnp.int32)
    pos_row = jnp.transpose(jnp.broadcast_to(pos, (T, LANES)))[0:1, :].astype(jnp.int32)

    r_hi = route.astype(BF16).astype(F32)
    r_mid = (route - r_hi).astype(BF16).astype(F32)
    r_lo = (route - r_hi - r_mid).astype(BF16).astype(F32)
    packed = r_hi + pltpu.roll(r_mid, _ROUTE_PIECE, axis=1) + pltpu.roll(r_lo, 2 * _ROUTE_PIECE, axis=1)
    xcat_ref[:, :hn_ref.shape[1]] = hn_ref[...]
    xcat_ref[:, hn_ref.shape[1]:] = packed.astype(BF16)

    def chunk_body(c, carry):
        g = ((c >= off1).astype(jnp.int32) + (c >= off2).astype(jnp.int32)
             + (c >= off3).astype(jnp.int32))
        row0 = c * C if isinstance(c, int) else pl.multiple_of(c * C, BF16_SUBLANES)
        sel = (pos_row == row0 + lax.broadcasted_iota(jnp.int32, (C, T), 0)).astype(BF16)
        xr = jnp.dot(sel, xcat_ref[...], preferred_element_type=F32)
        xs = xr[:, :hn_ref.shape[1]].astype(BF16)
        r3 = xr[:, hn_ref.shape[1]:]
        r = (r3 + pltpu.roll(r3, LANES - _ROUTE_PIECE, axis=1)
             + pltpu.roll(r3, LANES - 2 * _ROUTE_PIECE, axis=1))
        clane = lax.broadcasted_iota(jnp.int32, (C, LANES), 1)
        parts = []
        for j in range(EXPERTS_PER_GROUP):
            e = g * EXPERTS_PER_GROUP + j
            gate = jnp.dot(xs, wg_ref[e], preferred_element_type=F32)
            up = jnp.dot(xs, wu_ref[e], preferred_element_type=F32)
            wj = jnp.sum(jnp.where(clane == j, r, 0.0), axis=-1, keepdims=True)
            parts.append((gate / (1.0 + jnp.exp(-gate))) * up * wj)
        act = jnp.concatenate(parts, axis=-1).astype(BF16)
        ys_ref[pl.ds(row0, C), :] = jnp.dot(act, wd_ref[g], preferred_element_type=F32).astype(BF16)
        return carry

    n_main = N_GROUPS * (-(-(T // N_GROUPS) // C))
    n_tail = _MOE_NCHUNK - n_main
    for c in range(n_main):
        chunk_body(c, 0)

    back = (lax.broadcasted_iota(jnp.int32, (T, n_main * C), 1) == pos_i).astype(BF16)
    y_ref[...] = h_ref[...] + jnp.dot(back, ys_ref[:n_main * C, :], preferred_element_type=F32)

    @pl.when(n_used > n_main)
    def _():
        lax.fori_loop(n_main, n_used, chunk_body, 0)

        def zero_body(c, carry):
            ys_ref[pl.ds(pl.multiple_of(c * C, BF16_SUBLANES), C), :] = jnp.zeros((C, ys_ref.shape[1]), BF16)
            return carry

        lax.fori_loop(n_used, _MOE_NCHUNK, zero_body, 0)
        tail = (lax.broadcasted_iota(jnp.int32, (T, n_tail * C), 1) == pos_i - n_main * C).astype(BF16)
        y_ref[...] += jnp.dot(tail, ys_ref[n_main * C:, :], preferred_element_type=F32)

    y_ref[...] = _rms(y_ref[...], gfin_ref[...])


def _ffn_kernel(x_ref, yd_ref, yn_ref, gd_ref, gn_ref, wo_ref, gf_ref, wr_ref, br_ref,
                wg_ref, wu_ref, wd_ref, gfin_ref, y_ref, ys_ref, xcat_ref, h_buf, hn_buf, route_buf):
    i = pl.program_id(0)
    write_slot = i % 2
    read_slot = 1 - write_slot

    @pl.when(i == 0)
    def _():
        h_buf[1] = jnp.zeros(h_buf.shape[1:], F32)
        hn_buf[1] = jnp.zeros(hn_buf.shape[1:], BF16)
        route_buf[1] = jnp.zeros(route_buf.shape[1:], F32)

    _outproj_kernel(x_ref, yd_ref, yn_ref, gd_ref, gn_ref, wo_ref, gf_ref, wr_ref, br_ref,
                    h_buf.at[write_slot], hn_buf.at[write_slot], route_buf.at[write_slot])
    _moe_kernel(hn_buf.at[read_slot], route_buf.at[read_slot], h_buf.at[read_slot],
                wg_ref, wu_ref, wd_ref, gfin_ref, y_ref, ys_ref, xcat_ref)


def _ffn(x2, yd2, yn2, gd, gn, wo, gf, wr, br, wg, wu, wd, gfin):
    N, D = x2.shape
    T = _MOE_TILE
    n_tiles = N // T
    ahead = lambda w: pl.BlockSpec((T, w), lambda i: (jnp.minimum(i, n_tiles - 1), 0))
    behind = pl.BlockSpec((T, D), lambda i: (jnp.maximum(i - 1, 0), 0))
    whole = lambda a: pl.BlockSpec(a.shape, lambda i: (0,) * a.ndim)
    return pl.pallas_call(
        _ffn_kernel,
        grid=(n_tiles + 1,),
        in_specs=[ahead(D), ahead(WIDTH), ahead(WIDTH)]
                 + [whole(a) for a in (gd, gn, wo, gf, wr, br, wg, wu, wd, gfin)],
        out_specs=behind,
        out_shape=jax.ShapeDtypeStruct((N, D), F32),
        scratch_shapes=[pltpu.VMEM((_MOE_NCHUNK * _MOE_CHUNK, D), BF16), pltpu.VMEM((T, D + LANES), BF16),
                        pltpu.VMEM((2, T, D), F32), pltpu.VMEM((2, T, D), BF16),
                        pltpu.VMEM((2, T, LANES), F32)],
        compiler_params=pltpu.CompilerParams(
            dimension_semantics=("arbitrary",), vmem_limit_bytes=_MOE_VMEM_LIMIT),
        name="outproj_moe",
    )(x2, yd2, yn2, gd, gn, wo, gf, wr, br, wg, wu, wd, gfin)


def kernel(x, norm_mix_g, w_in, rpb, g_out_dil, g_out_na, w_out, norm_ffn_g, w_group, b_group,
           w_router, b_router, w_gate, w_up, w_down, norm_final_g):
    B, S, D = x.shape
    N = B * S
    depth = w_in.shape[0]
    assert depth == 1 and D == D_MODEL and S % (16 * _DQ) == 0

    t12, t3 = _dilated_tables()

    layer = 0
    col_scale = np.ones((6, WIDTH), np.float32)
    col_scale[0] = col_scale[3] = LOG2E * HEAD_DIM ** -0.5
    qkv = _inproj(x, norm_mix_g[layer].reshape(1, D), w_in[layer], jnp.asarray(col_scale.reshape(1, -1)),
                  tm=512)

    y_dil, y_na = _attention(qkv, jnp.asarray(t12), jnp.asarray(t3), _na_column_bias(rpb[layer]))

    n_route = N_EXPERTS + N_GROUPS
    w_r = jnp.concatenate([w_router[layer], w_group[layer], jnp.zeros((D, LANES - n_route), F32)], axis=1)
    b_r = jnp.concatenate([b_router[layer], b_group[layer], jnp.zeros((LANES - n_route,), F32)]).reshape(1, LANES)

    wd = w_down[layer].astype(BF16).reshape(N_GROUPS, _GROUP_WIDTH, D)
    y = _ffn(x.reshape(N, D), y_dil.reshape(N, WIDTH), y_na.reshape(N, WIDTH),
             g_out_dil[layer].reshape(1, WIDTH), g_out_na[layer].reshape(1, WIDTH),
             w_out[layer].astype(BF16), norm_ffn_g[layer].reshape(1, D), w_r.astype(BF16), b_r,
             w_gate[layer].astype(BF16), w_up[layer].astype(BF16), wd, norm_final_g.reshape(1, D))
    return y.reshape(B, S, D)
```

```python
import numpy as np
import jax
import jax.numpy as jnp
from jax import lax
from jax.experimental import pallas as pl
from jax.experimental.pallas import tpu as pltpu

D_MODEL = 1024
HEAD_DIM = 64
N_HEADS = 8
N_PAIRS = N_HEADS // 2
WIDTH = N_HEADS * HEAD_DIM
N_SLABS = 6 * N_PAIRS
DIL_PATTERNS = ((128, 1), (512, 4), (2048, 16))
DIL_RADIUS = 64
GRID_W = 64
NA_WIN_ROWS = 8
NA_WIN_COLS = 16
N_GROUPS = 4
EXPERTS_PER_GROUP = 4
N_EXPERTS = 16
D_EXPERT = 256
RMS_EPS = 1e-6
NEG_INF = -1e30
LOG2E = 1.4426950408889634

LANES = 128
BF16_SUBLANES = 16
VMEM_LIMIT = 48 * 1024 * 1024

F32 = jnp.float32
BF16 = jnp.bfloat16


def _rms(x, gain):
    return x * lax.rsqrt(jnp.mean(x * x, axis=-1, keepdims=True) + RMS_EPS) * gain


def _inproj_kernel(x_ref, g_ref, w_ref, s_ref, o_ref, wb_ref):
    @pl.when((pl.program_id(0) == 0) & (pl.program_id(1) == 0))
    def _():
        for j in range(N_SLABS):
            cols = slice(j * LANES, (j + 1) * LANES)
            wb_ref[:, cols] = (w_ref[:, cols] * s_ref[:, cols]).astype(BF16)

    xn = _rms(x_ref[0], g_ref[...]).astype(BF16)
    chunk = 4 * LANES
    for c in range(N_SLABS * LANES // chunk):
        acc = jnp.dot(xn, wb_ref[:, c * chunk:(c + 1) * chunk], preferred_element_type=F32)
        for j in range(chunk // LANES):
            o_ref[0, c * (chunk // LANES) + j] = acc[:, j * LANES:(j + 1) * LANES].astype(BF16)


def _inproj(x, gain, w, col_scale, tm):
    B, S, D = x.shape
    return pl.pallas_call(
        _inproj_kernel,
        grid=(B, S // tm),
        in_specs=[
            pl.BlockSpec((1, tm, D), lambda b, i: (b, i, 0)),
            pl.BlockSpec((1, D), lambda b, i: (0, 0)),
            pl.BlockSpec((D, N_SLABS * LANES), lambda b, i: (0, 0)),
            pl.BlockSpec((1, N_SLABS * LANES), lambda b, i: (0, 0)),
        ],
        out_specs=pl.BlockSpec((1, N_SLABS, tm, LANES), lambda b, i: (b, 0, i, 0)),
        out_shape=jax.ShapeDtypeStruct((B, N_SLABS, S, LANES), BF16),
        scratch_shapes=[pltpu.VMEM((D, N_SLABS * LANES), BF16)],
        compiler_params=pltpu.CompilerParams(
            dimension_semantics=("arbitrary", "arbitrary"), vmem_limit_bytes=VMEM_LIMIT),
        name="inproj",
    )(x, gain, w, col_scale)


def _attend(qb, kw, vw, bias, sums_on_mxu):
    lane = lax.broadcasted_iota(jnp.int32, qb.shape, 1)
    zero = jnp.zeros_like(qb)
    qq = jnp.concatenate([jnp.where(lane < HEAD_DIM, qb, zero),
                          jnp.where(lane >= HEAD_DIM, qb, zero)], axis=0)
    s = lax.dot_general(qq, kw, (((1,), (1,)), ((), ())), preferred_element_type=F32) + bias
    m = jnp.max(s, axis=-1, keepdims=True)
    p = jnp.exp2(s - m)
    if not sums_on_mxu:
        l = jnp.sum(p, axis=-1, keepdims=True)
        return m, l, jnp.dot(p.astype(BF16), vw, preferred_element_type=F32)
    pv = jnp.dot(p.astype(BF16), jnp.concatenate([vw, jnp.ones_like(vw)], axis=1),
                 preferred_element_type=F32)
    return m, pv[:, LANES:], pv[:, :LANES]


def _merge_heads(top, bottom, q):
    lane = lax.broadcasted_iota(jnp.int32, (q, LANES), 1)
    return jnp.where(lane < HEAD_DIM, jnp.broadcast_to(top, (q, LANES)),
                     jnp.broadcast_to(bottom, (q, LANES)))


_DQ = 128
_DW = 256
_UNROLL = 16


def _dilated_tables():
    slopes = 2.0 ** (-(np.arange(N_HEADS) + 1.0))
    q = np.arange(_DQ)[:, None]
    t12 = np.zeros((N_PAIRS, 2, 3, 2 * _DQ, _DW), np.float32)
    k = np.arange(_DW)[None, :]
    for pat, dil in enumerate((1, 4)):
        for case, off in enumerate((0, _DW // 4, _DW // 2)):
            delta = np.abs(k - (q + off))
            for h in range(N_HEADS):
                tab = np.where(delta <= DIL_RADIUS, -LOG2E * slopes[h] * dil * delta, NEG_INF)
                t12[h // 2, pat, case, (h % 2) * _DQ:(h % 2 + 1) * _DQ] = tab
    t3 = np.zeros((N_PAIRS, 2 * _DQ, _DQ), np.float32)
    delta = np.abs(np.arange(_DQ)[None, :] - q)
    for h in range(N_HEADS):
        t3[h // 2, (h % 2) * _DQ:(h % 2 + 1) * _DQ] = np.where(
            delta <= DIL_RADIUS, -LOG2E * slopes[h] * 16 * delta, NEG_INF)
    return t12, t3


def _dilated_kernel(q_ref, k_ref, v_ref, t12_ref, t3_ref, o_ref,
                    tmp, tmp4, q4, k4, v4, q16, k16, v16,
                    m1, l1, a1, m2, l2, a2, m3, l3, a3, onat):
    S = tmp.shape[0]
    L4, L16 = S // 4, S // 16

    for src, d4, d16 in ((q_ref, q4, q16), (k_ref, k4, k16), (v_ref, v4, v16)):
        tmp[...] = src[0, 0].astype(F32)
        for r in range(4):
            sub = tmp[pl.ds(r, L4, stride=4), :]
            tmp4[r * L4:(r + 1) * L4, :] = sub
            d4[r * L4:(r + 1) * L4, :] = sub.astype(BF16)
        for r16 in range(16):
            r4, c4 = r16 % 4, r16 // 4
            d16[r16 * L16:(r16 + 1) * L16, :] = tmp4[pl.ds(r4 * L4 + c4, L16, stride=4), :].astype(BF16)

    def block(qb, kw, vw, bias, m_ref, l_ref, a_ref, row):
        m, l, pv = _attend(qb, kw, vw, bias, sums_on_mxu=False)
        m_ref[pl.ds(row, _DQ), :] = _merge_heads(m[:_DQ], m[_DQ:], _DQ)
        l_ref[pl.ds(row, _DQ), :] = _merge_heads(l[:_DQ], l[_DQ:], _DQ)
        a_ref[pl.ds(row, _DQ), :] = _merge_heads(pv[:_DQ], pv[_DQ:], _DQ)

    def case_of(blk, n_blk):
        return jnp.where(blk == 0, 0, jnp.where(blk == n_blk - 1, 2, 1))

    n1 = S // _DQ

    def p1_body(blk, carry):
        t0 = pl.multiple_of(blk * _DQ, _DQ)
        ws = pl.multiple_of(jnp.clip(t0 - DIL_RADIUS, 0, S - _DW), DIL_RADIUS)
        block(q_ref[0, 0, pl.ds(t0, _DQ), :], k_ref[0, 0, pl.ds(ws, _DW), :],
              v_ref[0, 0, pl.ds(ws, _DW), :], t12_ref[0, 0, case_of(blk, n1)], m1, l1, a1, t0)
        return carry

    lax.fori_loop(0, n1, p1_body, 0, unroll=_UNROLL)

    n2 = L4 // _DQ

    def p2_body(j, carry):
        r = j // n2
        blk = j % n2
        l0 = blk * _DQ
        ws = jnp.clip(l0 - DIL_RADIUS, 0, L4 - _DW)
        row = pl.multiple_of(r * L4 + l0, _DQ)
        krow = pl.multiple_of(r * L4 + ws, DIL_RADIUS)
        block(q4[pl.ds(row, _DQ), :], k4[pl.ds(krow, _DW), :], v4[pl.ds(krow, _DW), :],
              t12_ref[0, 1, case_of(blk, n2)], m2, l2, a2, row)
        return carry

    lax.fori_loop(0, 4 * n2, p2_body, 0, unroll=_UNROLL)

    def p3_body(r, carry):
        row = pl.multiple_of(r * L16, L16)
        block(q16[pl.ds(row, L16), :], k16[pl.ds(row, L16), :], v16[pl.ds(row, L16), :],
              t3_ref[0], m3, l3, a3, row)
        return carry

    lax.fori_loop(0, 16, p3_body, 0, unroll=_UNROLL)

    for r16 in range(16):
        r4, c4 = r16 % 4, r16 // 4
        via4 = pl.ds(r4 * L4 + c4, L16, stride=4)
        via16 = pl.ds(r16 * L16, L16)
        mb, mc = m2[via4, :], m3[via16, :]
        mx = jnp.maximum(mb, mc)
        wb, wc = jnp.exp2(mb - mx), jnp.exp2(mc - mx)
        l2[via4, :] = wb * l2[via4, :] + wc * l3[via16, :]
        a2[via4, :] = wb * a2[via4, :] + wc * a3[via16, :]
        m2[via4, :] = mx
    for r4 in range(4):
        for part in range(L4 // _DQ):
            nat = pl.ds(r4 + 4 * _DQ * part, _DQ, stride=4)
            via4 = pl.ds(r4 * L4 + _DQ * part, _DQ)
            ma, mb = m1[nat, :], m2[via4, :]
            mx = jnp.maximum(ma, mb)
            wa, wb = jnp.exp2(ma - mx), jnp.exp2(mb - mx)
            den = wa * l1[nat, :] + wb * l2[via4, :]
            num = wa * a1[nat, :] + wb * a2[via4, :]
            onat[nat, :] = num / den
    o_ref[0] = onat[...].astype(BF16)


_NQ_ROWS = 4
_NK_ROWS = 12


def _na_row_select(rows):
    n_blk = rows // _NQ_ROWS
    sel = np.full((n_blk, _NQ_ROWS, _NK_ROWS), -1, np.int64)
    for i in range(n_blk):
        kr0 = min(max(_NQ_ROWS * i - NA_WIN_ROWS // 2, 0), rows - _NK_ROWS)
        for a in range(_NQ_ROWS):
            qr = _NQ_ROWS * i + a
            rs = min(max(qr - NA_WIN_ROWS // 2, 0), rows - NA_WIN_ROWS)
            for b in range(_NK_ROWS):
                kr = kr0 + b
                if rs <= kr < rs + NA_WIN_ROWS:
                    sel[i, a, b] = kr - qr + NA_WIN_ROWS - 1
    for i in range(2, n_blk - 1):
        assert np.array_equal(sel[1], sel[i])
    return sel[[0, 1, n_blk - 1]]


_NA_QCOLS = 16
_NA_KCOLS = 32
_NA_BORDER = NA_WIN_COLS // 2
_NA_COLUMN_SETS = tuple(
    (((_NA_BORDER + _NA_QCOLS * j, _NA_QCOLS),), ((_NA_QCOLS * j, _NA_KCOLS),), 0)
    for j in range((GRID_W - 2 * _NA_BORDER) // _NA_QCOLS)
) + ((((0, _NA_BORDER), (GRID_W - _NA_BORDER, _NA_BORDER)),
      ((0, NA_WIN_COLS), (GRID_W - NA_WIN_COLS, NA_WIN_COLS)), 1),)


def _na_column_bias(rpb):
    n_dr, n_dc = 2 * NA_WIN_ROWS - 1, 2 * NA_WIN_COLS - 1
    kinds = []
    for kind in (0, 1):
        sets = [cs for cs in _NA_COLUMN_SETS if cs[2] == kind]
        layouts = []
        for q_runs, k_runs, _ in sets:
            qc = np.concatenate([np.arange(c0, c0 + n) for c0, n in q_runs])[:, None]
            kc = np.concatenate([np.arange(c0, c0 + n) for c0, n in k_runs])[None, :]
            start = np.clip(qc - NA_WIN_COLS // 2, 0, GRID_W - NA_WIN_COLS)
            col_ok = (kc >= start) & (kc < start + NA_WIN_COLS)
            assert (col_ok.sum(axis=1) == NA_WIN_COLS).all()
            layouts.append((col_ok, np.clip(kc - qc + NA_WIN_COLS - 1, 0, n_dc - 1)))
        col_ok, dc = layouts[0]
        assert all(np.array_equal(col_ok, o) and np.array_equal(dc, d) for o, d in layouts)
        onehot = (dc.reshape(1, -1) == np.arange(n_dc)[:, None]).astype(np.float32)
        t = jnp.dot(rpb.astype(F32).reshape(N_HEADS * n_dr, n_dc), onehot, precision=lax.Precision.HIGHEST)
        t = jnp.where(col_ok[None, None], LOG2E * t.reshape(N_HEADS, n_dr, _NA_QCOLS, _NA_KCOLS), NEG_INF)
        kinds.append(jnp.tile(t, (1, 1, 1, LANES // _NA_KCOLS)).reshape(
            N_PAIRS, 2, n_dr, _NA_QCOLS, LANES))
    return jnp.stack(kinds, axis=1)


def _na_expand_table(cb_ref, tab_ref, rows):
    n_blk = rows // _NQ_ROWS
    nq, nk = _NQ_ROWS * _NA_QCOLS, _NK_ROWS * _NA_KCOLS
    sel = _na_row_select(rows)

    @pl.when(pl.program_id(1) == 0)
    def _():
        a_idx = lax.broadcasted_iota(jnp.int32, (nq, nk), 0) // _NA_QCOLS
        b_idx = lax.broadcasted_iota(jnp.int32, (nq, nk), 1) // _NA_KCOLS
        for c, i in enumerate((0, 1, n_blk - 1)):
            kr0 = min(max(_NQ_ROWS * i - NA_WIN_ROWS // 2, 0), rows - _NK_ROWS)
            qr = _NQ_ROWS * i + a_idx
            kr = kr0 + b_idx
            first = jnp.clip(qr - NA_WIN_ROWS // 2, 0, rows - NA_WIN_ROWS)
            dr = jnp.where((kr >= first) & (kr < first + NA_WIN_ROWS), kr - qr + NA_WIN_ROWS - 1, -1)
            for kind in range(cb_ref.shape[1]):
                for h in range(2):
                    out = jnp.full((nq, nk), NEG_INF, F32)
                    for r in sorted(set(int(s) for s in sel[c].reshape(-1)) - {-1}):
                        tile = jnp.tile(cb_ref[0, kind, h, r], (_NQ_ROWS, nk // LANES))
                        out = jnp.where(dr == r, tile, out)
                    tab_ref[kind, c, h * nq:(h + 1) * nq, :] = out


def _na_blocks(q_ref, k_ref, v_ref, tab_ref, o_ref, qf, of):
    S = q_ref.shape[2]
    rows = S // GRID_W
    n_blk = rows // _NQ_ROWS
    nq = _NQ_ROWS * _NA_QCOLS
    lane = lax.broadcasted_iota(jnp.int32, (nq, LANES), 1)
    qf[...] = q_ref[0, 0].astype(F32)
    for i in range(n_blk):
        kr0 = min(max(_NQ_ROWS * i - NA_WIN_ROWS // 2, 0), rows - _NK_ROWS)
        case = 0 if i == 0 else (2 if i == n_blk - 1 else 1)
        for q_runs, k_runs, kind in _NA_COLUMN_SETS:
            q_rows = [((_NQ_ROWS * i + a) * GRID_W + c0, n) for a in range(_NQ_ROWS) for c0, n in q_runs]
            k_rows = [((kr0 + b) * GRID_W + c0, n) for b in range(_NK_ROWS) for c0, n in k_runs]
            qb = jnp.concatenate([qf[r0:r0 + n, :] for r0, n in q_rows], axis=0).astype(BF16)
            kw = jnp.concatenate([k_ref[0, 0, r0:r0 + n, :] for r0, n in k_rows], axis=0)
            vw = jnp.concatenate([v_ref[0, 0, r0:r0 + n, :] for r0, n in k_rows], axis=0)
            m, l, pv = _attend(qb, kw, vw, tab_ref[kind, case], sums_on_mxu=True)
            o = pv / l
            o = jnp.where(lane < HEAD_DIM, o[:nq], o[nq:])
            at = 0
            for r0, n in q_rows:
                of[r0:r0 + n, :] = o[at:at + n]
                at += n
    o_ref[0] = of[...].astype(BF16)


def _attention_kernel(qd_ref, kd_ref, vd_ref, t12_ref, t3_ref, qn_ref, kn_ref, vn_ref, cb_ref,
                      od_ref, on_ref, tab_ref, qf, of, *dilated_scratch):
    _na_expand_table(cb_ref, tab_ref, qn_ref.shape[2] // GRID_W)
    _na_blocks(qn_ref, kn_ref, vn_ref, tab_ref, on_ref, qf, of)
    _dilated_kernel(qd_ref, kd_ref, vd_ref, t12_ref, t3_ref, od_ref, *dilated_scratch)


def _attention(qkv, t12, t3, col_bias):
    B, _, S, _ = qkv.shape
    f32_buf = pltpu.VMEM((S, LANES), F32)
    bf16_buf = pltpu.VMEM((S, LANES), BF16)
    slab = lambda off: pl.BlockSpec((1, 1, S, LANES), lambda p, b: (b, off + p, 0, 0))
    per_pair = lambda a: pl.BlockSpec((1,) + a.shape[1:], lambda p, b: (p,) + (0,) * (a.ndim - 1))
    out = pl.BlockSpec((1, S, LANES), lambda p, b: (b, 0, p))
    return pl.pallas_call(
        _attention_kernel,
        grid=(N_PAIRS, B),
        in_specs=[slab(0), slab(N_PAIRS), slab(2 * N_PAIRS), per_pair(t12), per_pair(t3),
                  slab(3 * N_PAIRS), slab(4 * N_PAIRS), slab(5 * N_PAIRS), per_pair(col_bias)],
        out_specs=[out, out],
        out_shape=[jax.ShapeDtypeStruct((B, S, WIDTH), BF16)] * 2,
        scratch_shapes=([pltpu.VMEM((col_bias.shape[1], 3, 2 * _NQ_ROWS * _NA_QCOLS,
                                     _NK_ROWS * _NA_KCOLS), F32)]
                        + [f32_buf] * 2 + [f32_buf] * 2 + [bf16_buf] * 6 + [f32_buf] * 10),
        compiler_params=pltpu.CompilerParams(
            dimension_semantics=("arbitrary", "arbitrary"), vmem_limit_bytes=VMEM_LIMIT),
        name="attention",
    )(qkv, qkv, qkv, t12, t3, qkv, qkv, qkv, col_bias)


_GROUP_LANE0 = N_EXPERTS
_ROUTE_GROUP_LANE = EXPERTS_PER_GROUP


def _route(logits):
    tm = logits.shape[0]
    n_rows = _GROUP_LANE0 + 2 * N_GROUPS
    lt = jnp.transpose(logits)[:n_rows, :]
    row_i = lax.broadcasted_iota(jnp.int32, (n_rows, tm), 0)
    row = row_i.astype(F32)
    big = float(LANES)
    is_group = (row_i >= _GROUP_LANE0) & (row_i < _GROUP_LANE0 + N_GROUPS)
    gl = jnp.where(is_group, lt, NEG_INF)
    gmax = jnp.max(gl, axis=0, keepdims=True)
    g_idx = jnp.min(jnp.where(is_group & (gl == gmax), row, big), axis=0, keepdims=True) - _GROUP_LANE0
    g_weight = 1.0 / jnp.sum(jnp.where(is_group, jnp.exp(gl - gmax), 0.0), axis=0, keepdims=True)
    in_group = (row_i < N_EXPERTS) & ((row_i // EXPERTS_PER_GROUP).astype(F32) == g_idx)
    el = jnp.where(in_group, lt, NEG_INF)
    v1 = jnp.max(el, axis=0, keepdims=True)
    i1 = jnp.min(jnp.where(in_group & (el == v1), row, big), axis=0, keepdims=True)
    rest = in_group & (row != i1)
    el2 = jnp.where(rest, lt, NEG_INF)
    v2 = jnp.max(el2, axis=0, keepdims=True)
    i2 = jnp.min(jnp.where(rest & (el2 == v2), row, big), axis=0, keepdims=True)
    e2 = jnp.exp(v2 - v1)
    w1 = g_weight / (1.0 + e2)
    w2 = g_weight * e2 / (1.0 + e2)
    base = g_idx * EXPERTS_PER_GROUP
    rec_i = lax.broadcasted_iota(jnp.int32, (LANES, tm), 0)
    rec = rec_i.astype(F32)
    record = jnp.where(rec == i1 - base, w1,
                       jnp.where(rec == i2 - base, w2,
                                 jnp.where(rec_i == _ROUTE_GROUP_LANE, g_idx, 0.0)))
    return jnp.transpose(record)


def _outproj_kernel(x_ref, yd_ref, yn_ref, gd_ref, gn_ref, wo_ref, gf_ref, wr_ref,
                    br_ref, h_ref, hn_ref, comb_ref):
    yd = _rms(yd_ref[...].astype(F32), gd_ref[...]).astype(BF16)
    yn = _rms(yn_ref[...].astype(F32), gn_ref[...]).astype(BF16)
    h = x_ref[...] + jnp.dot(jnp.concatenate([yd, yn], axis=-1), wo_ref[...], preferred_element_type=F32)
    h_ref[...] = h
    hn = _rms(h, gf_ref[...]).astype(BF16)
    hn_ref[...] = hn
    logits = jnp.dot(hn, wr_ref[...], preferred_element_type=F32) + br_ref[...]
    comb_ref[...] = _route(logits)


def _outproj(x2, yd2, yn2, gd, gn, wo, gf, wr, br, tm):
    N, D = x2.shape
    row = lambda w: pl.BlockSpec((tm, w), lambda i: (i, 0))
    full = lambda a, b: pl.BlockSpec((a, b), lambda i: (0, 0))
    return pl.pallas_call(
        _outproj_kernel,
        grid=(N // tm,),
        in_specs=[row(D), row(WIDTH), row(WIDTH), full(1, WIDTH), full(1, WIDTH), full(2 * WIDTH, D),
                  full(1, D), full(D, LANES), full(1, LANES)],
        out_specs=[row(D), row(D), row(LANES)],
        out_shape=[jax.ShapeDtypeStruct((N, D), F32), jax.ShapeDtypeStruct((N, D), BF16),
                   jax.ShapeDtypeStruct((N, LANES), F32)],
        compiler_params=pltpu.CompilerParams(
            dimension_semantics=("arbitrary",), vmem_limit_bytes=VMEM_LIMIT),
        name="outproj_route",
    )(x2, yd2, yn2, gd, gn, wo, gf, wr, br)


_MOE_TILE = 512
_MOE_CHUNK = 144
_MOE_NCHUNK = _MOE_TILE // _MOE_CHUNK + N_GROUPS
_GROUP_WIDTH = EXPERTS_PER_GROUP * D_EXPERT
_ROUTE_PIECE = 8
_MOE_VMEM_LIMIT = 58 * 1024 * 1024


def _moe_kernel(hn_ref, route_ref, h_ref, wg_ref, wu_ref, wd_ref, gfin_ref, y_ref, ys_ref, xcat_ref):
    T, C = _MOE_TILE, _MOE_CHUNK
    route = route_ref[...]
    lane = lax.broadcasted_iota(jnp.int32, (T, LANES), 1)
    gid = jnp.sum(jnp.where(lane == _ROUTE_GROUP_LANE, route, 0.0), axis=-1, keepdims=True)
    onehot = jnp.where((lane < N_GROUPS) & (lane.astype(F32) == gid), 1.0, 0.0)

    before = (lax.broadcasted_iota(jnp.int32, (LANES, LANES), 1)
              < lax.broadcasted_iota(jnp.int32, (LANES, LANES), 0)).astype(BF16)
    count = jnp.zeros((1, LANES), F32)
    ranks = []
    for blk in range(T // LANES):
        oh = onehot[blk * LANES:(blk + 1) * LANES]
        ranks.append(jnp.dot(before, oh.astype(BF16), preferred_element_type=F32) + count)
        count = count + jnp.sum(oh, axis=0, keepdims=True)
    rank = jnp.concatenate(ranks, axis=0)
    nchunk = jnp.floor((count + (C - 1)) * (1.0 / C)).astype(jnp.int32)
    off1 = nchunk[0, 0]
    off2 = off1 + nchunk[0, 1]
    off3 = off2 + nchunk[0, 2]
    n_used = off3 + nchunk[0, 3]
    start = jnp.where(lane == 1, off1, jnp.where(lane == 2, off2, jnp.where(lane == 3, off3, 0)))
    pos = jnp.sum(onehot * (rank + (start * C).astype(F32)), axis=-1, keepdims=True)
    pos_i = pos.astype(jnp.int32)
    pos_row = jnp.transpose(jnp.broadcast_to(pos, (T, LANES)))[0:1, :].astype(jnp.int32)

    r_hi = route.astype(BF16).astype(F32)
    r_mid = (route - r_hi).astype(BF16).astype(F32)
    r_lo = (route - r_hi - r_mid).astype(BF16).astype(F32)
    packed = r_hi + pltpu.roll(r_mid, _ROUTE_PIECE, axis=1) + pltpu.roll(r_lo, 2 * _ROUTE_PIECE, axis=1)
    xcat_ref[:, :hn_ref.shape[1]] = hn_ref[...]
    xcat_ref[:, hn_ref.shape[1]:] = packed.astype(BF16)

    def chunk_body(c, carry):
        g = ((c >= off1).astype(jnp.int32) + (c >= off2).astype(jnp.int32)
             + (c >= off3).astype(jnp.int32))
        row0 = c * C if isinstance(c, int) else pl.multiple_of(c * C, BF16_SUBLANES)
        sel = (pos_row == row0 + lax.broadcasted_iota(jnp.int32, (C, T), 0)).astype(BF16)
        xr = jnp.dot(sel, xcat_ref[...], preferred_element_type=F32)
        xs = xr[:, :hn_ref.shape[1]].astype(BF16)
        r3 = xr[:, hn_ref.shape[1]:]
        r = (r3 + pltpu.roll(r3, LANES - _ROUTE_PIECE, axis=1)
             + pltpu.roll(r3, LANES - 2 * _ROUTE_PIECE, axis=1))
        clane = lax.broadcasted_iota(jnp.int32, (C, LANES), 1)
        parts = []
        for j in range(EXPERTS_PER_GROUP):
            e = g * EXPERTS_PER_GROUP + j
            gate = jnp.dot(xs, wg_ref[e], preferred_element_type=F32)
            up = jnp.dot(xs, wu_ref[e], preferred_element_type=F32)
            wj = jnp.sum(jnp.where(clane == j, r, 0.0), axis=-1, keepdims=True)
            parts.append((gate / (1.0 + jnp.exp(-gate))) * up * wj)
        act = jnp.concatenate(parts, axis=-1).astype(BF16)
        ys_ref[pl.ds(row0, C), :] = jnp.dot(act, wd_ref[g], preferred_element_type=F32).astype(BF16)
        return carry

    n_main = N_GROUPS * (-(-(T // N_GROUPS) // C))
    n_tail = _MOE_NCHUNK - n_main
    for c in range(n_main):
        chunk_body(c, 0)

    back = (lax.broadcasted_iota(jnp.int32, (T, n_main * C), 1) == pos_i).astype(BF16)
    y_ref[...] = h_ref[...] + jnp.dot(back, ys_ref[:n_main * C, :], preferred_element_type=F32)

    @pl.when(n_used > n_main)
    def _():
        lax.fori_loop(n_main, n_used, chunk_body, 0)

        def zero_body(c, carry):
            ys_ref[pl.ds(pl.multiple_of(c * C, BF16_SUBLANES), C), :] = jnp.zeros((C, ys_ref.shape[1]), BF16)
            return carry

        lax.fori_loop(n_used, _MOE_NCHUNK, zero_body, 0)
        tail = (lax.broadcasted_iota(jnp.int32, (T, n_tail * C), 1) == pos_i - n_main * C).astype(BF16)
        y_ref[...] += jnp.dot(tail, ys_ref[n_main * C:, :], preferred_element_type=F32)

    y_ref[...] = _rms(y_ref[...], gfin_ref[...])


def _moe(hn, route, h, wg, wu, wd, gfin):
    N, D = hn.shape
    T = _MOE_TILE
    row = lambda w: pl.BlockSpec((T, w), lambda i: (i, 0))
    whole = lambda a: pl.BlockSpec(a.shape, lambda i: (0,) * a.ndim)
    return pl.pallas_call(
        _moe_kernel,
        grid=(N // T,),
        in_specs=[row(D), row(LANES), row(D),
                  whole(wg), whole(wu), whole(wd), whole(gfin)],
        out_specs=row(D),
        out_shape=jax.ShapeDtypeStruct((N, D), F32),
        scratch_shapes=[pltpu.VMEM((_MOE_NCHUNK * _MOE_CHUNK, D), BF16), pltpu.VMEM((T, D + LANES), BF16)],
        compiler_params=pltpu.CompilerParams(
            dimension_semantics=("arbitrary",), vmem_limit_bytes=_MOE_VMEM_LIMIT),
        name="moe_grouped",
    )(hn, route, h, wg, wu, wd, gfin)


def kernel(x, norm_mix_g, w_in, rpb, g_out_dil, g_out_na, w_out, norm_ffn_g, w_group, b_group,
           w_router, b_router, w_gate, w_up, w_down, norm_final_g):
    B, S, D = x.shape
    N = B * S
    depth = w_in.shape[0]
    assert depth == 1 and D == D_MODEL and S % (16 * _DQ) == 0

    t12, t3 = _dilated_tables()

    layer = 0
    col_scale = np.ones((6, WIDTH), np.float32)
    col_scale[0] = col_scale[3] = LOG2E * HEAD_DIM ** -0.5
    qkv = _inproj(x, norm_mix_g[layer].reshape(1, D), w_in[layer], jnp.asarray(col_scale.reshape(1, -1)),
                  tm=512)

    y_dil, y_na = _attention(qkv, jnp.asarray(t12), jnp.asarray(t3), _na_column_bias(rpb[layer]))

    n_route = N_EXPERTS + N_GROUPS
    w_r = jnp.concatenate([w_router[layer], w_group[layer], jnp.zeros((D, LANES - n_route), F32)], axis=1)
    b_r = jnp.concatenate([b_router[layer], b_group[layer], jnp.zeros((LANES - n_route,), F32)]).reshape(1, LANES)

    h, hn, route = _outproj(
        x.reshape(N, D), y_dil.reshape(N, WIDTH), y_na.reshape(N, WIDTH),
        g_out_dil[layer].reshape(1, WIDTH), g_out_na[layer].reshape(1, WIDTH),
        w_out[layer].astype(BF16), norm_ffn_g[layer].reshape(1, D), w_r.astype(BF16), b_r, tm=512)

    wd = w_down[layer].astype(BF16).reshape(N_GROUPS, _GROUP_WIDTH, D)
    y = _moe(hn, route, h, w_gate[layer].astype(BF16), w_up[layer].astype(BF16), wd,
             norm_final_g.reshape(1, D))
    return y.reshape(B, S, D)
```

```python
import numpy as np
import jax
import jax.numpy as jnp
from jax import lax
from jax.experimental import pallas as pl
from jax.experimental.pallas import tpu as pltpu

D_MODEL = 1024
HEAD_DIM = 64
N_HEADS = 8
N_PAIRS = N_HEADS // 2
WIDTH = N_HEADS * HEAD_DIM
N_SLABS = 6 * N_PAIRS
DIL_PATTERNS = ((128, 1), (512, 4), (2048, 16))
DIL_RADIUS = 64
GRID_W = 64
NA_WIN_ROWS = 8
NA_WIN_COLS = 16
N_GROUPS = 4
EXPERTS_PER_GROUP = 4
N_EXPERTS = 16
D_EXPERT = 256
RMS_EPS = 1e-6
NEG_INF = -1e30
LOG2E = 1.4426950408889634

LANES = 128
BF16_SUBLANES = 16
VMEM_LIMIT = 48 * 1024 * 1024

F32 = jnp.float32
BF16 = jnp.bfloat16


def _rms(x, gain):
    return x * lax.rsqrt(jnp.mean(x * x, axis=-1, keepdims=True) + RMS_EPS) * gain


def _inproj_kernel(x_ref, g_ref, w_ref, s_ref, o_ref, wb_ref):
    @pl.when((pl.program_id(0) == 0) & (pl.program_id(1) == 0))
    def _():
        for j in range(N_SLABS):
            cols = slice(j * LANES, (j + 1) * LANES)
            wb_ref[:, cols] = (w_ref[:, cols] * s_ref[:, cols]).astype(BF16)

    xn = _rms(x_ref[0], g_ref[...]).astype(BF16)
    chunk = 4 * LANES
    for c in range(N_SLABS * LANES // chunk):
        acc = jnp.dot(xn, wb_ref[:, c * chunk:(c + 1) * chunk], preferred_element_type=F32)
        for j in range(chunk // LANES):
            o_ref[0, c * (chunk // LANES) + j] = acc[:, j * LANES:(j + 1) * LANES].astype(BF16)


def _inproj(x, gain, w, col_scale, tm):
    B, S, D = x.shape
    return pl.pallas_call(
        _inproj_kernel,
        grid=(B, S // tm),
        in_specs=[
            pl.BlockSpec((1, tm, D), lambda b, i: (b, i, 0)),
            pl.BlockSpec((1, D), lambda b, i: (0, 0)),
            pl.BlockSpec((D, N_SLABS * LANES), lambda b, i: (0, 0)),
            pl.BlockSpec((1, N_SLABS * LANES), lambda b, i: (0, 0)),
        ],
        out_specs=pl.BlockSpec((1, N_SLABS, tm, LANES), lambda b, i: (b, 0, i, 0)),
        out_shape=jax.ShapeDtypeStruct((B, N_SLABS, S, LANES), BF16),
        scratch_shapes=[pltpu.VMEM((D, N_SLABS * LANES), BF16)],
        compiler_params=pltpu.CompilerParams(
            dimension_semantics=("arbitrary", "arbitrary"), vmem_limit_bytes=VMEM_LIMIT),
        name="inproj",
    )(x, gain, w, col_scale)


def _attend(qb, kw, vw, bias, sums_on_mxu):
    lane = lax.broadcasted_iota(jnp.int32, qb.shape, 1)
    zero = jnp.zeros_like(qb)
    qq = jnp.concatenate([jnp.where(lane < HEAD_DIM, qb, zero),
                          jnp.where(lane >= HEAD_DIM, qb, zero)], axis=0)
    s = lax.dot_general(qq, kw, (((1,), (1,)), ((), ())), preferred_element_type=F32) + bias
    m = jnp.max(s, axis=-1, keepdims=True)
    p = jnp.exp2(s - m)
    if not sums_on_mxu:
        l = jnp.sum(p, axis=-1, keepdims=True)
        return m, l, jnp.dot(p.astype(BF16), vw, preferred_element_type=F32)
    pv = jnp.dot(p.astype(BF16), jnp.concatenate([vw, jnp.ones_like(vw)], axis=1),
                 preferred_element_type=F32)
    return m, pv[:, LANES:], pv[:, :LANES]


def _merge_heads(top, bottom, q):
    lane = lax.broadcasted_iota(jnp.int32, (q, LANES), 1)
    return jnp.where(lane < HEAD_DIM, jnp.broadcast_to(top, (q, LANES)),
                     jnp.broadcast_to(bottom, (q, LANES)))


_DQ = 128
_DW = 256
_UNROLL = 16


def _dilated_tables():
    slopes = 2.0 ** (-(np.arange(N_HEADS) + 1.0))
    q = np.arange(_DQ)[:, None]
    t12 = np.zeros((N_PAIRS, 2, 3, 2 * _DQ, _DW), np.float32)
    k = np.arange(_DW)[None, :]
    for pat, dil in enumerate((1, 4)):
        for case, off in enumerate((0, _DW // 4, _DW // 2)):
            delta = np.abs(k - (q + off))
            for h in range(N_HEADS):
                tab = np.where(delta <= DIL_RADIUS, -LOG2E * slopes[h] * dil * delta, NEG_INF)
                t12[h // 2, pat, case, (h % 2) * _DQ:(h % 2 + 1) * _DQ] = tab
    t3 = np.zeros((N_PAIRS, 2 * _DQ, _DQ), np.float32)
    delta = np.abs(np.arange(_DQ)[None, :] - q)
    for h in range(N_HEADS):
        t3[h // 2, (h % 2) * _DQ:(h % 2 + 1) * _DQ] = np.where(
            delta <= DIL_RADIUS, -LOG2E * slopes[h] * 16 * delta, NEG_INF)
    return t12, t3


def _dilated_kernel(q_ref, k_ref, v_ref, t12_ref, t3_ref, o_ref,
                    tmp, tmp4, q4, k4, v4, q16, k16, v16,
                    m1, l1, a1, m2, l2, a2, m3, l3, a3, onat):
    S = tmp.shape[0]
    L4, L16 = S // 4, S // 16

    for src, d4, d16 in ((q_ref, q4, q16), (k_ref, k4, k16), (v_ref, v4, v16)):
        tmp[...] = src[0, 0].astype(F32)
        for r in range(4):
            sub = tmp[pl.ds(r, L4, stride=4), :]
            tmp4[r * L4:(r + 1) * L4, :] = sub
            d4[r * L4:(r + 1) * L4, :] = sub.astype(BF16)
        for r16 in range(16):
            r4, c4 = r16 % 4, r16 // 4
            d16[r16 * L16:(r16 + 1) * L16, :] = tmp4[pl.ds(r4 * L4 + c4, L16, stride=4), :].astype(BF16)

    def block(qb, kw, vw, bias, m_ref, l_ref, a_ref, row):
        m, l, pv = _attend(qb, kw, vw, bias, sums_on_mxu=False)
        m_ref[pl.ds(row, _DQ), :] = _merge_heads(m[:_DQ], m[_DQ:], _DQ)
        l_ref[pl.ds(row, _DQ), :] = _merge_heads(l[:_DQ], l[_DQ:], _DQ)
        a_ref[pl.ds(row, _DQ), :] = _merge_heads(pv[:_DQ], pv[_DQ:], _DQ)

    def case_of(blk, n_blk):
        return jnp.where(blk == 0, 0, jnp.where(blk == n_blk - 1, 2, 1))

    n1 = S // _DQ

    def p1_body(blk, carry):
        t0 = pl.multiple_of(blk * _DQ, _DQ)
        ws = pl.multiple_of(jnp.clip(t0 - DIL_RADIUS, 0, S - _DW), DIL_RADIUS)
        block(q_ref[0, 0, pl.ds(t0, _DQ), :], k_ref[0, 0, pl.ds(ws, _DW), :],
              v_ref[0, 0, pl.ds(ws, _DW), :], t12_ref[0, 0, case_of(blk, n1)], m1, l1, a1, t0)
        return carry

    lax.fori_loop(0, n1, p1_body, 0, unroll=_UNROLL)

    n2 = L4 // _DQ

    def p2_body(j, carry):
        r = j // n2
        blk = j % n2
        l0 = blk * _DQ
        ws = jnp.clip(l0 - DIL_RADIUS, 0, L4 - _DW)
        row = pl.multiple_of(r * L4 + l0, _DQ)
        krow = pl.multiple_of(r * L4 + ws, DIL_RADIUS)
        block(q4[pl.ds(row, _DQ), :], k4[pl.ds(krow, _DW), :], v4[pl.ds(krow, _DW), :],
              t12_ref[0, 1, case_of(blk, n2)], m2, l2, a2, row)
        return carry

    lax.fori_loop(0, 4 * n2, p2_body, 0, unroll=_UNROLL)

    def p3_body(r, carry):
        row = pl.multiple_of(r * L16, L16)
        block(q16[pl.ds(row, L16), :], k16[pl.ds(row, L16), :], v16[pl.ds(row, L16), :],
              t3_ref[0], m3, l3, a3, row)
        return carry

    lax.fori_loop(0, 16, p3_body, 0, unroll=_UNROLL)

    for r16 in range(16):
        r4, c4 = r16 % 4, r16 // 4
        via4 = pl.ds(r4 * L4 + c4, L16, stride=4)
        via16 = pl.ds(r16 * L16, L16)
        mb, mc = m2[via4, :], m3[via16, :]
        mx = jnp.maximum(mb, mc)
        wb, wc = jnp.exp2(mb - mx), jnp.exp2(mc - mx)
        l2[via4, :] = wb * l2[via4, :] + wc * l3[via16, :]
        a2[via4, :] = wb * a2[via4, :] + wc * a3[via16, :]
        m2[via4, :] = mx
    for r4 in range(4):
        for part in range(L4 // _DQ):
            nat = pl.ds(r4 + 4 * _DQ * part, _DQ, stride=4)
            via4 = pl.ds(r4 * L4 + _DQ * part, _DQ)
            ma, mb = m1[nat, :], m2[via4, :]
            mx = jnp.maximum(ma, mb)
            wa, wb = jnp.exp2(ma - mx), jnp.exp2(mb - mx)
            den = wa * l1[nat, :] + wb * l2[via4, :]
            num = wa * a1[nat, :] + wb * a2[via4, :]
            onat[nat, :] = num / den
    o_ref[0] = onat[...].astype(BF16)


_NQ_ROWS = 4
_NK_ROWS = 12


def _na_row_select(rows):
    n_blk = rows // _NQ_ROWS
    sel = np.full((n_blk, _NQ_ROWS, _NK_ROWS), -1, np.int64)
    for i in range(n_blk):
        kr0 = min(max(_NQ_ROWS * i - NA_WIN_ROWS // 2, 0), rows - _NK_ROWS)
        for a in range(_NQ_ROWS):
            qr = _NQ_ROWS * i + a
            rs = min(max(qr - NA_WIN_ROWS // 2, 0), rows - NA_WIN_ROWS)
            for b in range(_NK_ROWS):
                kr = kr0 + b
                if rs <= kr < rs + NA_WIN_ROWS:
                    sel[i, a, b] = kr - qr + NA_WIN_ROWS - 1
    for i in range(2, n_blk - 1):
        assert np.array_equal(sel[1], sel[i])
    return sel[[0, 1, n_blk - 1]]


_NA_QCOLS = 16
_NA_KCOLS = 32
_NA_BORDER = NA_WIN_COLS // 2
_NA_COLUMN_SETS = tuple(
    (((_NA_BORDER + _NA_QCOLS * j, _NA_QCOLS),), ((_NA_QCOLS * j, _NA_KCOLS),), 0)
    for j in range((GRID_W - 2 * _NA_BORDER) // _NA_QCOLS)
) + ((((0, _NA_BORDER), (GRID_W - _NA_BORDER, _NA_BORDER)),
      ((0, NA_WIN_COLS), (GRID_W - NA_WIN_COLS, NA_WIN_COLS)), 1),)


def _na_column_bias(rpb):
    n_dr, n_dc = 2 * NA_WIN_ROWS - 1, 2 * NA_WIN_COLS - 1
    kinds = []
    for kind in (0, 1):
        sets = [cs for cs in _NA_COLUMN_SETS if cs[2] == kind]
        layouts = []
        for q_runs, k_runs, _ in sets:
            qc = np.concatenate([np.arange(c0, c0 + n) for c0, n in q_runs])[:, None]
            kc = np.concatenate([np.arange(c0, c0 + n) for c0, n in k_runs])[None, :]
            start = np.clip(qc - NA_WIN_COLS // 2, 0, GRID_W - NA_WIN_COLS)
            col_ok = (kc >= start) & (kc < start + NA_WIN_COLS)
            assert (col_ok.sum(axis=1) == NA_WIN_COLS).all()
            layouts.append((col_ok, np.clip(kc - qc + NA_WIN_COLS - 1, 0, n_dc - 1)))
        col_ok, dc = layouts[0]
        assert all(np.array_equal(col_ok, o) and np.array_equal(dc, d) for o, d in layouts)
        onehot = (dc.reshape(1, -1) == np.arange(n_dc)[:, None]).astype(np.float32)
        t = jnp.dot(rpb.astype(F32).reshape(N_HEADS * n_dr, n_dc), onehot, precision=lax.Precision.HIGHEST)
        t = jnp.where(col_ok[None, None], LOG2E * t.reshape(N_HEADS, n_dr, _NA_QCOLS, _NA_KCOLS), NEG_INF)
        kinds.append(jnp.tile(t, (1, 1, 1, LANES // _NA_KCOLS)).reshape(
            N_PAIRS, 2, n_dr, _NA_QCOLS, LANES))
    return jnp.stack(kinds, axis=1)


def _na_expand_table(cb_ref, tab_ref, rows):
    n_blk = rows // _NQ_ROWS
    nq, nk = _NQ_ROWS * _NA_QCOLS, _NK_ROWS * _NA_KCOLS
    sel = _na_row_select(rows)

    @pl.when(pl.program_id(1) == 0)
    def _():
        a_idx = lax.broadcasted_iota(jnp.int32, (nq, nk), 0) // _NA_QCOLS
        b_idx = lax.broadcasted_iota(jnp.int32, (nq, nk), 1) // _NA_KCOLS
        for c, i in enumerate((0, 1, n_blk - 1)):
            kr0 = min(max(_NQ_ROWS * i - NA_WIN_ROWS // 2, 0), rows - _NK_ROWS)
            qr = _NQ_ROWS * i + a_idx
            kr = kr0 + b_idx
            first = jnp.clip(qr - NA_WIN_ROWS // 2, 0, rows - NA_WIN_ROWS)
            dr = jnp.where((kr >= first) & (kr < first + NA_WIN_ROWS), kr - qr + NA_WIN_ROWS - 1, -1)
            for kind in range(cb_ref.shape[1]):
                for h in range(2):
                    out = jnp.full((nq, nk), NEG_INF, F32)
                    for r in sorted(set(int(s) for s in sel[c].reshape(-1)) - {-1}):
                        tile = jnp.tile(cb_ref[0, kind, h, r], (_NQ_ROWS, nk // LANES))
                        out = jnp.where(dr == r, tile, out)
                    tab_ref[kind, c, h * nq:(h + 1) * nq, :] = out


def _na_blocks(q_ref, k_ref, v_ref, tab_ref, o_ref, qf, of):
    S = q_ref.shape[2]
    rows = S // GRID_W
    n_blk = rows // _NQ_ROWS
    nq = _NQ_ROWS * _NA_QCOLS
    lane = lax.broadcasted_iota(jnp.int32, (nq, LANES), 1)
    qf[...] = q_ref[0, 0].astype(F32)
    for i in range(n_blk):
        kr0 = min(max(_NQ_ROWS * i - NA_WIN_ROWS // 2, 0), rows - _NK_ROWS)
        case = 0 if i == 0 else (2 if i == n_blk - 1 else 1)
        for q_runs, k_runs, kind in _NA_COLUMN_SETS:
            q_rows = [((_NQ_ROWS * i + a) * GRID_W + c0, n) for a in range(_NQ_ROWS) for c0, n in q_runs]
            k_rows = [((kr0 + b) * GRID_W + c0, n) for b in range(_NK_ROWS) for c0, n in k_runs]
            qb = jnp.concatenate([qf[r0:r0 + n, :] for r0, n in q_rows], axis=0).astype(BF16)
            kw = jnp.concatenate([k_ref[0, 0, r0:r0 + n, :] for r0, n in k_rows], axis=0)
            vw = jnp.concatenate([v_ref[0, 0, r0:r0 + n, :] for r0, n in k_rows], axis=0)
            m, l, pv = _attend(qb, kw, vw, tab_ref[kind, case], sums_on_mxu=True)
            o = pv / l
            o = jnp.where(lane < HEAD_DIM, o[:nq], o[nq:])
            at = 0
            for r0, n in q_rows:
                of[r0:r0 + n, :] = o[at:at + n]
                at += n
    o_ref[0] = of[...].astype(BF16)


def _attention_kernel(qd_ref, kd_ref, vd_ref, t12_ref, t3_ref, qn_ref, kn_ref, vn_ref, cb_ref,
                      od_ref, on_ref, tab_ref, qf, of, *dilated_scratch):
    _na_expand_table(cb_ref, tab_ref, qn_ref.shape[2] // GRID_W)
    _dilated_kernel(qd_ref, kd_ref, vd_ref, t12_ref, t3_ref, od_ref, *dilated_scratch)
    _na_blocks(qn_ref, kn_ref, vn_ref, tab_ref, on_ref, qf, of)


def _attention(qkv, t12, t3, col_bias):
    B, _, S, _ = qkv.shape
    f32_buf = pltpu.VMEM((S, LANES), F32)
    bf16_buf = pltpu.VMEM((S, LANES), BF16)
    slab = lambda off: pl.BlockSpec((1, 1, S, LANES), lambda p, b: (b, off + p, 0, 0))
    per_pair = lambda a: pl.BlockSpec((1,) + a.shape[1:], lambda p, b: (p,) + (0,) * (a.ndim - 1))
    out = pl.BlockSpec((1, S, LANES), lambda p, b: (b, 0, p))
    return pl.pallas_call(
        _attention_kernel,
        grid=(N_PAIRS, B),
        in_specs=[slab(0), slab(N_PAIRS), slab(2 * N_PAIRS), per_pair(t12), per_pair(t3),
                  slab(3 * N_PAIRS), slab(4 * N_PAIRS), slab(5 * N_PAIRS), per_pair(col_bias)],
        out_specs=[out, out],
        out_shape=[jax.ShapeDtypeStruct((B, S, WIDTH), BF16)] * 2,
        scratch_shapes=([pltpu.VMEM((col_bias.shape[1], 3, 2 * _NQ_ROWS * _NA_QCOLS,
                                     _NK_ROWS * _NA_KCOLS), F32)]
                        + [f32_buf] * 2 + [f32_buf] * 2 + [bf16_buf] * 6 + [f32_buf] * 10),
        compiler_params=pltpu.CompilerParams(
            dimension_semantics=("arbitrary", "arbitrary"), vmem_limit_bytes=VMEM_LIMIT),
        name="attention",
    )(qkv, qkv, qkv, t12, t3, qkv, qkv, qkv, col_bias)


_GROUP_LANE0 = N_EXPERTS
_ROUTE_GROUP_LANE = EXPERTS_PER_GROUP


def _route(logits):
    tm = logits.shape[0]
    n_rows = _GROUP_LANE0 + 2 * N_GROUPS
    lt = jnp.transpose(logits)[:n_rows, :]
    row_i = lax.broadcasted_iota(jnp.int32, (n_rows, tm), 0)
    row = row_i.astype(F32)
    big = float(LANES)
    is_group = (row_i >= _GROUP_LANE0) & (row_i < _GROUP_LANE0 + N_GROUPS)
    gl = jnp.where(is_group, lt, NEG_INF)
    gmax = jnp.max(gl, axis=0, keepdims=True)
    g_idx = jnp.min(jnp.where(is_group & (gl == gmax), row, big), axis=0, keepdims=True) - _GROUP_LANE0
    g_weight = 1.0 / jnp.sum(jnp.where(is_group, jnp.exp(gl - gmax), 0.0), axis=0, keepdims=True)
    in_group = (row_i < N_EXPERTS) & ((row_i // EXPERTS_PER_GROUP).astype(F32) == g_idx)
    el = jnp.where(in_group, lt, NEG_INF)
    v1 = jnp.max(el, axis=0, keepdims=True)
    i1 = jnp.min(jnp.where(in_group & (el == v1), row, big), axis=0, keepdims=True)
    rest = in_group & (row != i1)
    el2 = jnp.where(rest, lt, NEG_INF)
    v2 = jnp.max(el2, axis=0, keepdims=True)
    i2 = jnp.min(jnp.where(rest & (el2 == v2), row, big), axis=0, keepdims=True)
    e2 = jnp.exp(v2 - v1)
    w1 = g_weight / (1.0 + e2)
    w2 = g_weight * e2 / (1.0 + e2)
    base = g_idx * EXPERTS_PER_GROUP
    rec_i = lax.broadcasted_iota(jnp.int32, (LANES, tm), 0)
    rec = rec_i.astype(F32)
    record = jnp.where(rec == i1 - base, w1,
                       jnp.where(rec == i2 - base, w2,
                                 jnp.where(rec_i == _ROUTE_GROUP_LANE, g_idx, 0.0)))
    return jnp.transpose(record)


def _outproj_kernel(x_ref, yd_ref, yn_ref, gd_ref, gn_ref, wo_ref, gf_ref, wr_ref,
                    br_ref, h_ref, hn_ref, comb_ref):
    yd = _rms(yd_ref[...].astype(F32), gd_ref[...]).astype(BF16)
    yn = _rms(yn_ref[...].astype(F32), gn_ref[...]).astype(BF16)
    h = x_ref[...] + jnp.dot(jnp.concatenate([yd, yn], axis=-1), wo_ref[...], preferred_element_type=F32)
    h_ref[...] = h
    hn = _rms(h, gf_ref[...]).astype(BF16)
    hn_ref[...] = hn
    logits = jnp.dot(hn, wr_ref[...], preferred_element_type=F32) + br_ref[...]
    comb_ref[...] = _route(logits)


def _outproj(x2, yd2, yn2, gd, gn, wo, gf, wr, br, tm):
    N, D = x2.shape
    row = lambda w: pl.BlockSpec((tm, w), lambda i: (i, 0))
    full = lambda a, b: pl.BlockSpec((a, b), lambda i: (0, 0))
    return pl.pallas_call(
        _outproj_kernel,
        grid=(N // tm,),
        in_specs=[row(D), row(WIDTH), row(WIDTH), full(1, WIDTH), full(1, WIDTH), full(2 * WIDTH, D),
                  full(1, D), full(D, LANES), full(1, LANES)],
        out_specs=[row(D), row(D), row(LANES)],
        out_shape=[jax.ShapeDtypeStruct((N, D), F32), jax.ShapeDtypeStruct((N, D), BF16),
                   jax.ShapeDtypeStruct((N, LANES), F32)],
        compiler_params=pltpu.CompilerParams(
            dimension_semantics=("arbitrary",), vmem_limit_bytes=VMEM_LIMIT),
        name="outproj_route",
    )(x2, yd2, yn2, gd, gn, wo, gf, wr, br)


_MOE_TILE = 512
_MOE_CHUNK = 144
_MOE_NCHUNK = _MOE_TILE // _MOE_CHUNK + N_GROUPS
_GROUP_WIDTH = EXPERTS_PER_GROUP * D_EXPERT
_ROUTE_PIECE = 8
_MOE_VMEM_LIMIT = 58 * 1024 * 1024


def _moe_kernel(hn_ref, route_ref, h_ref, wg_ref, wu_ref, wd_ref, gfin_ref, y_ref, ys_ref, xcat_ref):
    T, C = _MOE_TILE, _MOE_CHUNK
    route = route_ref[...]
    lane = lax.broadcasted_iota(jnp.int32, (T, LANES), 1)
    gid = jnp.sum(jnp.where(lane == _ROUTE_GROUP_LANE, route, 0.0), axis=-1, keepdims=True)
    onehot = jnp.where((lane < N_GROUPS) & (lane.astype(F32) == gid), 1.0, 0.0)

    before = (lax.broadcasted_iota(jnp.int32, (LANES, LANES), 1)
              < lax.broadcasted_iota(jnp.int32, (LANES, LANES), 0)).astype(BF16)
    count = jnp.zeros((1, LANES), F32)
    ranks = []
    for blk in range(T // LANES):
        oh = onehot[blk * LANES:(blk + 1) * LANES]
        ranks.append(jnp.dot(before, oh.astype(BF16), preferred_element_type=F32) + count)
        count = count + jnp.sum(oh, axis=0, keepdims=True)
    rank = jnp.concatenate(ranks, axis=0)
    nchunk = jnp.floor((count + (C - 1)) * (1.0 / C)).astype(jnp.int32)
    off1 = nchunk[0, 0]
    off2 = off1 + nchunk[0, 1]
    off3 = off2 + nchunk[0, 2]
    n_used = off3 + nchunk[0, 3]
    start = jnp.where(lane == 1, off1, jnp.where(lane == 2, off2, jnp.where(lane == 3, off3, 0)))
    pos = jnp.sum(onehot * (rank + (start * C).astype(F32)), axis=-1, keepdims=True)
    pos_i = pos.astype(jnp.int32)
    pos_row = jnp.transpose(jnp.broadcast_to(pos, (T, LANES)))[0:1, :].astype(jnp.int32)

    r_hi = route.astype(BF16).astype(F32)
    r_mid = (route - r_hi).astype(BF16).astype(F32)
    r_lo = (route - r_hi - r_mid).astype(BF16).astype(F32)
    packed = r_hi + pltpu.roll(r_mid, _ROUTE_PIECE, axis=1) + pltpu.roll(r_lo, 2 * _ROUTE_PIECE, axis=1)
    xcat_ref[:, :hn_ref.shape[1]] = hn_ref[...]
    xcat_ref[:, hn_ref.shape[1]:] = packed.astype(BF16)

    def chunk_body(c, carry):
        g = ((c >= off1).astype(jnp.int32) + (c >= off2).astype(jnp.int32)
             + (c >= off3).astype(jnp.int32))
        row0 = c * C if isinstance(c, int) else pl.multiple_of(c * C, BF16_SUBLANES)
        sel = (pos_row == row0 + lax.broadcasted_iota(jnp.int32, (C, T), 0)).astype(BF16)
        xr = jnp.dot(sel, xcat_ref[...], preferred_element_type=F32)
        xs = xr[:, :hn_ref.shape[1]].astype(BF16)
        r3 = xr[:, hn_ref.shape[1]:]
        r = (r3 + pltpu.roll(r3, LANES - _ROUTE_PIECE, axis=1)
             + pltpu.roll(r3, LANES - 2 * _ROUTE_PIECE, axis=1))
        clane = lax.broadcasted_iota(jnp.int32, (C, LANES), 1)
        parts = []
        for j in range(EXPERTS_PER_GROUP):
            e = g * EXPERTS_PER_GROUP + j
            gate = jnp.dot(xs, wg_ref[e], preferred_element_type=F32)
            up = jnp.dot(xs, wu_ref[e], preferred_element_type=F32)
            wj = jnp.sum(jnp.where(clane == j, r, 0.0), axis=-1, keepdims=True)
            parts.append((gate / (1.0 + jnp.exp(-gate))) * up * wj)
        act = jnp.concatenate(parts, axis=-1).astype(BF16)
        ys_ref[pl.ds(row0, C), :] = jnp.dot(act, wd_ref[g], preferred_element_type=F32).astype(BF16)
        return carry

    n_main = N_GROUPS * (-(-(T // N_GROUPS) // C))
    n_tail = _MOE_NCHUNK - n_main
    for c in range(n_main):
        chunk_body(c, 0)

    back = (lax.broadcasted_iota(jnp.int32, (T, n_main * C), 1) == pos_i).astype(BF16)
    y_ref[...] = h_ref[...] + jnp.dot(back, ys_ref[:n_main * C, :], preferred_element_type=F32)

    @pl.when(n_used > n_main)
    def _():
        lax.fori_loop(n_main, n_used, chunk_body, 0)

        def zero_body(c, carry):
            ys_ref[pl.ds(pl.multiple_of(c * C, BF16_SUBLANES), C), :] = jnp.zeros((C, ys_ref.shape[1]), BF16)
            return carry

        lax.fori_loop(n_used, _MOE_NCHUNK, zero_body, 0)
        tail = (lax.broadcasted_iota(jnp.int32, (T, n_tail * C), 1) == pos_i - n_main * C).astype(BF16)
        y_ref[...] += jnp.dot(tail, ys_ref[n_main * C:, :], preferred_element_type=F32)

    y_ref[...] = _rms(y_ref[...], gfin_ref[...])


def _moe(hn, route, h, wg, wu, wd, gfin):
    N, D = hn.shape
    T = _MOE_TILE
    row = lambda w: pl.BlockSpec((T, w), lambda i: (i, 0))
    whole = lambda a: pl.BlockSpec(a.shape, lambda i: (0,) * a.ndim)
    return pl.pallas_call(
        _moe_kernel,
        grid=(N // T,),
        in_specs=[row(D), row(LANES), row(D),
                  whole(wg), whole(wu), whole(wd), whole(gfin)],
        out_specs=row(D),
        out_shape=jax.ShapeDtypeStruct((N, D), F32),
        scratch_shapes=[pltpu.VMEM((_MOE_NCHUNK * _MOE_CHUNK, D), BF16), pltpu.VMEM((T, D + LANES), BF16)],
        compiler_params=pltpu.CompilerParams(
            dimension_semantics=("arbitrary",), vmem_limit_bytes=_MOE_VMEM_LIMIT),
        name="moe_grouped",
    )(hn, route, h, wg, wu, wd, gfin)


def kernel(x, norm_mix_g, w_in, rpb, g_out_dil, g_out_na, w_out, norm_ffn_g, w_group, b_group,
           w_router, b_router, w_gate, w_up, w_down, norm_final_g):
    B, S, D = x.shape
    N = B * S
    depth = w_in.shape[0]
    assert depth == 1 and D == D_MODEL and S % (16 * _DQ) == 0

    t12, t3 = _dilated_tables()

    layer = 0
    col_scale = np.ones((6, WIDTH), np.float32)
    col_scale[0] = col_scale[3] = LOG2E * HEAD_DIM ** -0.5
    qkv = _inproj(x, norm_mix_g[layer].reshape(1, D), w_in[layer], jnp.asarray(col_scale.reshape(1, -1)),
                  tm=1024)

    y_dil, y_na = _attention(qkv, jnp.asarray(t12), jnp.asarray(t3), _na_column_bias(rpb[layer]))

    n_route = N_EXPERTS + N_GROUPS
    w_r = jnp.concatenate([w_router[layer], w_group[layer], jnp.zeros((D, LANES - n_route), F32)], axis=1)
    b_r = jnp.concatenate([b_router[layer], b_group[layer], jnp.zeros((LANES - n_route,), F32)]).reshape(1, LANES)

    h, hn, route = _outproj(
        x.reshape(N, D), y_dil.reshape(N, WIDTH), y_na.reshape(N, WIDTH),
        g_out_dil[layer].reshape(1, WIDTH), g_out_na[layer].reshape(1, WIDTH),
        w_out[layer].astype(BF16), norm_ffn_g[layer].reshape(1, D), w_r.astype(BF16), b_r, tm=512)

    wd = w_down[layer].astype(BF16).reshape(N_GROUPS, _GROUP_WIDTH, D)
    y = _moe(hn, route, h, w_gate[layer].astype(BF16), w_up[layer].astype(BF16), wd,
             norm_final_g.reshape(1, D))
    return y.reshape(B, S, D)
```

```python
import numpy as np
import jax
import jax.numpy as jnp
from jax import lax
from jax.experimental import pallas as pl
from jax.experimental.pallas import tpu as pltpu

D_MODEL = 1024
HEAD_DIM = 64
N_HEADS = 8
N_PAIRS = N_HEADS // 2
WIDTH = N_HEADS * HEAD_DIM
N_SLABS = 6 * N_PAIRS
DIL_PATTERNS = ((128, 1), (512, 4), (2048, 16))
DIL_RADIUS = 64
GRID_W = 64
NA_WIN_ROWS = 8
NA_WIN_COLS = 16
N_GROUPS = 4
EXPERTS_PER_GROUP = 4
N_EXPERTS = 16
D_EXPERT = 256
RMS_EPS = 1e-6
NEG_INF = -1e30
LOG2E = 1.4426950408889634

LANES = 128
BF16_SUBLANES = 16
VMEM_LIMIT = 48 * 1024 * 1024

F32 = jnp.float32
BF16 = jnp.bfloat16


def _rms(x, gain):
    return x * lax.rsqrt(jnp.mean(x * x, axis=-1, keepdims=True) + RMS_EPS) * gain


def _inproj_kernel(x_ref, g_ref, w_ref, s_ref, o_ref, wb_ref):
    @pl.when((pl.program_id(0) == 0) & (pl.program_id(1) == 0))
    def _():
        for j in range(N_SLABS):
            cols = slice(j * LANES, (j + 1) * LANES)
            wb_ref[:, cols] = (w_ref[:, cols] * s_ref[:, cols]).astype(BF16)

    xn = _rms(x_ref[0], g_ref[...]).astype(BF16)
    chunk = 4 * LANES
    for c in range(N_SLABS * LANES // chunk):
        acc = jnp.dot(xn, wb_ref[:, c * chunk:(c + 1) * chunk], preferred_element_type=F32)
        for j in range(chunk // LANES):
            o_ref[0, c * (chunk // LANES) + j] = acc[:, j * LANES:(j + 1) * LANES].astype(BF16)


def _inproj(x, gain, w, col_scale, tm):
    B, S, D = x.shape
    return pl.pallas_call(
        _inproj_kernel,
        grid=(B, S // tm),
        in_specs=[
            pl.BlockSpec((1, tm, D), lambda b, i: (b, i, 0)),
            pl.BlockSpec((1, D), lambda b, i: (0, 0)),
            pl.BlockSpec((D, N_SLABS * LANES), lambda b, i: (0, 0)),
            pl.BlockSpec((1, N_SLABS * LANES), lambda b, i: (0, 0)),
        ],
        out_specs=pl.BlockSpec((1, N_SLABS, tm, LANES), lambda b, i: (b, 0, i, 0)),
        out_shape=jax.ShapeDtypeStruct((B, N_SLABS, S, LANES), BF16),
        scratch_shapes=[pltpu.VMEM((D, N_SLABS * LANES), BF16)],
        compiler_params=pltpu.CompilerParams(
            dimension_semantics=("arbitrary", "arbitrary"), vmem_limit_bytes=VMEM_LIMIT),
        name="inproj",
    )(x, gain, w, col_scale)


def _attend(qb, kw, vw, bias, sums_on_mxu):
    lane = lax.broadcasted_iota(jnp.int32, qb.shape, 1)
    zero = jnp.zeros_like(qb)
    qq = jnp.concatenate([jnp.where(lane < HEAD_DIM, qb, zero),
                          jnp.where(lane >= HEAD_DIM, qb, zero)], axis=0)
    s = lax.dot_general(qq, kw, (((1,), (1,)), ((), ())), preferred_element_type=F32) + bias
    m = jnp.max(s, axis=-1, keepdims=True)
    p = jnp.exp2(s - m)
    if not sums_on_mxu:
        l = jnp.sum(p, axis=-1, keepdims=True)
        return m, l, jnp.dot(p.astype(BF16), vw, preferred_element_type=F32)
    pv = jnp.dot(p.astype(BF16), jnp.concatenate([vw, jnp.ones_like(vw)], axis=1),
                 preferred_element_type=F32)
    return m, pv[:, LANES:], pv[:, :LANES]


def _merge_heads(top, bottom, q):
    lane = lax.broadcasted_iota(jnp.int32, (q, LANES), 1)
    return jnp.where(lane < HEAD_DIM, jnp.broadcast_to(top, (q, LANES)),
                     jnp.broadcast_to(bottom, (q, LANES)))


_DQ = 128
_DW = 256
_UNROLL = 16


def _dilated_tables():
    slopes = 2.0 ** (-(np.arange(N_HEADS) + 1.0))
    q = np.arange(_DQ)[:, None]
    t12 = np.zeros((N_PAIRS, 2, 3, 2 * _DQ, _DW), np.float32)
    k = np.arange(_DW)[None, :]
    for pat, dil in enumerate((1, 4)):
        for case, off in enumerate((0, _DW // 4, _DW // 2)):
            delta = np.abs(k - (q + off))
            for h in range(N_HEADS):
                tab = np.where(delta <= DIL_RADIUS, -LOG2E * slopes[h] * dil * delta, NEG_INF)
                t12[h // 2, pat, case, (h % 2) * _DQ:(h % 2 + 1) * _DQ] = tab
    t3 = np.zeros((N_PAIRS, 2 * _DQ, _DQ), np.float32)
    delta = np.abs(np.arange(_DQ)[None, :] - q)
    for h in range(N_HEADS):
        t3[h // 2, (h % 2) * _DQ:(h % 2 + 1) * _DQ] = np.where(
            delta <= DIL_RADIUS, -LOG2E * slopes[h] * 16 * delta, NEG_INF)
    return t12, t3


def _dilated_kernel(q_ref, k_ref, v_ref, t12_ref, t3_ref, o_ref,
                    tmp, tmp4, q4, k4, v4, q16, k16, v16,
                    m1, l1, a1, m2, l2, a2, m3, l3, a3, onat):
    S = tmp.shape[0]
    L4, L16 = S // 4, S // 16

    for src, d4, d16 in ((q_ref, q4, q16), (k_ref, k4, k16), (v_ref, v4, v16)):
        tmp[...] = src[0, 0].astype(F32)
        for r in range(4):
            sub = tmp[pl.ds(r, L4, stride=4), :]
            tmp4[r * L4:(r + 1) * L4, :] = sub
            d4[r * L4:(r + 1) * L4, :] = sub.astype(BF16)
        for r16 in range(16):
            r4, c4 = r16 % 4, r16 // 4
            d16[r16 * L16:(r16 + 1) * L16, :] = tmp4[pl.ds(r4 * L4 + c4, L16, stride=4), :].astype(BF16)

    def block(qb, kw, vw, bias, m_ref, l_ref, a_ref, row):
        m, l, pv = _attend(qb, kw, vw, bias, sums_on_mxu=False)
        m_ref[pl.ds(row, _DQ), :] = _merge_heads(m[:_DQ], m[_DQ:], _DQ)
        l_ref[pl.ds(row, _DQ), :] = _merge_heads(l[:_DQ], l[_DQ:], _DQ)
        a_ref[pl.ds(row, _DQ), :] = _merge_heads(pv[:_DQ], pv[_DQ:], _DQ)

    def case_of(blk, n_blk):
        return jnp.where(blk == 0, 0, jnp.where(blk == n_blk - 1, 2, 1))

    n1 = S // _DQ

    def p1_body(blk, carry):
        t0 = pl.multiple_of(blk * _DQ, _DQ)
        ws = pl.multiple_of(jnp.clip(t0 - DIL_RADIUS, 0, S - _DW), DIL_RADIUS)
        block(q_ref[0, 0, pl.ds(t0, _DQ), :], k_ref[0, 0, pl.ds(ws, _DW), :],
              v_ref[0, 0, pl.ds(ws, _DW), :], t12_ref[0, 0, case_of(blk, n1)], m1, l1, a1, t0)
        return carry

    lax.fori_loop(0, n1, p1_body, 0, unroll=_UNROLL)

    n2 = L4 // _DQ

    def p2_body(j, carry):
        r = j // n2
        blk = j % n2
        l0 = blk * _DQ
        ws = jnp.clip(l0 - DIL_RADIUS, 0, L4 - _DW)
        row = pl.multiple_of(r * L4 + l0, _DQ)
        krow = pl.multiple_of(r * L4 + ws, DIL_RADIUS)
        block(q4[pl.ds(row, _DQ), :], k4[pl.ds(krow, _DW), :], v4[pl.ds(krow, _DW), :],
              t12_ref[0, 1, case_of(blk, n2)], m2, l2, a2, row)
        return carry

    lax.fori_loop(0, 4 * n2, p2_body, 0, unroll=_UNROLL)

    def p3_body(r, carry):
        row = pl.multiple_of(r * L16, L16)
        block(q16[pl.ds(row, L16), :], k16[pl.ds(row, L16), :], v16[pl.ds(row, L16), :],
              t3_ref[0], m3, l3, a3, row)
        return carry

    lax.fori_loop(0, 16, p3_body, 0, unroll=_UNROLL)

    for r16 in range(16):
        r4, c4 = r16 % 4, r16 // 4
        via4 = pl.ds(r4 * L4 + c4, L16, stride=4)
        via16 = pl.ds(r16 * L16, L16)
        mb, mc = m2[via4, :], m3[via16, :]
        mx = jnp.maximum(mb, mc)
        wb, wc = jnp.exp2(mb - mx), jnp.exp2(mc - mx)
        l2[via4, :] = wb * l2[via4, :] + wc * l3[via16, :]
        a2[via4, :] = wb * a2[via4, :] + wc * a3[via16, :]
        m2[via4, :] = mx
    for r4 in range(4):
        for part in range(L4 // _DQ):
            nat = pl.ds(r4 + 4 * _DQ * part, _DQ, stride=4)
            via4 = pl.ds(r4 * L4 + _DQ * part, _DQ)
            ma, mb = m1[nat, :], m2[via4, :]
            mx = jnp.maximum(ma, mb)
            wa, wb = jnp.exp2(ma - mx), jnp.exp2(mb - mx)
            den = wa * l1[nat, :] + wb * l2[via4, :]
            num = wa * a1[nat, :] + wb * a2[via4, :]
            onat[nat, :] = num / den
    o_ref[0] = onat[...].astype(BF16)


_NQ_ROWS = 4
_NK_ROWS = 12


def _na_row_select(rows):
    n_blk = rows // _NQ_ROWS
    sel = np.full((n_blk, _NQ_ROWS, _NK_ROWS), -1, np.int64)
    for i in range(n_blk):
        kr0 = min(max(_NQ_ROWS * i - NA_WIN_ROWS // 2, 0), rows - _NK_ROWS)
        for a in range(_NQ_ROWS):
            qr = _NQ_ROWS * i + a
            rs = min(max(qr - NA_WIN_ROWS // 2, 0), rows - NA_WIN_ROWS)
            for b in range(_NK_ROWS):
                kr = kr0 + b
                if rs <= kr < rs + NA_WIN_ROWS:
                    sel[i, a, b] = kr - qr + NA_WIN_ROWS - 1
    for i in range(2, n_blk - 1):
        assert np.array_equal(sel[1], sel[i])
    return sel[[0, 1, n_blk - 1]]


_NA_QCOLS = 16
_NA_KCOLS = 32
_NA_BORDER = NA_WIN_COLS // 2
_NA_COLUMN_SETS = tuple(
    (((_NA_BORDER + _NA_QCOLS * j, _NA_QCOLS),), ((_NA_QCOLS * j, _NA_KCOLS),), 0)
    for j in range((GRID_W - 2 * _NA_BORDER) // _NA_QCOLS)
) + ((((0, _NA_BORDER), (GRID_W - _NA_BORDER, _NA_BORDER)),
      ((0, NA_WIN_COLS), (GRID_W - NA_WIN_COLS, NA_WIN_COLS)), 1),)


def _na_column_bias(rpb):
    n_dr, n_dc = 2 * NA_WIN_ROWS - 1, 2 * NA_WIN_COLS - 1
    kinds = []
    for kind in (0, 1):
        sets = [cs for cs in _NA_COLUMN_SETS if cs[2] == kind]
        layouts = []
        for q_runs, k_runs, _ in sets:
            qc = np.concatenate([np.arange(c0, c0 + n) for c0, n in q_runs])[:, None]
            kc = np.concatenate([np.arange(c0, c0 + n) for c0, n in k_runs])[None, :]
            start = np.clip(qc - NA_WIN_COLS // 2, 0, GRID_W - NA_WIN_COLS)
            col_ok = (kc >= start) & (kc < start + NA_WIN_COLS)
            assert (col_ok.sum(axis=1) == NA_WIN_COLS).all()
            layouts.append((col_ok, np.clip(kc - qc + NA_WIN_COLS - 1, 0, n_dc - 1)))
        col_ok, dc = layouts[0]
        assert all(np.array_equal(col_ok, o) and np.array_equal(dc, d) for o, d in layouts)
        onehot = (dc.reshape(1, -1) == np.arange(n_dc)[:, None]).astype(np.float32)
        t = jnp.dot(rpb.astype(F32).reshape(N_HEADS * n_dr, n_dc), onehot, precision=lax.Precision.HIGHEST)
        t = jnp.where(col_ok[None, None], LOG2E * t.reshape(N_HEADS, n_dr, _NA_QCOLS, _NA_KCOLS), NEG_INF)
        kinds.append(jnp.tile(t, (1, 1, 1, LANES // _NA_KCOLS)).reshape(
            N_PAIRS, 2, n_dr, _NA_QCOLS, LANES))
    return jnp.stack(kinds, axis=1)


def _na_expand_table(cb_ref, tab_ref, rows):
    n_blk = rows // _NQ_ROWS
    nq, nk = _NQ_ROWS * _NA_QCOLS, _NK_ROWS * _NA_KCOLS
    sel = _na_row_select(rows)

    @pl.when(pl.program_id(1) == 0)
    def _():
        a_idx = lax.broadcasted_iota(jnp.int32, (nq, nk), 0) // _NA_QCOLS
        b_idx = lax.broadcasted_iota(jnp.int32, (nq, nk), 1) // _NA_KCOLS
        for c, i in enumerate((0, 1, n_blk - 1)):
            kr0 = min(max(_NQ_ROWS * i - NA_WIN_ROWS // 2, 0), rows - _NK_ROWS)
            qr = _NQ_ROWS * i + a_idx
            kr = kr0 + b_idx
            first = jnp.clip(qr - NA_WIN_ROWS // 2, 0, rows - NA_WIN_ROWS)
            dr = jnp.where((kr >= first) & (kr < first + NA_WIN_ROWS), kr - qr + NA_WIN_ROWS - 1, -1)
            for kind in range(cb_ref.shape[1]):
                for h in range(2):
                    out = jnp.full((nq, nk), NEG_INF, F32)
                    for r in sorted(set(int(s) for s in sel[c].reshape(-1)) - {-1}):
                        tile = jnp.tile(cb_ref[0, kind, h, r], (_NQ_ROWS, nk // LANES))
                        out = jnp.where(dr == r, tile, out)
                    tab_ref[kind, c, h * nq:(h + 1) * nq, :] = out


def _na_blocks(q_ref, k_ref, v_ref, tab_ref, o_ref, qf, of):
    S = q_ref.shape[2]
    rows = S // GRID_W
    n_blk = rows // _NQ_ROWS
    nq = _NQ_ROWS * _NA_QCOLS
    lane = lax.broadcasted_iota(jnp.int32, (nq, LANES), 1)
    qf[...] = q_ref[0, 0].astype(F32)
    for i in range(n_blk):
        kr0 = min(max(_NQ_ROWS * i - NA_WIN_ROWS // 2, 0), rows - _NK_ROWS)
        case = 0 if i == 0 else (2 if i == n_blk - 1 else 1)
        for q_runs, k_runs, kind in _NA_COLUMN_SETS:
            q_rows = [((_NQ_ROWS * i + a) * GRID_W + c0, n) for a in range(_NQ_ROWS) for c0, n in q_runs]
            k_rows = [((kr0 + b) * GRID_W + c0, n) for b in range(_NK_ROWS) for c0, n in k_runs]
            qb = jnp.concatenate([qf[r0:r0 + n, :] for r0, n in q_rows], axis=0).astype(BF16)
            kw = jnp.concatenate([k_ref[0, 0, r0:r0 + n, :] for r0, n in k_rows], axis=0)
            vw = jnp.concatenate([v_ref[0, 0, r0:r0 + n, :] for r0, n in k_rows], axis=0)
            m, l, pv = _attend(qb, kw, vw, tab_ref[kind, case], sums_on_mxu=True)
            o = pv / l
            o = jnp.where(lane < HEAD_DIM, o[:nq], o[nq:])
            at = 0
            for r0, n in q_rows:
                of[r0:r0 + n, :] = o[at:at + n]
                at += n
    o_ref[0] = of[...].astype(BF16)


def _attention_kernel(qd_ref, kd_ref, vd_ref, t12_ref, t3_ref, qn_ref, kn_ref, vn_ref, cb_ref,
                      od_ref, on_ref, tab_ref, qf, of, *dilated_scratch):
    _na_expand_table(cb_ref, tab_ref, qn_ref.shape[2] // GRID_W)
    _dilated_kernel(qd_ref, kd_ref, vd_ref, t12_ref, t3_ref, od_ref, *dilated_scratch)
    _na_blocks(qn_ref, kn_ref, vn_ref, tab_ref, on_ref, qf, of)


def _attention(qkv, t12, t3, col_bias):
    B, _, S, _ = qkv.shape
    f32_buf = pltpu.VMEM((S, LANES), F32)
    bf16_buf = pltpu.VMEM((S, LANES), BF16)
    slab = lambda off: pl.BlockSpec((1, 1, S, LANES), lambda p, b: (b, off + p, 0, 0))
    per_pair = lambda a: pl.BlockSpec((1,) + a.shape[1:], lambda p, b: (p,) + (0,) * (a.ndim - 1))
    out = pl.BlockSpec((1, S, LANES), lambda p, b: (b, 0, p))
    return pl.pallas_call(
        _attention_kernel,
        grid=(N_PAIRS, B),
        in_specs=[slab(0), slab(N_PAIRS), slab(2 * N_PAIRS), per_pair(t12), per_pair(t3),
                  slab(3 * N_PAIRS), slab(4 * N_PAIRS), slab(5 * N_PAIRS), per_pair(col_bias)],
        out_specs=[out, out],
        out_shape=[jax.ShapeDtypeStruct((B, S, WIDTH), BF16)] * 2,
        scratch_shapes=([pltpu.VMEM((col_bias.shape[1], 3, 2 * _NQ_ROWS * _NA_QCOLS,
                                     _NK_ROWS * _NA_KCOLS), F32)]
                        + [f32_buf] * 2 + [f32_buf] * 2 + [bf16_buf] * 6 + [f32_buf] * 10),
        compiler_params=pltpu.CompilerParams(
            dimension_semantics=("arbitrary", "arbitrary"), vmem_limit_bytes=VMEM_LIMIT),
        name="attention",
    )(qkv, qkv, qkv, t12, t3, qkv, qkv, qkv, col_bias)


_GROUP_LANE0 = N_EXPERTS
_ROUTE_GROUP_LANE = EXPERTS_PER_GROUP


def _route(logits):
    tm = logits.shape[0]
    n_rows = _GROUP_LANE0 + 2 * N_GROUPS
    lt = jnp.transpose(logits)[:n_rows, :]
    row_i = lax.broadcasted_iota(jnp.int32, (n_rows, tm), 0)
    row = row_i.astype(F32)
    big = float(LANES)
    is_group = (row_i >= _GROUP_LANE0) & (row_i < _GROUP_LANE0 + N_GROUPS)
    gl = jnp.where(is_group, lt, NEG_INF)
    gmax = jnp.max(gl, axis=0, keepdims=True)
    g_idx = jnp.min(jnp.where(is_group & (gl == gmax), row, big), axis=0, keepdims=True) - _GROUP_LANE0
    g_weight = 1.0 / jnp.sum(jnp.where(is_group, jnp.exp(gl - gmax), 0.0), axis=0, keepdims=True)
    in_group = (row_i < N_EXPERTS) & ((row_i // EXPERTS_PER_GROUP).astype(F32) == g_idx)
    el = jnp.where(in_group, lt, NEG_INF)
    v1 = jnp.max(el, axis=0, keepdims=True)
    i1 = jnp.min(jnp.where(in_group & (el == v1), row, big), axis=0, keepdims=True)
    rest = in_group & (row != i1)
    el2 = jnp.where(rest, lt, NEG_INF)
    v2 = jnp.max(el2, axis=0, keepdims=True)
    i2 = jnp.min(jnp.where(rest & (el2 == v2), row, big), axis=0, keepdims=True)
    e2 = jnp.exp(v2 - v1)
    w1 = g_weight / (1.0 + e2)
    w2 = g_weight * e2 / (1.0 + e2)
    base = g_idx * EXPERTS_PER_GROUP
    rec_i = lax.broadcasted_iota(jnp.int32, (LANES, tm), 0)
    rec = rec_i.astype(F32)
    record = jnp.where(rec == i1 - base, w1,
                       jnp.where(rec == i2 - base, w2,
                                 jnp.where(rec_i == _ROUTE_GROUP_LANE, g_idx, 0.0)))
    return jnp.transpose(record)


def _outproj_kernel(x_ref, yd_ref, yn_ref, gd_ref, gn_ref, wo_ref, gf_ref, wr_ref,
                    br_ref, h_ref, hn_ref, comb_ref):
    yd = _rms(yd_ref[...].astype(F32), gd_ref[...]).astype(BF16)
    yn = _rms(yn_ref[...].astype(F32), gn_ref[...]).astype(BF16)
    h = x_ref[...] + jnp.dot(jnp.concatenate([yd, yn], axis=-1), wo_ref[...], preferred_element_type=F32)
    h_ref[...] = h
    hn = _rms(h, gf_ref[...]).astype(BF16)
    hn_ref[...] = hn
    logits = jnp.dot(hn, wr_ref[...], preferred_element_type=F32) + br_ref[...]
    comb_ref[...] = _route(logits)


def _outproj(x2, yd2, yn2, gd, gn, wo, gf, wr, br, tm):
    N, D = x2.shape
    row = lambda w: pl.BlockSpec((tm, w), lambda i: (i, 0))
    full = lambda a, b: pl.BlockSpec((a, b), lambda i: (0, 0))
    return pl.pallas_call(
        _outproj_kernel,
        grid=(N // tm,),
        in_specs=[row(D), row(WIDTH), row(WIDTH), full(1, WIDTH), full(1, WIDTH), full(2 * WIDTH, D),
                  full(1, D), full(D, LANES), full(1, LANES)],
        out_specs=[row(D), row(D), row(LANES)],
        out_shape=[jax.ShapeDtypeStruct((N, D), F32), jax.ShapeDtypeStruct((N, D), BF16),
                   jax.ShapeDtypeStruct((N, LANES), F32)],
        compiler_params=pltpu.CompilerParams(
            dimension_semantics=("arbitrary",), vmem_limit_bytes=VMEM_LIMIT),
        name="outproj_route",
    )(x2, yd2, yn2, gd, gn, wo, gf, wr, br)


_MOE_TILE = 512
_MOE_CHUNK = 144
_MOE_NCHUNK = _MOE_TILE // _MOE_CHUNK + N_GROUPS
_GROUP_WIDTH = EXPERTS_PER_GROUP * D_EXPERT
_ROUTE_PIECE = 8
_MOE_VMEM_LIMIT = 58 * 1024 * 1024


def _moe_kernel(hn_ref, route_ref, h_ref, wg_ref, wu_ref, wd_ref, gfin_ref, y_ref, ys_ref, xcat_ref):
    T, C = _MOE_TILE, _MOE_CHUNK
    route = route_ref[...]
    lane = lax.broadcasted_iota(jnp.int32, (T, LANES), 1)
    gid = jnp.sum(jnp.where(lane == _ROUTE_GROUP_LANE, route, 0.0), axis=-1, keepdims=True)
    onehot = jnp.where((lane < N_GROUPS) & (lane.astype(F32) == gid), 1.0, 0.0)

    before = (lax.broadcasted_iota(jnp.int32, (LANES, LANES), 1)
              < lax.broadcasted_iota(jnp.int32, (LANES, LANES), 0)).astype(BF16)
    count = jnp.zeros((1, LANES), F32)
    ranks = []
    for blk in range(T // LANES):
        oh = onehot[blk * LANES:(blk + 1) * LANES]
        ranks.append(jnp.dot(before, oh.astype(BF16), preferred_element_type=F32) + count)
        count = count + jnp.sum(oh, axis=0, keepdims=True)
    rank = jnp.concatenate(ranks, axis=0)
    nchunk = jnp.floor((count + (C - 1)) * (1.0 / C)).astype(jnp.int32)
    off1 = nchunk[0, 0]
    off2 = off1 + nchunk[0, 1]
    off3 = off2 + nchunk[0, 2]
    n_used = off3 + nchunk[0, 3]
    start = jnp.where(lane == 1, off1, jnp.where(lane == 2, off2, jnp.where(lane == 3, off3, 0)))
    pos = jnp.sum(onehot * (rank + (start * C).astype(F32)), axis=-1, keepdims=True)
    pos_i = pos.astype(jnp.int32)
    pos_row = jnp.transpose(jnp.broadcast_to(pos, (T, LANES)))[0:1, :].astype(jnp.int32)

    r_hi = route.astype(BF16).astype(F32)
    r_mid = (route - r_hi).astype(BF16).astype(F32)
    r_lo = (route - r_hi - r_mid).astype(BF16).astype(F32)
    packed = r_hi + pltpu.roll(r_mid, _ROUTE_PIECE, axis=1) + pltpu.roll(r_lo, 2 * _ROUTE_PIECE, axis=1)
    xcat_ref[:, :hn_ref.shape[1]] = hn_ref[...]
    xcat_ref[:, hn_ref.shape[1]:] = packed.astype(BF16)

    def chunk_body(c, carry):
        g = ((c >= off1).astype(jnp.int32) + (c >= off2).astype(jnp.int32)
             + (c >= off3).astype(jnp.int32))
        row0 = c * C if isinstance(c, int) else pl.multiple_of(c * C, BF16_SUBLANES)
        sel = (pos_row == row0 + lax.broadcasted_iota(jnp.int32, (C, T), 0)).astype(BF16)
        xr = jnp.dot(sel, xcat_ref[...], preferred_element_type=F32)
        xs = xr[:, :hn_ref.shape[1]].astype(BF16)
        r3 = xr[:, hn_ref.shape[1]:]
        r = (r3 + pltpu.roll(r3, LANES - _ROUTE_PIECE, axis=1)
             + pltpu.roll(r3, LANES - 2 * _ROUTE_PIECE, axis=1))
        clane = lax.broadcasted_iota(jnp.int32, (C, LANES), 1)
        parts = []
        for j in range(EXPERTS_PER_GROUP):
            e = g * EXPERTS_PER_GROUP + j
            gate = jnp.dot(xs, wg_ref[e], preferred_element_type=F32)
            up = jnp.dot(xs, wu_ref[e], preferred_element_type=F32)
            wj = jnp.sum(jnp.where(clane == j, r, 0.0), axis=-1, keepdims=True)
            parts.append((gate / (1.0 + jnp.exp(-gate))) * up * wj)
        act = jnp.concatenate(parts, axis=-1).astype(BF16)
        ys_ref[pl.ds(row0, C), :] = jnp.dot(act, wd_ref[g], preferred_element_type=F32).astype(BF16)
        return carry

    n_main = N_GROUPS * (-(-(T // N_GROUPS) // C))
    n_tail = _MOE_NCHUNK - n_main
    for c in range(n_main):
        chunk_body(c, 0)

    back = (lax.broadcasted_iota(jnp.int32, (T, n_main * C), 1) == pos_i).astype(BF16)
    y_ref[...] = h_ref[...] + jnp.dot(back, ys_ref[:n_main * C, :], preferred_element_type=F32)

    @pl.when(n_used > n_main)
    def _():
        lax.fori_loop(n_main, n_used, chunk_body, 0)

        def zero_body(c, carry):
            ys_ref[pl.ds(pl.multiple_of(c * C, BF16_SUBLANES), C), :] = jnp.zeros((C, ys_ref.shape[1]), BF16)
            return carry

        lax.fori_loop(n_used, _MOE_NCHUNK, zero_body, 0)
        tail = (lax.broadcasted_iota(jnp.int32, (T, n_tail * C), 1) == pos_i - n_main * C).astype(BF16)
        y_ref[...] += jnp.dot(tail, ys_ref[n_main * C:, :], preferred_element_type=F32)

    y_ref[...] = _rms(y_ref[...], gfin_ref[...])


def _moe(hn, route, h, wg, wu, wd, gfin):
    N, D = hn.shape
    T = _MOE_TILE
    row = lambda w: pl.BlockSpec((T, w), lambda i: (i, 0))
    whole = lambda a: pl.BlockSpec(a.shape, lambda i: (0,) * a.ndim)
    return pl.pallas_call(
        _moe_kernel,
        grid=(N // T,),
        in_specs=[row(D), row(LANES), row(D),
                  whole(wg), whole(wu), whole(wd), whole(gfin)],
        out_specs=row(D),
        out_shape=jax.ShapeDtypeStruct((N, D), F32),
        scratch_shapes=[pltpu.VMEM((_MOE_NCHUNK * _MOE_CHUNK, D), BF16), pltpu.VMEM((T, D + LANES), BF16)],
        compiler_params=pltpu.CompilerParams(
            dimension_semantics=("arbitrary",), vmem_limit_bytes=_MOE_VMEM_LIMIT),
        name="moe_grouped",
    )(hn, route, h, wg, wu, wd, gfin)


def kernel(x, norm_mix_g, w_in, rpb, g_out_dil, g_out_na, w_out, norm_ffn_g, w_group, b_group,
           w_router, b_router, w_gate, w_up, w_down, norm_final_g):
    B, S, D = x.shape
    N = B * S
    depth = w_in.shape[0]
    assert depth == 1 and D == D_MODEL and S % (16 * _DQ) == 0

    t12, t3 = _dilated_tables()

    layer = 0
    col_scale = np.ones((6, WIDTH), np.float32)
    col_scale[0] = col_scale[3] = LOG2E * HEAD_DIM ** -0.5
    qkv = _inproj(x, norm_mix_g[layer].reshape(1, D), w_in[layer], jnp.asarray(col_scale.reshape(1, -1)),
                  tm=1024)

    y_dil, y_na = _attention(qkv, jnp.asarray(t12), jnp.asarray(t3), _na_column_bias(rpb[layer]))

    n_route = N_EXPERTS + N_GROUPS
    w_r = jnp.concatenate([w_router[layer], w_group[layer], jnp.zeros((D, LANES - n_route), F32)], axis=1)
    b_r = jnp.concatenate([b_router[layer], b_group[layer], jnp.zeros((LANES - n_route,), F32)]).reshape(1, LANES)

    h, hn, route = _outproj(
        x.reshape(N, D), y_dil.reshape(N, WIDTH), y_na.reshape(N, WIDTH),
        g_out_dil[layer].reshape(1, WIDTH), g_out_na[layer].reshape(1, WIDTH),
        w_out[layer].astype(BF16), norm_ffn_g[layer].reshape(1, D), w_r.astype(BF16), b_r, tm=1024)

    wd = w_down[layer].astype(BF16).reshape(N_GROUPS, _GROUP_WIDTH, D)
    y = _moe(hn, route, h, w_gate[layer].astype(BF16), w_up[layer].astype(BF16), wd,
             norm_final_g.reshape(1, D))
    return y.reshape(B, S, D)
```
